```python
import jax, jax.numpy as jnp
from jax import lax
import numpy as np

D_MODEL = 2048
BATCH = 4
SEQ = 4096
DEPTH = 4

GRID_W = 64
CTX_LEN = 256

RW_WIDTH = D_MODEL // 4
GQ_WIDTH = D_MODEL // 2
NA_WIDTH = D_MODEL // 4

RW_HEAD_DIM = 64
RW_HEADS = RW_WIDTH // RW_HEAD_DIM
DECAY_LORA = max(32, int(round(1.8 * D_MODEL ** 0.5 / 32)) * 32)
ICLR_LORA = max(32, int(round(1.8 * D_MODEL ** 0.5 / 32)) * 32)
GATE_LORA = max(32, int(round(0.6 * D_MODEL ** 0.8 / 32)) * 32)
RW_GN_EPS = 64e-5
RW_SIZES = (RW_WIDTH, RW_WIDTH, RW_WIDTH, DECAY_LORA, DECAY_LORA, ICLR_LORA, ICLR_LORA, GATE_LORA)
RW_COLS = sum(RW_SIZES)

GQ_HEAD_DIM = 128
GQ_HEADS = GQ_WIDTH // GQ_HEAD_DIM
GQ_GROUP = 4
GQ_KV_HEADS = GQ_HEADS // GQ_GROUP
GQ_KV_WIDTH = GQ_KV_HEADS * GQ_HEAD_DIM
GQ_COLS = GQ_WIDTH + 2 * GQ_KV_WIDTH
ROPE_THETA = 10000.0
Q_BLOCK = 128

NA_HEAD_DIM = 64
NA_HEADS = NA_WIDTH // NA_HEAD_DIM
NA_WIN_ROWS = 8
NA_WIN_COLS = 16
NA_COLS = 3 * NA_WIDTH

GATE_COLS = 3 * D_MODEL
IN_SIZES = (RW_COLS, GQ_COLS, NA_COLS, GATE_COLS)
N_IN = sum(IN_SIZES)

D_FF = -(-8 * D_MODEL // (3 * 256)) * 256
NORM_EPS = 1e-6

kernel_name = 'hybrid_rwkv7_gqa_natten_dit_block'


def rms_norm(x, g):
    xf = x.astype(jnp.float32)
    y = xf * lax.rsqrt(jnp.mean(xf * xf, axis=-1, keepdims=True) + NORM_EPS)
    return (y * g.astype(jnp.float32)).astype(x.dtype)


def split_cols(z, sizes):
    offs = np.cumsum((0,) + tuple(sizes))
    return [z[..., int(offs[i]):int(offs[i + 1])] for i in range(len(sizes))]


def heads(t, n):
    b, s, _ = t.shape
    return t.reshape(b, s, n, -1).transpose(0, 2, 1, 3)


def merge_heads(t):
    b, h, s, d = t.shape
    return t.transpose(0, 2, 1, 3).reshape(b, s, h * d)


def axial_rope(n_tok, head_dim):
    n_freq = head_dim // 4
    inv = ROPE_THETA ** (-jnp.arange(n_freq, dtype=jnp.float32) / n_freq)
    t = jnp.arange(n_tok, dtype=jnp.int32)
    row = (t // GRID_W).astype(jnp.float32)
    col = (t % GRID_W).astype(jnp.float32)
    ang = jnp.concatenate([row[:, None] * inv, col[:, None] * inv], axis=-1)
    return jnp.cos(ang), jnp.sin(ang)


def apply_rope(x, cos, sin):
    x1 = x[..., 0::2].astype(jnp.float32)
    x2 = x[..., 1::2].astype(jnp.float32)
    out = jnp.stack([x1 * cos - x2 * sin, x1 * sin + x2 * cos], axis=-1)
    return out.reshape(x.shape).astype(x.dtype)


def centred_shift(z, mu_prev, mu_next):
    zp = jnp.pad(z, ((0, 0), (1, 0), (0, 0)))[:, :-1]
    zn = jnp.pad(z, ((0, 0), (0, 1), (0, 0)))[:, 1:]
    return z + mu_prev * (zp - z) + mu_next * (zn - z)


def rwkv_inputs(z, w0, w_up, a0, a_up, k_k, k_a):
    r, k, v, wf, wb, af, ab, gd = split_cols(z, RW_SIZES)
    b, t, _ = r.shape
    hd = lambda u: u.reshape(b, t, RW_HEADS, RW_HEAD_DIM).astype(jnp.float32)
    kk = hd(k * k_k)
    kk = kk / jnp.maximum(jnp.sqrt(jnp.sum(kk * kk, axis=-1, keepdims=True)), 1e-12)
    dirs = []
    for d, (wl, al) in enumerate(((wf, af), (wb, ab))):
        w_raw = (w0[d] + jnp.tanh(wl) @ w_up[d]).astype(jnp.float32)
        decay = jnp.exp(-jnp.exp(-jax.nn.softplus(-w_raw) - 0.5))
        a = jax.nn.sigmoid(a0[d] + al @ a_up[d])
        k_d = k * (1 + (a - 1) * k_a)
        dirs.append((hd(decay), hd(k_d), -kk, kk * hd(a)))
    return hd(r), hd(v), dirs, gd


def rwkv7_scan(r, w, k, v, a, b, s0, reverse, emit):
    def step(s, inp):
        w_t, k_t, v_t, a_t, b_t = inp[:5]
        sa = jnp.einsum('bhvk,bhk->bhv', s, a_t)
        s = s * w_t[:, :, None, :] + sa[..., None] * b_t[:, :, None, :] + v_t[..., None] * k_t[:, :, None, :]
        if emit:
            return s, jnp.einsum('bhvk,bhk->bhv', s, inp[5])
        return s, None
    seqs = (w, k, v, a, b, r) if emit else (w, k, v, a, b)
    xs = tuple(jnp.swapaxes(u, 0, 1) for u in seqs)
    s, out = lax.scan(step, s0, xs, reverse=reverse)
    return (jnp.swapaxes(out, 0, 1) if emit else None), s


def rwkv_output(r, v, k_f, k_b, wkv, gd, g_up, r_k, ln_w, ln_b, dtype):
    b, t, h, n = wkv.shape
    mu = jnp.mean(wkv, axis=-1, keepdims=True)
    var = jnp.mean(jnp.square(wkv - mu), axis=-1, keepdims=True)
    y = ((wkv - mu) * lax.rsqrt(var + RW_GN_EPS)).reshape(b, t, h * n) * ln_w + ln_b
    bonus = (jnp.sum(r * k_f * r_k, axis=-1, keepdims=True) + jnp.sum(r * k_b * r_k, axis=-1, keepdims=True)) * v
    y = y + bonus.reshape(b, t, h * n)
    g = jax.nn.sigmoid(gd) @ g_up
    return (y * g).astype(dtype)


def gqa_q(zq, gain, rope):
    q = rms_norm(heads(zq, GQ_HEADS), gain)
    return apply_rope(q, *rope) if rope is not None else q


def gqa_kv(zkv, gain, rope):
    k, v = split_cols(zkv, (GQ_KV_WIDTH, GQ_KV_WIDTH))
    k = rms_norm(heads(k, GQ_KV_HEADS), gain)
    if rope is not None:
        k = apply_rope(k, *rope)
    return k, heads(v, GQ_KV_HEADS)


def dense_attend(q, k, v):
    b, hq, s, dh = q.shape
    hk = k.shape[1]
    qg = q.reshape(b, hk, hq // hk, s, dh)
    sc = jnp.einsum('bhgqd,bhkd->bhgqk', qg, k).astype(jnp.float32) * dh ** -0.5
    pr = jax.nn.softmax(sc, axis=-1).astype(v.dtype)
    o = jnp.einsum('bhgqk,bhkd->bhgqd', pr, v).reshape(b, hq, s, dh)
    return merge_heads(o)


def gqa_latent(q, k, v, kc, vc):
    b, hq, t, dh = q.shape
    g = hq // GQ_KV_HEADS
    nb = t // Q_BLOCK
    qb = jnp.moveaxis(q.reshape(b, GQ_KV_HEADS, g, nb, Q_BLOCK, dh), 3, 0)
    scale = dh ** -0.5
    def block(qi):
        s = jnp.concatenate([jnp.einsum('bhgqd,bhkd->bhgqk', qi, k),
                             jnp.einsum('bhgqd,bhkd->bhgqk', qi, kc)], axis=-1).astype(jnp.float32) * scale
        pr = jax.nn.softmax(s, axis=-1).astype(v.dtype)
        return (jnp.einsum('bhgqk,bhkd->bhgqd', pr[..., :t], v)
                + jnp.einsum('bhgqk,bhkd->bhgqd', pr[..., t:], vc))
    o = lax.map(block, qb)
    o = jnp.moveaxis(o, 0, 3).reshape(b, hq, t, dh)
    return merge_heads(o)


def na_latent(q, k, v, kc, vc, rpb):
    b, h, t, dh = q.shape
    rows = t // GRID_W
    wr = min(NA_WIN_ROWS, rows)
    wc = NA_WIN_COLS
    cols = np.arange(GRID_W)
    cstart = np.clip(cols - wc // 2, 0, GRID_W - wc)
    col_idx = cstart[:, None] + np.arange(wc)[None, :]
    col_off = col_idx - cols[:, None] + (NA_WIN_COLS - 1)
    rpb_cols = rpb[:, :, col_off]
    kg = k.reshape(b, h, rows, GRID_W, dh)
    vg = v.reshape(b, h, rows, GRID_W, dh)
    qg = jnp.moveaxis(q.reshape(b, h, rows, GRID_W, dh), 2, 0)
    scale = dh ** -0.5
    nwin = wr * wc
    def one_row(args):
        i, q_row = args
        rs = jnp.clip(i - wr // 2, 0, rows - wr)
        k_win = lax.dynamic_slice_in_dim(kg, rs, wr, axis=2)[:, :, :, col_idx]
        v_win = lax.dynamic_slice_in_dim(vg, rs, wr, axis=2)[:, :, :, col_idx]
        row_off = rs + jnp.arange(wr) - i + (NA_WIN_ROWS - 1)
        bias = jnp.take(rpb_cols, row_off, axis=1).transpose(0, 2, 1, 3)
        s_lat = jnp.einsum('bhqd,bhrqcd->bhqrc', q_row, k_win).astype(jnp.float32) * scale + bias
        s_ctx = jnp.einsum('bhqd,bhcd->bhqc', q_row, kc).astype(jnp.float32) * scale
        s = jnp.concatenate([s_lat.reshape(b, h, GRID_W, nwin), s_ctx], axis=-1)
        pr = jax.nn.softmax(s, axis=-1).astype(v.dtype)
        p_lat = pr[..., :nwin].reshape(b, h, GRID_W, wr, wc)
        return (jnp.einsum('bhqrc,bhrqcd->bhqd', p_lat, v_win)
                + jnp.einsum('bhqc,bhcd->bhqd', pr[..., nwin:], vc))
    o = lax.map(one_row, (jnp.arange(rows, dtype=jnp.int32), qg))
    o = jnp.moveaxis(o, 0, 2).reshape(b, h, t, dh)
    return merge_heads(o)


def merge_branches(ya, yb, yc, zg, w_br_a, w_br_b, w_br_c, w_out):
    ga, gb, gc = split_cols(jax.nn.sigmoid(zg), (D_MODEL, D_MODEL, D_MODEL))
    m = ga * (ya @ w_br_a) + gb * (yb @ w_br_b) + gc * (yc @ w_br_c)
    return m @ w_out


def swiglu(h, w1, w3, w2):
    return (jax.nn.silu(h @ w1) * (h @ w3)) @ w2


def mixer(h, hc, p, rope, need_ctx):
    dtype = h.dtype
    zr, zq, zn, zg = split_cols(h @ p['w_in'], IN_SIZES)
    zrc, zqc, znc, zgc = split_cols(hc @ p['w_in'], IN_SIZES)

    lora = (p['rw_w0'], p['rw_w_up'], p['rw_a0'], p['rw_a_up'], p['rw_k_k'], p['rw_k_a'])
    r, v, dirs, gd = rwkv_inputs(centred_shift(zr, p['rw_mu_prev'], p['rw_mu_next']), *lora)
    r_c, v_c, dirs_c, gd_c = rwkv_inputs(centred_shift(zrc, p['rw_mu_prev'], p['rw_mu_next']), *lora)
    s_zero = jnp.zeros((hc.shape[0], RW_HEADS, RW_HEAD_DIM, RW_HEAD_DIM), jnp.float32)
    outs, outs_c = [], []
    for d, rev in enumerate((False, True)):
        dec, kd, av, bv = dirs_c[d]
        oc, s_ctx = rwkv7_scan(r_c, dec, kd, v_c, av, bv, s_zero, rev, need_ctx)
        outs_c.append(oc)
        dec, kd, av, bv = dirs[d]
        o, _ = rwkv7_scan(r, dec, kd, v, av, bv, s_ctx, rev, True)
        outs.append(o)
    rw_tail = (p['rw_g_up'], p['rw_r_k'], p['rw_ln_w'], p['rw_ln_b'], dtype)
    ya = rwkv_output(r, v, dirs[0][1], dirs[1][1], outs[0] + outs[1], gd, *rw_tail)

    gk, gv = gqa_kv(zq[..., GQ_WIDTH:], p['gq_k_norm'], rope)
    gkc, gvc = gqa_kv(zqc[..., GQ_WIDTH:], p['gq_k_norm'], None)
    yb = gqa_latent(gqa_q(zq[..., :GQ_WIDTH], p['gq_q_norm'], rope), gk, gv, gkc, gvc)

    nq, nk, nv = (heads(u, NA_HEADS) for u in split_cols(zn, (NA_WIDTH, NA_WIDTH, NA_WIDTH)))
    nkc = heads(znc[..., NA_WIDTH:2 * NA_WIDTH], NA_HEADS)
    nvc = heads(znc[..., 2 * NA_WIDTH:], NA_HEADS)
    yc = na_latent(nq, nk, nv, nkc, nvc, p['na_rpb'])

    br = (p['w_br_a'], p['w_br_b'], p['w_br_c'], p['w_out'])
    y = merge_branches(ya, yb, yc, zg, *br)
    if not need_ctx:
        return y, None
    ya_c = rwkv_output(r_c, v_c, dirs_c[0][1], dirs_c[1][1], outs_c[0] + outs_c[1], gd_c, *rw_tail)
    yb_c = dense_attend(gqa_q(zqc[..., :GQ_WIDTH], p['gq_q_norm'], None), gkc, gvc)
    yc_c = dense_attend(heads(znc[..., :NA_WIDTH], NA_HEADS), nkc, nvc)
    return y, merge_branches(ya_c, yb_c, yc_c, zgc, *br)


def setup_inputs(seed: int = 0) -> dict:
    key = jax.random.key(seed)
    ks = iter(jax.random.split(key, 40))
    L, D = DEPTH, D_MODEL
    f32 = jnp.float32
    def normal(shape, scale):
        return jax.random.normal(next(ks), shape, f32) * scale
    def unif(shape, lo, hi):
        return jax.random.uniform(next(ks), shape, f32, lo, hi)
    return {
        'x': normal((BATCH, SEQ, D), 1.0),
        'c': normal((BATCH, D), 1.0),
        'ctx': normal((BATCH, CTX_LEN, D), 1.0),
        'c_ctx': normal((D,), 1.0),
        'ada_w': normal((L, D, 6 * D), 0.5 * D ** -0.5),
        'ada_b': normal((L, 6 * D), 0.02),
        'norm1': 1.0 + normal((L, D), 0.05),
        'norm2': 1.0 + normal((L, D), 0.05),
        'w_in': normal((L, D, N_IN), D ** -0.5),
        'rw_mu_prev': unif((L, RW_COLS), 0.0, 0.5),
        'rw_mu_next': unif((L, RW_COLS), 0.0, 0.5),
        'rw_w0': unif((L, 2, RW_WIDTH), -6.0, 0.0),
        'rw_w_up': normal((L, 2, DECAY_LORA, RW_WIDTH), 0.1),
        'rw_a0': normal((L, 2, RW_WIDTH), 0.1),
        'rw_a_up': normal((L, 2, ICLR_LORA, RW_WIDTH), ICLR_LORA ** -0.5),
        'rw_g_up': normal((L, GATE_LORA, RW_WIDTH), GATE_LORA ** -0.5),
        'rw_k_k': 0.85 + normal((L, RW_WIDTH), 0.05),
        'rw_k_a': 1.0 + normal((L, RW_WIDTH), 0.05),
        'rw_r_k': normal((L, RW_HEADS, RW_HEAD_DIM), 0.1),
        'rw_ln_w': 1.0 + normal((L, RW_WIDTH), 0.05),
        'rw_ln_b': normal((L, RW_WIDTH), 0.02),
        'gq_q_norm': 1.0 + normal((L, GQ_HEAD_DIM), 0.05),
        'gq_k_norm': 1.0 + normal((L, GQ_HEAD_DIM), 0.05),
        'na_rpb': normal((L, NA_HEADS, 2 * NA_WIN_ROWS - 1, 2 * NA_WIN_COLS - 1), 0.5),
        'w_br_a': normal((L, RW_WIDTH, D), RW_WIDTH ** -0.5),
        'w_br_b': normal((L, GQ_WIDTH, D), GQ_WIDTH ** -0.5),
        'w_br_c': normal((L, NA_WIDTH, D), NA_WIDTH ** -0.5),
        'w_out': normal((L, D, D), D ** -0.5),
        'ffn_w1': normal((L, D, D_FF), D ** -0.5),
        'ffn_w3': normal((L, D, D_FF), D ** -0.5),
        'ffn_w2': normal((L, D_FF, D), D_FF ** -0.5),
        'final_norm': 1.0 + normal((D,), 0.05),
    }


def reference(x, c, ctx, c_ctx, ada_w, ada_b, norm1, norm2, w_in, rw_mu_prev, rw_mu_next, rw_w0, rw_w_up,
              rw_a0, rw_a_up, rw_g_up, rw_k_k, rw_k_a, rw_r_k, rw_ln_w, rw_ln_b, gq_q_norm, gq_k_norm, na_rpb,
              w_br_a, w_br_b, w_br_c, w_out, ffn_w1, ffn_w3, ffn_w2, final_norm):
    rope = axial_rope(x.shape[1], GQ_HEAD_DIM)
    xc = ctx
    for l in range(DEPTH):
        need_ctx = l < DEPTH - 1
        p = {'w_in': w_in[l], 'rw_mu_prev': rw_mu_prev[l], 'rw_mu_next': rw_mu_next[l],
             'rw_w0': rw_w0[l], 'rw_w_up': rw_w_up[l], 'rw_a0': rw_a0[l], 'rw_a_up': rw_a_up[l],
             'rw_g_up': rw_g_up[l], 'rw_k_k': rw_k_k[l], 'rw_k_a': rw_k_a[l], 'rw_r_k': rw_r_k[l],
             'rw_ln_w': rw_ln_w[l], 'rw_ln_b': rw_ln_b[l], 'gq_q_norm': gq_q_norm[l],
             'gq_k_norm': gq_k_norm[l], 'na_rpb': na_rpb[l], 'w_br_a': w_br_a[l], 'w_br_b': w_br_b[l],
             'w_br_c': w_br_c[l], 'w_out': w_out[l]}
        mod = jax.nn.silu(c) @ ada_w[l] + ada_b[l]
        mod_c = jax.nn.silu(c_ctx) @ ada_w[l] + ada_b[l]
        sh1, sc1, g1, sh2, sc2, g2 = jnp.split(mod[:, None, :], 6, axis=-1)
        sh1c, sc1c, g1c, sh2c, sc2c, g2c = jnp.split(mod_c, 6, axis=-1)
        h = rms_norm(x, norm1[l]) * (1 + sc1) + sh1
        hc = rms_norm(xc, norm1[l]) * (1 + sc1c) + sh1c
        y, yc = mixer(h, hc, p, rope, need_ctx)
        x = x + g1 * y
        h = rms_norm(x, norm2[l]) * (1 + sc2) + sh2
        x = x + g2 * swiglu(h, ffn_w1[l], ffn_w3[l], ffn_w2[l])
        if need_ctx:
            xc = xc + g1c * yc
            hc = rms_norm(xc, norm2[l]) * (1 + sc2c) + sh2c
            xc = xc + g2c * swiglu(hc, ffn_w1[l], ffn_w3[l], ffn_w2[l])
    return rms_norm(x, final_norm)
```

```python
import functools
import math

import numpy as np
import jax
import jax.numpy as jnp
from jax import lax
from jax.experimental import pallas as pl
from jax.experimental.pallas import tpu as pltpu

F32 = jnp.float32
BF16 = jnp.bfloat16

GRP = 256
GRID_W = 64
NORM_EPS = 1e-6

RW_HEAD = 64
RW_HEADS = 8
RW_WIDTH = RW_HEAD * RW_HEADS
LORA_PAD = 128
RW_GN_EPS = 64e-5
SCAN_CHUNK = 64

GQ_HEAD = 128
GQ_HEADS = 8
GQ_KV_HEADS = 2
GQ_GROUP = GQ_HEADS // GQ_KV_HEADS
GQ_WIDTH = GQ_HEAD * GQ_HEADS
GQ_KV_WIDTH = GQ_HEAD * GQ_KV_HEADS
ROPE_THETA = 10000.0

NA_HEAD = 64
NA_HEADS = 8
NA_WIDTH = NA_HEAD * NA_HEADS
NA_WIN_ROWS = 8
NA_WIN_COLS = 16
NEG_BIG = -1e30

VMEM_LIMIT = 56 * 1024 * 1024

ZG_OFF, ZG_W = 0, 6144
ZRKV_OFF = 6144
ZQ_OFF = ZRKV_OFF + 1536
ZN_OFF = ZQ_OFF + 1536
ZL_OFF = ZN_OFF + 1536
ZL_W = 4 * LORA_PAD + 256
Z_W = ZL_OFF + ZL_W


def _cparams(sem):
    return pltpu.CompilerParams(dimension_semantics=sem, vmem_limit_bytes=VMEM_LIMIT)


def _dot(a, b, dims=None, precision=None):
    if dims is None:
        dims = (((a.ndim - 1,), (0,)), ((), ()))
    return lax.dot_general(a, b, dims, precision=precision, preferred_element_type=F32)


def _dot_nt(a, b, precision=None):
    return _dot(a, b, (((1,), (1,)), ((), ())), precision)


def _ada_kernel(c_ref, w_ref, b_ref, o_ref):
    c = c_ref[...]
    a = (c * jax.nn.sigmoid(c)).astype(BF16)
    o_ref[0] = _dot(a, w_ref[0].astype(BF16)) + b_ref[0]


def _ada_mod(cc, ada_w, ada_b):
    L, D, N = ada_w.shape
    R = cc.shape[0]
    tn = 1536
    return pl.pallas_call(
        _ada_kernel,
        grid=(L, N // tn),
        in_specs=[pl.BlockSpec((R, D), lambda l, j: (0, 0)),
                  pl.BlockSpec((1, D, tn), lambda l, j: (l, 0, j)),
                  pl.BlockSpec((1, 1, tn), lambda l, j: (l, 0, j))],
        out_specs=pl.BlockSpec((1, R, tn), lambda l, j: (l, 0, j)),
        out_shape=jax.ShapeDtypeStruct((L, R, N), F32),
        compiler_params=_cparams(("parallel", "parallel")),
        name="ada_mod",
    )(cc, ada_w, ada_b.reshape(L, 1, N))


def _modnorm_kernel(x_ref, nw_ref, sh_ref, sc_ref, h_ref, *, groups):
    for g in range(groups):
        rows = pl.ds(g * GRP, GRP)
        x = x_ref[rows, :]
        y = x * lax.rsqrt(jnp.mean(x * x, axis=-1, keepdims=True) + NORM_EPS) * nw_ref[...]
        h_ref[rows, :] = (y * (1.0 + sc_ref[g]) + sh_ref[g]).astype(BF16)


def _modnorm(x, nw, modg, sh_blk, sc_blk):
    M, D = x.shape
    tm = _tile(M, 512)
    groups = tm // GRP
    return pl.pallas_call(
        functools.partial(_modnorm_kernel, groups=groups),
        grid=(M // tm,),
        in_specs=[pl.BlockSpec((tm, D), lambda i: (i, 0)),
                  pl.BlockSpec((1, D), lambda i: (0, 0)),
                  pl.BlockSpec((groups, 1, D), lambda i: (i, 0, sh_blk)),
                  pl.BlockSpec((groups, 1, D), lambda i: (i, 0, sc_blk))],
        out_specs=pl.BlockSpec((tm, D), lambda i: (i, 0)),
        out_shape=jax.ShapeDtypeStruct((M, D), BF16),
        compiler_params=_cparams(("parallel",)),
        name="modnorm",
    )(x, nw.reshape(1, D), modg, modg)


def _mm_kernel(a_ref, b_ref, o_ref):
    o_ref[...] = _dot(a_ref[...], b_ref[...]).astype(o_ref.dtype)


def _matmul(a, b, tm, tn, out_dtype, name):
    M, K = a.shape
    N = b.shape[1]
    return pl.pallas_call(
        _mm_kernel,
        grid=(M // tm, N // tn),
        in_specs=[pl.BlockSpec((tm, K), lambda i, j: (i, 0)),
                  pl.BlockSpec((K, tn), lambda i, j: (0, j))],
        out_specs=pl.BlockSpec((tm, tn), lambda i, j: (i, j)),
        out_shape=jax.ShapeDtypeStruct((M, N), out_dtype),
        compiler_params=_cparams(("parallel", "parallel")),
        name=name,
    )(a, b)


def _mm_res_kernel(a_ref, b_ref, res_ref, gate_ref, o_ref, *, groups):
    acc = _dot(a_ref[...], b_ref[...])
    for g in range(groups):
        rows = slice(g * GRP, (g + 1) * GRP)
        o_ref[rows, :] = res_ref[rows, :] + gate_ref[g] * acc[rows, :]


def _matmul_res(a, b, res, modg, gate_blk, tm, tn, name):
    M, K = a.shape
    N = b.shape[1]
    groups = tm // GRP
    nb = N // tn
    return pl.pallas_call(
        functools.partial(_mm_res_kernel, groups=groups),
        grid=(M // tm, nb),
        in_specs=[pl.BlockSpec((tm, K), lambda i, j: (i, 0)),
                  pl.BlockSpec((K, tn), lambda i, j: (0, j)),
                  pl.BlockSpec((tm, tn), lambda i, j: (i, j)),
                  pl.BlockSpec((groups, 1, tn), lambda i, j: (i, 0, gate_blk * nb + j))],
        out_specs=pl.BlockSpec((tm, tn), lambda i, j: (i, j)),
        out_shape=jax.ShapeDtypeStruct((M, N), F32),
        compiler_params=_cparams(("parallel", "parallel")),
        name=name,
    )(a, b, res, modg)


def _ffn_up_kernel(h_ref, w1_ref, w3_ref, o_ref):
    h = h_ref[...]
    a = _dot(h, w1_ref[...])
    b = _dot(h, w3_ref[...])
    o_ref[...] = (a * jax.nn.sigmoid(a) * b).astype(o_ref.dtype)


def _ffn_up(h, w1, w3, tm, tn):
    M, K = h.shape
    N = w1.shape[1]
    return pl.pallas_call(
        _ffn_up_kernel,
        grid=(M // tm, N // tn),
        in_specs=[pl.BlockSpec((tm, K), lambda i, j: (i, 0)),
                  pl.BlockSpec((K, tn), lambda i, j: (0, j)),
                  pl.BlockSpec((K, tn), lambda i, j: (0, j))],
        out_specs=pl.BlockSpec((tm, tn), lambda i, j: (i, j)),
        out_shape=jax.ShapeDtypeStruct((M, N), BF16),
        compiler_params=_cparams(("parallel", "parallel")),
        name="ffn_up",
    )(h, w1, w3)


def _merge_kernel(ya_ref, yb_ref, yc_ref, wa_ref, wb_ref, wc_ref, ga_ref, gb_ref, gc_ref, o_ref):
    m = jax.nn.sigmoid(ga_ref[...]) * _dot(ya_ref[...], wa_ref[...])
    m = m + jax.nn.sigmoid(gb_ref[...]) * _dot(yb_ref[...], wb_ref[...])
    m = m + jax.nn.sigmoid(gc_ref[...]) * _dot(yc_ref[...], wc_ref[...])
    o_ref[...] = m.astype(o_ref.dtype)


def _merge(ya, yb, yc, wa, wb, wc, z, tm, tn=512):
    M = ya.shape[0]
    D = wa.shape[1]
    nb = D // tn
    g0 = ZG_OFF // tn
    return pl.pallas_call(
        _merge_kernel,
        grid=(M // tm, nb),
        in_specs=[pl.BlockSpec((tm, ya.shape[1]), lambda i, j: (i, 0)),
                  pl.BlockSpec((tm, yb.shape[1]), lambda i, j: (i, 0)),
                  pl.BlockSpec((tm, yc.shape[1]), lambda i, j: (i, 0)),
                  pl.BlockSpec((wa.shape[0], tn), lambda i, j: (0, j)),
                  pl.BlockSpec((wb.shape[0], tn), lambda i, j: (0, j)),
                  pl.BlockSpec((wc.shape[0], tn), lambda i, j: (0, j)),
                  pl.BlockSpec((tm, tn), lambda i, j: (i, g0 + j)),
                  pl.BlockSpec((tm, tn), lambda i, j: (i, g0 + nb + j)),
                  pl.BlockSpec((tm, tn), lambda i, j: (i, g0 + 2 * nb + j))],
        out_specs=pl.BlockSpec((tm, tn), lambda i, j: (i, j)),
        out_shape=jax.ShapeDtypeStruct((M, D), BF16),
        compiler_params=_cparams(("parallel", "parallel")),
        name="merge",
    )(ya, yb, yc, wa, wb, wc, z, z, z)


def _final_norm_kernel(x_ref, w_ref, o_ref):
    x = x_ref[0]
    o_ref[0] = x * lax.rsqrt(jnp.mean(x * x, axis=-1, keepdims=True) + NORM_EPS) * w_ref[...]


def _final_norm(xt, w, ctx_len):
    B, T, D = xt.shape
    S = T - ctx_len
    cg = ctx_len // GRP
    return pl.pallas_call(
        _final_norm_kernel,
        grid=(B, S // GRP),
        in_specs=[pl.BlockSpec((1, GRP, D), lambda b, i: (b, cg + i, 0)),
                  pl.BlockSpec((1, D), lambda b, i: (0, 0))],
        out_specs=pl.BlockSpec((1, GRP, D), lambda b, i: (b, i, 0)),
        out_shape=jax.ShapeDtypeStruct((B, S, D), F32),
        compiler_params=_cparams(("parallel", "parallel")),
        name="final_norm",
    )(xt, w.reshape(1, D))


def _gqa_prep_kernel(z_ref, cos_ref, sin_ref, gq_ref, gk_ref, q_ref, k_ref, v_ref):
    cosf = cos_ref[...]
    sinf = sin_ref[...]

    def norm_rope(x, gain):
        y = x * lax.rsqrt(jnp.mean(x * x, axis=-1, keepdims=True) + NORM_EPS) * gain
        return y * cosf + pltpu.roll(y, GQ_HEAD // 2, axis=1) * sinf

    scale = GQ_HEAD ** -0.5
    for h in range(GQ_HEADS):
        cols = slice(h * GQ_HEAD, (h + 1) * GQ_HEAD)
        q_ref[:, cols] = (norm_rope(z_ref[:, cols], gq_ref[...]) * scale).astype(BF16)
    for h in range(GQ_KV_HEADS):
        cols = slice(h * GQ_HEAD, (h + 1) * GQ_HEAD)
        zc = slice(GQ_WIDTH + h * GQ_HEAD, GQ_WIDTH + (h + 1) * GQ_HEAD)
        k_ref[:, cols] = norm_rope(z_ref[:, zc], gk_ref[...]).astype(BF16)
    v_ref[...] = z_ref[:, GQ_WIDTH + GQ_KV_WIDTH:].astype(BF16)


def _gqa_prep(z, cosf, sinf, gq, gk, gps):
    M = z.shape[0]
    zw = GQ_WIDTH + 2 * GQ_KV_WIDTH
    return pl.pallas_call(
        _gqa_prep_kernel,
        grid=(M // GRP,),
        in_specs=[pl.BlockSpec((GRP, zw), lambda i: (i, ZQ_OFF // zw)),
                  pl.BlockSpec((GRP, GQ_HEAD), lambda i: (i % gps, 0)),
                  pl.BlockSpec((GRP, GQ_HEAD), lambda i: (i % gps, 0)),
                  pl.BlockSpec((1, GQ_HEAD), lambda i: (0, 0)),
                  pl.BlockSpec((1, GQ_HEAD), lambda i: (0, 0))],
        out_specs=[pl.BlockSpec((GRP, GQ_WIDTH), lambda i: (i, 0)),
                   pl.BlockSpec((GRP, GQ_KV_WIDTH), lambda i: (i, 0)),
                   pl.BlockSpec((GRP, GQ_KV_WIDTH), lambda i: (i, 0))],
        out_shape=[jax.ShapeDtypeStruct((M, GQ_WIDTH), BF16),
                   jax.ShapeDtypeStruct((M, GQ_KV_WIDTH), BF16),
                   jax.ShapeDtypeStruct((M, GQ_KV_WIDTH), BF16)],
        compiler_params=_cparams(("parallel",)),
        name="gqa_prep",
    )(z, cosf, sinf, gq.reshape(1, GQ_HEAD), gk.reshape(1, GQ_HEAD))


def _gqa_attn_kernel(q_ref, k_ref, v_ref, o_ref, m_ref, l_ref, acc_ref, *, tq, tk, ctx_tiles, n_ctx_kv, n_kv):
    qi = pl.program_id(2)
    q = jnp.concatenate([q_ref[0, :, h * GQ_HEAD:(h + 1) * GQ_HEAD] for h in range(GQ_GROUP)], axis=0)
    m_ref[...] = jnp.full(m_ref.shape, NEG_BIG, F32)
    l_ref[...] = jnp.zeros(l_ref.shape, F32)
    acc_ref[...] = jnp.zeros(acc_ref.shape, F32)

    def body(kv, carry):
        off = pl.multiple_of(kv * tk, tk)
        kc = k_ref[0, pl.ds(off, tk), :]
        vc = v_ref[0, pl.ds(off, tk), :]
        s = _dot_nt(q, kc)
        m_old = m_ref[...]
        m_new = jnp.maximum(m_old, jnp.max(s, axis=-1, keepdims=True))
        p = jnp.exp(s - m_new)
        alpha = jnp.exp(m_old - m_new)
        l_ref[...] = alpha * l_ref[...] + jnp.sum(p, axis=-1, keepdims=True)
        acc_ref[...] = alpha * acc_ref[...] + _dot(p.astype(BF16), vc)
        m_ref[...] = m_new
        return carry

    lax.fori_loop(0, jnp.where(qi < ctx_tiles, n_ctx_kv, n_kv), body, 0)
    o = acc_ref[...] / l_ref[...]
    for h in range(GQ_GROUP):
        o_ref[0, :, h * GQ_HEAD:(h + 1) * GQ_HEAD] = o[h * tq:(h + 1) * tq].astype(o_ref.dtype)


def _gqa_attn(q, k, v, ctx_len, tq=256, tk=256):
    B, T, _ = q.shape
    gw = GQ_GROUP * GQ_HEAD
    kern = functools.partial(_gqa_attn_kernel, tq=tq, tk=tk, ctx_tiles=ctx_len // tq,
                             n_ctx_kv=ctx_len // tk, n_kv=T // tk)
    return pl.pallas_call(
        kern,
        grid=(B, GQ_KV_HEADS, T // tq),
        in_specs=[pl.BlockSpec((1, tq, gw), lambda b, g, i: (b, i, g)),
                  pl.BlockSpec((1, T, GQ_HEAD), lambda b, g, i: (b, 0, g)),
                  pl.BlockSpec((1, T, GQ_HEAD), lambda b, g, i: (b, 0, g))],
        out_specs=pl.BlockSpec((1, tq, gw), lambda b, g, i: (b, i, g)),
        out_shape=jax.ShapeDtypeStruct((B, T, GQ_WIDTH), BF16),
        scratch_shapes=[pltpu.VMEM((GQ_GROUP * tq, 1), F32),
                        pltpu.VMEM((GQ_GROUP * tq, 1), F32),
                        pltpu.VMEM((GQ_GROUP * tq, GQ_HEAD), F32)],
        compiler_params=_cparams(("parallel", "parallel", "arbitrary")),
        name="gqa_attn",
    )(q, k, v)


def _na_kernel(q_ref, k_ref, v_ref, bias_ref, o_ref, *, ctx_len, rows):
    scale = NA_HEAD ** -0.5
    wr = min(NA_WIN_ROWS, rows)
    nwin = wr * GRID_W
    kc = k_ref[0, 0, 0:ctx_len, :]
    vc = v_ref[0, 0, 0:ctx_len, :]

    qc = q_ref[0, 0, 0:ctx_len, :] * scale
    s = _dot_nt(qc, kc)
    p = jnp.exp(s - jnp.max(s, axis=-1, keepdims=True))
    o = _dot(p.astype(BF16), vc) / jnp.sum(p, axis=-1, keepdims=True)
    o_ref[0, 0, 0:ctx_len, :] = o.astype(o_ref.dtype)

    def row_body(i, carry):
        rs = jnp.clip(i - wr // 2, 0, rows - wr)
        d = i - rs
        qoff = pl.multiple_of(ctx_len + i * GRID_W, GRID_W)
        koff = pl.multiple_of(ctx_len + rs * GRID_W, GRID_W)
        qr = q_ref[0, 0, pl.ds(qoff, GRID_W), :] * scale
        kw = k_ref[0, 0, pl.ds(koff, nwin), :]
        vw = v_ref[0, 0, pl.ds(koff, nwin), :]
        s1 = _dot_nt(qr, kw) + bias_ref[0, d]
        s2 = _dot_nt(qr, kc)
        m = jnp.maximum(jnp.max(s1, axis=-1, keepdims=True), jnp.max(s2, axis=-1, keepdims=True))
        p1 = jnp.exp(s1 - m)
        p2 = jnp.exp(s2 - m)
        l = jnp.sum(p1, axis=-1, keepdims=True) + jnp.sum(p2, axis=-1, keepdims=True)
        o = (_dot(p1.astype(BF16), vw) + _dot(p2.astype(BF16), vc)) / l
        o_ref[0, 0, pl.ds(qoff, GRID_W), :] = o.astype(o_ref.dtype)
        return carry

    lax.fori_loop(0, rows, row_body, 0)


def _na_bias_table(rpb, rows):
    wr = min(NA_WIN_ROWS, rows)
    cols = np.arange(GRID_W)
    cstart = np.clip(cols - NA_WIN_COLS // 2, 0, GRID_W - NA_WIN_COLS)
    c = np.arange(GRID_W)
    inside = (c[None, :] >= cstart[:, None]) & (c[None, :] < cstart[:, None] + NA_WIN_COLS)
    col_off = np.clip(c[None, :] - cols[:, None] + (NA_WIN_COLS - 1), 0, 2 * NA_WIN_COLS - 2)
    d = np.arange(wr)
    r = np.arange(wr)
    row_off = r[None, :] - d[:, None] + (NA_WIN_ROWS - 1)
    t = rpb[:, row_off][:, :, :, col_off]
    t = jnp.where(inside[None, None, None], t, NEG_BIG)
    return t.transpose(0, 1, 3, 2, 4).reshape(rpb.shape[0], wr, GRID_W, wr * GRID_W)


def _na_attn(q, k, v, bias, ctx_len):
    B, H, T, _ = q.shape
    rows = (T - ctx_len) // GRID_W
    wr = bias.shape[1]
    blk = pl.BlockSpec((1, 1, T, NA_HEAD), lambda b, h: (b, h, 0, 0))
    return pl.pallas_call(
        functools.partial(_na_kernel, ctx_len=ctx_len, rows=rows),
        grid=(B, H),
        in_specs=[blk, blk, blk,
                  pl.BlockSpec((1, wr, GRID_W, wr * GRID_W), lambda b, h: (h, 0, 0, 0))],
        out_specs=blk,
        out_shape=jax.ShapeDtypeStruct((B, H, T, NA_HEAD), BF16),
        compiler_params=_cparams(("parallel", "parallel")),
        name="na_attn",
    )(q, k, v, bias)


def _rw_prep_kernel(z_ref, zp_ref, zn_ref, l_ref, lp_ref, ln_ref,
                    mup_ref, mun_ref, lmup_ref, lmun_ref, kk_ref_, ka_ref, w0_ref, wup_ref, a0_ref, aup_ref,
                    gup_ref,
                    r_out, v_out, kk_out, g_out, cs_out, lw_out, kd_out, bd_out, *, gps, ctx_groups):
    t = pl.program_id(1)
    first = jnp.logical_or(t == 0, t == ctx_groups)
    last = jnp.logical_or(t == ctx_groups - 1, t == gps - 1)
    pv = jnp.where(first, 0.0, 1.0)
    nv = jnp.where(last, 0.0, 1.0)
    row = lax.broadcasted_iota(jnp.int32, (GRP, 1), 0)

    def shift(x_ref, p_ref, n_ref, mp_ref, mn_ref):
        x = x_ref[...]
        xp = jnp.where(row == 0, p_ref[7:8, :] * pv, pltpu.roll(x, 1, axis=0))
        xn = jnp.where(row == GRP - 1, n_ref[0:1, :] * nv, pltpu.roll(x, GRP - 1, axis=0))
        return x + mp_ref[...] * (xp - x) + mn_ref[...] * (xn - x)

    zs = shift(z_ref, zp_ref, zn_ref, mup_ref, mun_ref)
    ls = shift(l_ref, lp_ref, ln_ref, lmup_ref, lmun_ref)
    r = zs[:, 0:RW_WIDTH]
    k = zs[:, RW_WIDTH:2 * RW_WIDTH]
    v = zs[:, 2 * RW_WIDTH:3 * RW_WIDTH]

    hi = lax.Precision.HIGHEST
    ci = lax.broadcasted_iota(jnp.int32, (RW_WIDTH, RW_WIDTH), 0) // RW_HEAD
    cj = lax.broadcasted_iota(jnp.int32, (RW_WIDTH, RW_WIDTH), 1) // RW_HEAD
    head_ones = jnp.where(ci == cj, 1.0, 0.0).astype(F32)
    kk = k * kk_ref_[...]
    ss = _dot(kk * kk, head_ones, precision=hi)
    kk = kk / jnp.maximum(jnp.sqrt(ss), 1e-12)

    ti = lax.broadcasted_iota(jnp.int32, (GRP, GRP), 0)
    tj = lax.broadcasted_iota(jnp.int32, (GRP, GRP), 1)
    same = (ti // SCAN_CHUNK) == (tj // SCAN_CHUNK)

    def heads_out(ref, val, lead):
        for h in range(RW_HEADS):
            ref[lead + (h,)] = val[:, h * RW_HEAD:(h + 1) * RW_HEAD]

    heads_out(r_out, r, (0,))
    heads_out(v_out, v, (0,))
    heads_out(kk_out, kk, (0,))
    gd = ls[:, 4 * LORA_PAD:]
    g_out[...] = _dot(jax.nn.sigmoid(gd), gup_ref[...], precision=hi)

    for d in range(2):
        wl = ls[:, d * LORA_PAD:(d + 1) * LORA_PAD]
        al = ls[:, (2 + d) * LORA_PAD:(3 + d) * LORA_PAD]
        w_raw = w0_ref[d] + _dot(jnp.tanh(wl), wup_ref[d], precision=hi)
        lw = -math.exp(-0.5) * jax.nn.sigmoid(w_raw)
        a = jax.nn.sigmoid(a0_ref[d] + _dot(al, aup_ref[d], precision=hi))
        kd = k * (1.0 + (a - 1.0) * ka_ref[...])
        bd = kk * a
        tri = jnp.where(jnp.logical_and(same, (ti >= tj) if d == 0 else (ti <= tj)), 1.0, 0.0).astype(F32)
        cs = _dot(tri, lw, precision=hi)
        heads_out(cs_out, cs, (d, 0))
        heads_out(lw_out, lw, (d, 0))
        heads_out(kd_out, kd, (d, 0))
        heads_out(bd_out, bd, (d, 0))


def _rw_prep(z, p, B, gps, ctx_groups):
    M = z.shape[0]
    T = gps * GRP
    nb8 = M // 8
    rb = GRP // 8
    zc = ZRKV_OFF // 1536
    lc = ZL_OFF // ZL_W

    def cur(c):
        return lambda b, t: (b * gps + t, c)

    def prev(c):
        return lambda b, t: (jnp.maximum((b * gps + t) * rb - 1, 0), c)

    def nxt(c):
        return lambda b, t: (jnp.minimum((b * gps + t + 1) * rb, nb8 - 1), c)

    def const(shape):
        nd = len(shape)
        return pl.BlockSpec(shape, lambda b, t: (0,) * nd)

    hm = pl.BlockSpec((1, RW_HEADS, GRP, RW_HEAD), lambda b, t: (b, 0, t, 0))
    hm2 = pl.BlockSpec((2, 1, RW_HEADS, GRP, RW_HEAD), lambda b, t: (0, b, 0, t, 0))
    hshape = jax.ShapeDtypeStruct((B, RW_HEADS, T, RW_HEAD), F32)
    hshape2 = jax.ShapeDtypeStruct((2, B, RW_HEADS, T, RW_HEAD), F32)
    return pl.pallas_call(
        functools.partial(_rw_prep_kernel, gps=gps, ctx_groups=ctx_groups),
        grid=(B, gps),
        in_specs=[pl.BlockSpec((GRP, 1536), cur(zc)),
                  pl.BlockSpec((8, 1536), prev(zc)),
                  pl.BlockSpec((8, 1536), nxt(zc)),
                  pl.BlockSpec((GRP, ZL_W), cur(lc)),
                  pl.BlockSpec((8, ZL_W), prev(lc)),
                  pl.BlockSpec((8, ZL_W), nxt(lc)),
                  const((1, 1536)), const((1, 1536)), const((1, ZL_W)), const((1, ZL_W)),
                  const((1, RW_WIDTH)), const((1, RW_WIDTH)),
                  const((2, 1, RW_WIDTH)), const((2, LORA_PAD, RW_WIDTH)),
                  const((2, 1, RW_WIDTH)), const((2, LORA_PAD, RW_WIDTH)),
                  const((256, RW_WIDTH))],
        out_specs=[hm, hm, hm, pl.BlockSpec((GRP, RW_WIDTH), lambda b, t: (b * gps + t, 0)),
                   hm2, hm2, hm2, hm2],
        out_shape=[hshape, hshape, hshape, jax.ShapeDtypeStruct((M, RW_WIDTH), F32),
                   hshape2, hshape2, hshape2, hshape2],
        compiler_params=_cparams(("parallel", "parallel")),
        name="rw_prep",
    )(z, z, z, z, z, z, p['mu_p'], p['mu_n'], p['lmu_p'], p['lmu_n'], p['k_k'], p['k_a'],
      p['w0'], p['w_up'], p['a0'], p['a_up'], p['g_up'])


def _bmm(a, b, nt=False):
    a = a.astype(BF16)
    b = b.astype(BF16)
    out = []
    for h in range(a.shape[0]):
        out.append(_dot_nt(a[h], b[h]) if nt else _dot(a[h], b[h]))
    return jnp.stack(out, axis=0)


def _bmm_tn(a, b):
    a = a.astype(BF16)
    b = b.astype(BF16)
    out = []
    for h in range(a.shape[0]):
        out.append(_dot(a[h], b[h], (((0,), (0,)), ((), ()))))
    return jnp.stack(out, axis=0)


def _rw_scan_kernel(r_ref, v_ref, kk_ref, cs_ref, lw_ref, kd_ref, bd_ref, o_ref, s_ref):
    d = pl.program_id(1)
    j = pl.program_id(2)
    C = SCAN_CHUNK

    @pl.when(j == 0)
    def _():
        s_ref[...] = jnp.zeros(s_ref.shape, F32)

    fwd = d == 0
    ti = lax.broadcasted_iota(jnp.int32, (C, C), 0)
    tj = lax.broadcasted_iota(jnp.int32, (C, C), 1)
    before = (ti - tj) * jnp.where(fwd, 1, -1)
    strict = (before > 0)[None]
    incl = (before >= 0)[None]
    eye = (ti == tj)[None]

    r = r_ref[0]
    v = v_ref[0]
    cs = cs_ref[0, 0]
    lw = lw_ref[0, 0]
    pinv = jnp.exp(-cs)
    ra = r * jnp.exp(cs)
    at = -kk_ref[0] * jnp.exp(cs - lw)
    bt = bd_ref[0, 0] * pinv
    kt = kd_ref[0, 0] * pinv
    tot = jnp.where(fwd, cs[:, C - 1:C, :], cs[:, 0:1, :])
    pc = jnp.exp(tot)
    s0 = s_ref[...]

    lhs = jnp.concatenate([at, ra], axis=1)
    rhs = jnp.concatenate([bt, kt], axis=1)
    gm = _bmm(lhs, rhs, nt=True)
    a_ab = jnp.where(strict, gm[:, :C, :C], 0.0)
    a_ak = jnp.where(strict, gm[:, :C, C:], 0.0)
    a_rb = jnp.where(incl, gm[:, C:, :C], 0.0)
    a_rk = jnp.where(incl, gm[:, C:, C:], 0.0)

    tm = jnp.where(eye, 1.0, 0.0) + a_ab
    pw = a_ab
    for _ in range(int(math.log2(C)) - 1):
        pw = _bmm(pw, pw)
        tm = tm + _bmm(tm, pw)

    s_lhs = _bmm(lhs, s0, nt=True)
    u = _bmm(tm, s_lhs[:, :C] + _bmm(a_ak, v))
    o = s_lhs[:, C:] + _bmm(a_rb, u) + _bmm(a_rk, v)
    o_ref[0, 0] = o
    s_ref[...] = s0 * pc + _bmm_tn(u, bt * pc) + _bmm_tn(v, kt * pc)


def _rw_scan(r, v, kk, cs, lw, kd, bd, ctx_len):
    B, H, T, N = r.shape
    C = SCAN_CHUNK
    nch = T // C
    ncc = ctx_len // C

    def cmap(d, j):
        return jnp.where(d == 0, j, jnp.where(j < ncc, ncc - 1 - j, nch - 1 - (j - ncc)))

    shared = pl.BlockSpec((1, H, C, N), lambda b, d, j: (b, 0, cmap(d, j), 0))
    per_dir = pl.BlockSpec((1, 1, H, C, N), lambda b, d, j: (d, b, 0, cmap(d, j), 0))
    return pl.pallas_call(
        _rw_scan_kernel,
        grid=(B, 2, nch),
        in_specs=[shared, shared, shared, per_dir, per_dir, per_dir, per_dir],
        out_specs=per_dir,
        out_shape=jax.ShapeDtypeStruct((2, B, H, T, N), F32),
        scratch_shapes=[pltpu.VMEM((H, N, N), F32)],
        compiler_params=_cparams(("parallel", "parallel", "arbitrary")),
        name="rw_scan",
    )(r, v, kk, cs, lw, kd, bd)


def _rw_out_kernel(o_ref, r_ref, v_ref, kd_ref, g_ref, rk_ref, lnw_ref, lnb_ref, y_ref):
    ys = []
    for h in range(RW_HEADS):
        wkv = o_ref[0, 0, h] + o_ref[1, 0, h]
        mu = jnp.mean(wkv, axis=-1, keepdims=True)
        cen = wkv - mu
        var = jnp.mean(cen * cen, axis=-1, keepdims=True)
        y = cen * lax.rsqrt(var + RW_GN_EPS) * lnw_ref[h] + lnb_ref[h]
        r = r_ref[0, h]
        rk = rk_ref[h]
        bonus = (jnp.sum(r * kd_ref[0, 0, h] * rk, axis=-1, keepdims=True)
                 + jnp.sum(r * kd_ref[1, 0, h] * rk, axis=-1, keepdims=True)) * v_ref[0, h]
        ys.append(y + bonus)
    y_ref[...] = (jnp.concatenate(ys, axis=-1) * g_ref[...]).astype(y_ref.dtype)


def _rw_out(o, r, v, kd, g, rk, lnw, lnb, gps):
    B, H, T, N = r.shape
    M = B * T
    hm = pl.BlockSpec((1, H, GRP, N), lambda b, t: (b, 0, t, 0))
    hm2 = pl.BlockSpec((2, 1, H, GRP, N), lambda b, t: (0, b, 0, t, 0))
    par = pl.BlockSpec((H, 1, N), lambda b, t: (0, 0, 0))
    return pl.pallas_call(
        _rw_out_kernel,
        grid=(B, gps),
        in_specs=[hm2, hm, hm, hm2, pl.BlockSpec((GRP, RW_WIDTH), lambda b, t: (b * gps + t, 0)),
                  par, par, par],
        out_specs=pl.BlockSpec((GRP, RW_WIDTH), lambda b, t: (b * gps + t, 0)),
        out_shape=jax.ShapeDtypeStruct((M, RW_WIDTH), BF16),
        compiler_params=_cparams(("parallel", "parallel")),
        name="rw_out",
    )(o, r, v, kd, g, rk.reshape(H, 1, N), lnw.reshape(H, 1, N), lnb.reshape(H, 1, N))


def _pad_rows(w, n):
    return jnp.pad(w, [(0, 0)] * (w.ndim - 2) + [(0, n - w.shape[-2]), (0, 0)])


def _pad_cols(w, n):
    return jnp.pad(w, [(0, 0)] * (w.ndim - 1) + [(0, n - w.shape[-1])])


def _deinterleave(w):
    n = w.shape[-1] // GQ_HEAD
    w = w.reshape(w.shape[:-1] + (n, GQ_HEAD // 2, 2))
    return jnp.swapaxes(w, -1, -2).reshape(w.shape[:-3] + (n * GQ_HEAD,))


def _in_layout(w, lora):
    rkv = w[..., 0:1536]
    lo = [w[..., 1536 + i * lora: 1536 + (i + 1) * lora] for i in range(4)]
    gd = w[..., 1536 + 4 * lora: 1536 + 4 * lora + 256]
    o = 1536 + 4 * lora + 256
    zq = w[..., o:o + 1536]
    zn = w[..., o + 1536:o + 3072]
    zg = w[..., o + 3072:]
    zq = jnp.concatenate([_deinterleave(zq[..., :GQ_WIDTH + GQ_KV_WIDTH]), zq[..., GQ_WIDTH + GQ_KV_WIDTH:]], axis=-1)
    return jnp.concatenate([zg, rkv, zq, zn] + [_pad_cols(x, LORA_PAD) for x in lo] + [gd], axis=-1)


def _shift_layout(mu, lora):
    lo = [_pad_cols(mu[..., 1536 + i * lora: 1536 + (i + 1) * lora], LORA_PAD) for i in range(4)]
    return mu[..., 0:1536], jnp.concatenate(lo + [mu[..., 1536 + 4 * lora:]], axis=-1)


def _tile(n, pref):
    t = pref
    while n % t:
        t //= 2
    return t


def _axial_tables(ctx_len, seq):
    n_freq = GQ_HEAD // 4
    inv = ROPE_THETA ** (-jnp.arange(n_freq, dtype=F32) / n_freq)
    t = jnp.arange(seq, dtype=jnp.int32)
    row = (t // GRID_W).astype(F32)
    col = (t % GRID_W).astype(F32)
    ang = jnp.concatenate([row[:, None] * inv, col[:, None] * inv], axis=-1)
    cos, sin = jnp.cos(ang), jnp.sin(ang)
    cosf = jnp.concatenate([jnp.ones((ctx_len, GQ_HEAD), F32), jnp.concatenate([cos, cos], axis=-1)], axis=0)
    sinf = jnp.concatenate([jnp.zeros((ctx_len, GQ_HEAD), F32), jnp.concatenate([-sin, sin], axis=-1)], axis=0)
    return cosf, sinf


def kernel(x, c, ctx, c_ctx, ada_w, ada_b, norm1, norm2, w_in, rw_mu_prev, rw_mu_next, rw_w0, rw_w_up, rw_a0, rw_a_up, rw_g_up, rw_k_k, rw_k_a, rw_r_k, rw_ln_w, rw_ln_b, gq_q_norm, gq_k_norm, na_rpb, w_br_a, w_br_b, w_br_c, w_out, ffn_w1, ffn_w3, ffn_w2, final_norm):
    B, S, D = x.shape
    C = ctx.shape[1]
    L = ada_w.shape[0]
    T = C + S
    M = B * T
    gps = T // GRP
    cgr = C // GRP
    lora = rw_w_up.shape[2]
    assert C % GRP == 0 and S % GRP == 0 and S % GRID_W == 0
    assert w_in.shape[2] == 1536 + 4 * lora + 256 + 1536 + 1536 + 3 * D
    tm = _tile(M, 1024)

    w_in_p = _in_layout(w_in, lora).astype(BF16)
    mu_p, lmu_p = _shift_layout(rw_mu_prev, lora)
    mu_n, lmu_n = _shift_layout(rw_mu_next, lora)
    w_up = _pad_rows(rw_w_up, LORA_PAD)
    a_up = _pad_rows(rw_a_up, LORA_PAD)
    gq = _deinterleave(gq_q_norm)
    gk = _deinterleave(gq_k_norm)
    wa = w_br_a.astype(BF16)
    wb = w_br_b.astype(BF16)
    wc = w_br_c.astype(BF16)
    wo = w_out.astype(BF16)
    w1 = ffn_w1.astype(BF16)
    w3 = ffn_w3.astype(BF16)
    w2 = ffn_w2.astype(BF16)
    cosf, sinf = _axial_tables(C, S)
    rows = S // GRID_W

    cc = jnp.concatenate([c, c_ctx[None, :], jnp.zeros((-(B + 1) % 8, D), F32)], axis=0)
    mod = _ada_mod(cc, ada_w, ada_b)
    modg = jnp.concatenate([jnp.broadcast_to(mod[:, B, None, None, :], (L, B, cgr, 6 * D)),
                            jnp.broadcast_to(mod[:, :B, None, :], (L, B, gps - cgr, 6 * D))], axis=2)
    modg = modg.reshape(L, B * gps, 1, 6 * D)

    xt = jnp.concatenate([ctx, x], axis=1).reshape(M, D)

    for l in range(L):
        mg = modg[l]
        h = _modnorm(xt, norm1[l], mg, 0, 1)
        z = _matmul(h, w_in_p[l], tm, 768, F32, "in_proj")

        pr = {'mu_p': mu_p[l][None], 'mu_n': mu_n[l][None], 'lmu_p': lmu_p[l][None], 'lmu_n': lmu_n[l][None],
              'k_k': rw_k_k[l][None], 'k_a': rw_k_a[l][None],
              'w0': rw_w0[l][:, None, :], 'w_up': w_up[l], 'a0': rw_a0[l][:, None, :], 'a_up': a_up[l],
              'g_up': rw_g_up[l]}
        r_, v_, kk_, g_, cs_, lw_, kd_, bd_ = _rw_prep(z, pr, B, gps, cgr)
        o_ = _rw_scan(r_, v_, kk_, cs_, lw_, kd_, bd_, C)
        ya = _rw_out(o_, r_, v_, kd_, g_, rw_r_k[l], rw_ln_w[l], rw_ln_b[l], gps)

        q, k, v = _gqa_prep(z, cosf, sinf, gq[l], gk[l], gps)
        yb = _gqa_attn(q.reshape(B, T, GQ_WIDTH), k.reshape(B, T, GQ_KV_WIDTH), v.reshape(B, T, GQ_KV_WIDTH), C)
        yb = yb.reshape(M, GQ_WIDTH)

        zn = z[:, ZN_OFF:ZN_OFF + 3 * NA_WIDTH].astype(BF16).reshape(B, T, 3, NA_HEADS, NA_HEAD)
        zn = zn.transpose(2, 0, 3, 1, 4)
        yc = _na_attn(zn[0], zn[1], zn[2], _na_bias_table(na_rpb[l], rows), C)
        yc = yc.transpose(0, 2, 1, 3).reshape(M, NA_WIDTH)

        m = _merge(ya, yb, yc, wa[l], wb[l], wc[l], z, tm)
        xt = _matmul_res(m, wo[l], xt, mg, 2, tm, 512, "out_proj")
        h = _modnorm(xt, norm2[l], mg, 3, 4)
        u = _ffn_up(h, w1[l], w3[l], tm, 512)
        xt = _matmul_res(u, w2[l], xt, mg, 5, _tile(M, 512), 512, "ffn_down")

    return _final_norm(xt.reshape(B, T, D), final_norm, C)
```

```python
import functools
import math

import numpy as np
import jax
import jax.numpy as jnp
from jax import lax
from jax.experimental import pallas as pl
from jax.experimental.pallas import tpu as pltpu

F32 = jnp.float32
BF16 = jnp.bfloat16

GRP = 256
GRID_W = 64
NORM_EPS = 1e-6

RW_HEAD = 64
RW_HEADS = 8
RW_WIDTH = RW_HEAD * RW_HEADS
LORA_PAD = 128
RW_GN_EPS = 64e-5
SCAN_CHUNK = 64

GQ_HEAD = 128
GQ_HEADS = 8
GQ_KV_HEADS = 2
GQ_GROUP = GQ_HEADS // GQ_KV_HEADS
GQ_WIDTH = GQ_HEAD * GQ_HEADS
GQ_KV_WIDTH = GQ_HEAD * GQ_KV_HEADS
ROPE_THETA = 10000.0

NA_HEAD = 64
NA_HEADS = 8
NA_WIDTH = NA_HEAD * NA_HEADS
NA_WIN_ROWS = 8
NA_WIN_COLS = 16
NEG_BIG = -1e30

VMEM_LIMIT = 56 * 1024 * 1024

ZG_OFF, ZG_W = 0, 6144
ZRKV_OFF = 6144
ZQ_OFF = ZRKV_OFF + 1536
ZN_OFF = ZQ_OFF + 1536
ZL_OFF = ZN_OFF + 1536
ZL_W = 4 * LORA_PAD + 256
Z_W = ZL_OFF + ZL_W


def _cparams(sem):
    return pltpu.CompilerParams(dimension_semantics=sem, vmem_limit_bytes=VMEM_LIMIT)


def _dot(a, b, dims=None, precision=None):
    if dims is None:
        dims = (((a.ndim - 1,), (0,)), ((), ()))
    return lax.dot_general(a, b, dims, precision=precision, preferred_element_type=F32)


def _dot_nt(a, b, precision=None):
    return _dot(a, b, (((1,), (1,)), ((), ())), precision)


def _split3(x):
    x1 = x.astype(BF16)
    r1 = x - x1.astype(F32)
    x2 = r1.astype(BF16)
    x3 = (r1 - x2.astype(F32)).astype(BF16)
    return x1, x2, x3


def _ada_kernel(c_ref, w_ref, b_ref, o_ref):
    c = c_ref[...]
    a = (c * jax.nn.sigmoid(c)).astype(BF16)
    o_ref[0] = _dot(a, w_ref[0].astype(BF16)) + b_ref[0]


def _ada_mod(cc, ada_w, ada_b):
    L, D, N = ada_w.shape
    R = cc.shape[0]
    tn = 1536
    return pl.pallas_call(
        _ada_kernel,
        grid=(L, N // tn),
        in_specs=[pl.BlockSpec((R, D), lambda l, j: (0, 0)),
                  pl.BlockSpec((1, D, tn), lambda l, j: (l, 0, j)),
                  pl.BlockSpec((1, 1, tn), lambda l, j: (l, 0, j))],
        out_specs=pl.BlockSpec((1, R, tn), lambda l, j: (l, 0, j)),
        out_shape=jax.ShapeDtypeStruct((L, R, N), F32),
        compiler_params=_cparams(("parallel", "parallel")),
        name="ada_mod",
    )(cc, ada_w, ada_b.reshape(L, 1, N))


def _modnorm_kernel(x_ref, nw_ref, sh_ref, sc_ref, h_ref, *, groups):
    for g in range(groups):
        rows = pl.ds(g * GRP, GRP)
        x = x_ref[rows, :]
        y = x * lax.rsqrt(jnp.mean(x * x, axis=-1, keepdims=True) + NORM_EPS) * nw_ref[...]
        h_ref[rows, :] = (y * (1.0 + sc_ref[g]) + sh_ref[g]).astype(BF16)


def _modnorm(x, nw, modg, sh_blk, sc_blk):
    M, D = x.shape
    tm = _tile(M, 512)
    groups = tm // GRP
    return pl.pallas_call(
        functools.partial(_modnorm_kernel, groups=groups),
        grid=(M // tm,),
        in_specs=[pl.BlockSpec((tm, D), lambda i: (i, 0)),
                  pl.BlockSpec((1, D), lambda i: (0, 0)),
                  pl.BlockSpec((groups, 1, D), lambda i: (i, 0, sh_blk)),
                  pl.BlockSpec((groups, 1, D), lambda i: (i, 0, sc_blk))],
        out_specs=pl.BlockSpec((tm, D), lambda i: (i, 0)),
        out_shape=jax.ShapeDtypeStruct((M, D), BF16),
        compiler_params=_cparams(("parallel",)),
        name="modnorm",
    )(x, nw.reshape(1, D), modg, modg)


def _mm_kernel(a_ref, b_ref, o_ref):
    o_ref[...] = _dot(a_ref[...], b_ref[...]).astype(o_ref.dtype)


def _matmul(a, b, tm, tn, out_dtype, name):
    M, K = a.shape
    N = b.shape[1]
    return pl.pallas_call(
        _mm_kernel,
        grid=(M // tm, N // tn),
        in_specs=[pl.BlockSpec((tm, K), lambda i, j: (i, 0)),
                  pl.BlockSpec((K, tn), lambda i, j: (0, j))],
        out_specs=pl.BlockSpec((tm, tn), lambda i, j: (i, j)),
        out_shape=jax.ShapeDtypeStruct((M, N), out_dtype),
        compiler_params=_cparams(("parallel", "parallel")),
        name=name,
    )(a, b)


def _mm_res_kernel(a_ref, b_ref, res_ref, gate_ref, o_ref, *, groups):
    acc = _dot(a_ref[...], b_ref[...])
    for g in range(groups):
        rows = slice(g * GRP, (g + 1) * GRP)
        o_ref[rows, :] = res_ref[rows, :] + gate_ref[g] * acc[rows, :]


def _matmul_res(a, b, res, modg, gate_blk, tm, tn, name):
    M, K = a.shape
    N = b.shape[1]
    groups = tm // GRP
    nb = N // tn
    return pl.pallas_call(
        functools.partial(_mm_res_kernel, groups=groups),
        grid=(M // tm, nb),
        in_specs=[pl.BlockSpec((tm, K), lambda i, j: (i, 0)),
                  pl.BlockSpec((K, tn), lambda i, j: (0, j)),
                  pl.BlockSpec((tm, tn), lambda i, j: (i, j)),
                  pl.BlockSpec((groups, 1, tn), lambda i, j: (i, 0, gate_blk * nb + j))],
        out_specs=pl.BlockSpec((tm, tn), lambda i, j: (i, j)),
        out_shape=jax.ShapeDtypeStruct((M, N), F32),
        compiler_params=_cparams(("parallel", "parallel")),
        name=name,
    )(a, b, res, modg)


def _ffn_up_kernel(h_ref, w1_ref, w3_ref, o_ref):
    h = h_ref[...]
    a = _dot(h, w1_ref[...])
    b = _dot(h, w3_ref[...])
    o_ref[...] = (a * jax.nn.sigmoid(a) * b).astype(o_ref.dtype)


def _ffn_up(h, w1, w3, tm, tn):
    M, K = h.shape
    N = w1.shape[1]
    return pl.pallas_call(
        _ffn_up_kernel,
        grid=(M // tm, N // tn),
        in_specs=[pl.BlockSpec((tm, K), lambda i, j: (i, 0)),
                  pl.BlockSpec((K, tn), lambda i, j: (0, j)),
                  pl.BlockSpec((K, tn), lambda i, j: (0, j))],
        out_specs=pl.BlockSpec((tm, tn), lambda i, j: (i, j)),
        out_shape=jax.ShapeDtypeStruct((M, N), BF16),
        compiler_params=_cparams(("parallel", "parallel")),
        name="ffn_up",
    )(h, w1, w3)


def _merge_kernel(ya_ref, yb_ref, yc_ref, wa_ref, wb_ref, wc_ref, ga_ref, gb_ref, gc_ref, o_ref):
    m = jax.nn.sigmoid(ga_ref[...]) * _dot(ya_ref[...], wa_ref[...])
    m = m + jax.nn.sigmoid(gb_ref[...]) * _dot(yb_ref[...], wb_ref[...])
    m = m + jax.nn.sigmoid(gc_ref[...]) * _dot(yc_ref[...], wc_ref[...])
    o_ref[...] = m.astype(o_ref.dtype)


def _merge(ya, yb, yc, wa, wb, wc, z, tm, tn=512):
    M = ya.shape[0]
    D = wa.shape[1]
    nb = D // tn
    g0 = ZG_OFF // tn
    return pl.pallas_call(
        _merge_kernel,
        grid=(M // tm, nb),
        in_specs=[pl.BlockSpec((tm, ya.shape[1]), lambda i, j: (i, 0)),
                  pl.BlockSpec((tm, yb.shape[1]), lambda i, j: (i, 0)),
                  pl.BlockSpec((tm, yc.shape[1]), lambda i, j: (i, 0)),
                  pl.BlockSpec((wa.shape[0], tn), lambda i, j: (0, j)),
                  pl.BlockSpec((wb.shape[0], tn), lambda i, j: (0, j)),
                  pl.BlockSpec((wc.shape[0], tn), lambda i, j: (0, j)),
                  pl.BlockSpec((tm, tn), lambda i, j: (i, g0 + j)),
                  pl.BlockSpec((tm, tn), lambda i, j: (i, g0 + nb + j)),
                  pl.BlockSpec((tm, tn), lambda i, j: (i, g0 + 2 * nb + j))],
        out_specs=pl.BlockSpec((tm, tn), lambda i, j: (i, j)),
        out_shape=jax.ShapeDtypeStruct((M, D), BF16),
        compiler_params=_cparams(("parallel", "parallel")),
        name="merge",
    )(ya, yb, yc, wa, wb, wc, z, z, z)


def _final_norm_kernel(x_ref, w_ref, o_ref):
    x = x_ref[0]
    o_ref[0] = x * lax.rsqrt(jnp.mean(x * x, axis=-1, keepdims=True) + NORM_EPS) * w_ref[...]


def _final_norm(xt, w, ctx_len):
    B, T, D = xt.shape
    S = T - ctx_len
    cg = ctx_len // GRP
    return pl.pallas_call(
        _final_norm_kernel,
        grid=(B, S // GRP),
        in_specs=[pl.BlockSpec((1, GRP, D), lambda b, i: (b, cg + i, 0)),
                  pl.BlockSpec((1, D), lambda b, i: (0, 0))],
        out_specs=pl.BlockSpec((1, GRP, D), lambda b, i: (b, i, 0)),
        out_shape=jax.ShapeDtypeStruct((B, S, D), F32),
        compiler_params=_cparams(("parallel", "parallel")),
        name="final_norm",
    )(xt, w.reshape(1, D))


def _gqa_prep_kernel(z_ref, cos_ref, sin_ref, gq_ref, gk_ref, q_ref, k_ref, v_ref):
    cosf = cos_ref[...]
    sinf = sin_ref[...]

    def norm_rope(x, gain):
        y = x * lax.rsqrt(jnp.mean(x * x, axis=-1, keepdims=True) + NORM_EPS) * gain
        return y * cosf + pltpu.roll(y, GQ_HEAD // 2, axis=1) * sinf

    scale = GQ_HEAD ** -0.5
    for h in range(GQ_HEADS):
        cols = slice(h * GQ_HEAD, (h + 1) * GQ_HEAD)
        q_ref[:, cols] = (norm_rope(z_ref[:, cols], gq_ref[...]) * scale).astype(BF16)
    for h in range(GQ_KV_HEADS):
        cols = slice(h * GQ_HEAD, (h + 1) * GQ_HEAD)
        zc = slice(GQ_WIDTH + h * GQ_HEAD, GQ_WIDTH + (h + 1) * GQ_HEAD)
        k_ref[:, cols] = norm_rope(z_ref[:, zc], gk_ref[...]).astype(BF16)
    v_ref[...] = z_ref[:, GQ_WIDTH + GQ_KV_WIDTH:].astype(BF16)


def _gqa_prep(z, cosf, sinf, gq, gk, gps):
    M = z.shape[0]
    zw = GQ_WIDTH + 2 * GQ_KV_WIDTH
    return pl.pallas_call(
        _gqa_prep_kernel,
        grid=(M // GRP,),
        in_specs=[pl.BlockSpec((GRP, zw), lambda i: (i, ZQ_OFF // zw)),
                  pl.BlockSpec((GRP, GQ_HEAD), lambda i: (i % gps, 0)),
                  pl.BlockSpec((GRP, GQ_HEAD), lambda i: (i % gps, 0)),
                  pl.BlockSpec((1, GQ_HEAD), lambda i: (0, 0)),
                  pl.BlockSpec((1, GQ_HEAD), lambda i: (0, 0))],
        out_specs=[pl.BlockSpec((GRP, GQ_WIDTH), lambda i: (i, 0)),
                   pl.BlockSpec((GRP, GQ_KV_WIDTH), lambda i: (i, 0)),
                   pl.BlockSpec((GRP, GQ_KV_WIDTH), lambda i: (i, 0))],
        out_shape=[jax.ShapeDtypeStruct((M, GQ_WIDTH), BF16),
                   jax.ShapeDtypeStruct((M, GQ_KV_WIDTH), BF16),
                   jax.ShapeDtypeStruct((M, GQ_KV_WIDTH), BF16)],
        compiler_params=_cparams(("parallel",)),
        name="gqa_prep",
    )(z, cosf, sinf, gq.reshape(1, GQ_HEAD), gk.reshape(1, GQ_HEAD))


def _gqa_attn_kernel(q_ref, k_ref, v_ref, o_ref, *, ctx_tiles, ctx_len, n_keys):
    qi = pl.program_id(2)

    def attend(nk):
        k = k_ref[0, 0:nk, :]
        v = v_ref[0, 0:nk, :]
        for h in range(GQ_GROUP):
            cols = slice(h * GQ_HEAD, (h + 1) * GQ_HEAD)
            s = _dot_nt(q_ref[0, :, cols], k)
            p = jnp.exp(s - jnp.max(s, axis=-1, keepdims=True))
            o = _dot(p.astype(BF16), v) / jnp.sum(p, axis=-1, keepdims=True)
            o_ref[0, :, cols] = o.astype(o_ref.dtype)

    @pl.when(qi < ctx_tiles)
    def _():
        attend(ctx_len)

    @pl.when(qi >= ctx_tiles)
    def _():
        attend(n_keys)


def _gqa_attn(q, k, v, ctx_len, tq=256):
    B, T, _ = q.shape
    gw = GQ_GROUP * GQ_HEAD
    kern = functools.partial(_gqa_attn_kernel, ctx_tiles=ctx_len // tq, ctx_len=ctx_len, n_keys=T)
    return pl.pallas_call(
        kern,
        grid=(B, GQ_KV_HEADS, T // tq),
        in_specs=[pl.BlockSpec((1, tq, gw), lambda b, g, i: (b, i, g)),
                  pl.BlockSpec((1, T, GQ_HEAD), lambda b, g, i: (b, 0, g)),
                  pl.BlockSpec((1, T, GQ_HEAD), lambda b, g, i: (b, 0, g))],
        out_specs=pl.BlockSpec((1, tq, gw), lambda b, g, i: (b, i, g)),
        out_shape=jax.ShapeDtypeStruct((B, T, GQ_WIDTH), BF16),
        compiler_params=_cparams(("parallel", "parallel", "arbitrary")),
        name="gqa_attn",
    )(q, k, v)


def _na_kernel(q_ref, k_ref, v_ref, bias_ref, o_ref, qs_ref, ks_ref, vs_ref, *, ctx_len, rows):
    scale = NA_HEAD ** -0.5
    wr = min(NA_WIN_ROWS, rows)
    nwin = wr * GRID_W
    rpg = math.gcd(rows, 8)

    for hh in range(2):
        cols = slice(hh * NA_HEAD, (hh + 1) * NA_HEAD)
        qs_ref[hh] = (q_ref[0, :, cols] * scale).astype(BF16)
        ks_ref[hh] = k_ref[0, :, cols].astype(BF16)
        vs_ref[hh] = v_ref[0, :, cols].astype(BF16)

    for hh in range(2):
        cols = slice(hh * NA_HEAD, (hh + 1) * NA_HEAD)
        kc = ks_ref[hh, 0:ctx_len, :]
        vc = vs_ref[hh, 0:ctx_len, :]

        s = _dot_nt(qs_ref[hh, 0:ctx_len, :], kc)
        p = jnp.exp(s - jnp.max(s, axis=-1, keepdims=True))
        o = _dot(p.astype(BF16), vc) / jnp.sum(p, axis=-1, keepdims=True)
        o_ref[0, 0:ctx_len, cols] = o.astype(o_ref.dtype)

        def group_body(g, carry):
            i0 = g * rpg
            qoff = pl.multiple_of(ctx_len + i0 * GRID_W, rpg * GRID_W)
            qg = qs_ref[hh, pl.ds(qoff, rpg * GRID_W), :]
            s2 = _dot_nt(qg, kc)
            koffs, s1 = [], []
            for r in range(rpg):
                rs = jnp.clip(i0 + r - wr // 2, 0, rows - wr)
                koffs.append(pl.multiple_of(ctx_len + rs * GRID_W, GRID_W))
                kw = ks_ref[hh, pl.ds(koffs[r], nwin), :]
                s1.append(_dot_nt(qg[r * GRID_W:(r + 1) * GRID_W], kw) + bias_ref[hh, i0 + r - rs])
            s1 = jnp.concatenate(s1, axis=0)
            m = jnp.maximum(jnp.max(s1, axis=-1, keepdims=True), jnp.max(s2, axis=-1, keepdims=True))
            p1 = jnp.exp(s1 - m)
            p2 = jnp.exp(s2 - m)
            l = jnp.sum(p1, axis=-1, keepdims=True) + jnp.sum(p2, axis=-1, keepdims=True)
            p1 = p1.astype(BF16)
            o1 = [_dot(p1[r * GRID_W:(r + 1) * GRID_W], vs_ref[hh, pl.ds(koffs[r], nwin), :]) for r in range(rpg)]
            o = (jnp.concatenate(o1, axis=0) + _dot(p2.astype(BF16), vc)) / l
            o_ref[0, pl.ds(qoff, rpg * GRID_W), cols] = o.astype(o_ref.dtype)
            return carry

        lax.fori_loop(0, rows // rpg, group_body, 0)


def _na_bias_table(rpb, rows):
    wr = min(NA_WIN_ROWS, rows)
    cols = np.arange(GRID_W)
    cstart = np.clip(cols - NA_WIN_COLS // 2, 0, GRID_W - NA_WIN_COLS)
    c = np.arange(GRID_W)
    inside = (c[None, :] >= cstart[:, None]) & (c[None, :] < cstart[:, None] + NA_WIN_COLS)
    col_off = np.clip(c[None, :] - cols[:, None] + (NA_WIN_COLS - 1), 0, 2 * NA_WIN_COLS - 2)
    onehot = (np.arange(2 * NA_WIN_COLS - 1)[:, None, None] == col_off[None]).astype(np.float32)
    tc = jnp.einsum('...ro,ojc->...rjc', rpb, onehot, precision=lax.Precision.HIGHEST)
    tc = jnp.where(inside, tc, NEG_BIG)
    t = jnp.stack([tc[..., NA_WIN_ROWS - 1 - d:NA_WIN_ROWS - 1 - d + wr, :, :] for d in range(wr)], axis=-4)
    t = jnp.swapaxes(t, -3, -2)
    return t.reshape(rpb.shape[:-2] + (wr, GRID_W, wr * GRID_W))


def _na_attn(z, bias, ctx_len):
    B, T, _ = z.shape
    rows = (T - ctx_len) // GRID_W
    wr = bias.shape[1]
    pw = 2 * NA_HEAD
    c0 = ZN_OFF // pw
    per = NA_WIDTH // pw
    scr = pltpu.VMEM((2, T, NA_HEAD), BF16)
    return pl.pallas_call(
        functools.partial(_na_kernel, ctx_len=ctx_len, rows=rows),
        grid=(B, per),
        in_specs=[pl.BlockSpec((1, T, pw), lambda b, h: (b, 0, c0 + h)),
                  pl.BlockSpec((1, T, pw), lambda b, h: (b, 0, c0 + per + h)),
                  pl.BlockSpec((1, T, pw), lambda b, h: (b, 0, c0 + 2 * per + h)),
                  pl.BlockSpec((2, wr, GRID_W, wr * GRID_W), lambda b, h: (h, 0, 0, 0))],
        out_specs=pl.BlockSpec((1, T, pw), lambda b, h: (b, 0, h)),
        out_shape=jax.ShapeDtypeStruct((B, T, NA_WIDTH), BF16),
        scratch_shapes=[scr, scr, scr],
        compiler_params=_cparams(("parallel", "parallel")),
        name="na_attn",
    )(z, z, z, bias)


def _rw_prep_kernel(z_ref, zp_ref, zn_ref, l_ref, lp_ref, ln_ref,
                    mup_ref, mun_ref, lmup_ref, lmun_ref, kk_ref_, ka_ref, w0_ref, wup_ref, a0_ref, aup_ref,
                    gup_ref,
                    r_out, v_out, kk_out, g_out, cs_out, lw_out, kd_out, bd_out, *, gps, ctx_groups):
    t = pl.program_id(1)
    first = jnp.logical_or(t == 0, t == ctx_groups)
    last = jnp.logical_or(t == ctx_groups - 1, t == gps - 1)
    pv = jnp.where(first, 0.0, 1.0)
    nv = jnp.where(last, 0.0, 1.0)
    row = lax.broadcasted_iota(jnp.int32, (GRP, 1), 0)

    def shift(x_ref, p_ref, n_ref, mp_ref, mn_ref):
        x = x_ref[...]
        xp = jnp.where(row == 0, p_ref[7:8, :] * pv, pltpu.roll(x, 1, axis=0))
        xn = jnp.where(row == GRP - 1, n_ref[0:1, :] * nv, pltpu.roll(x, GRP - 1, axis=0))
        return x + mp_ref[...] * (xp - x) + mn_ref[...] * (xn - x)

    zs = shift(z_ref, zp_ref, zn_ref, mup_ref, mun_ref)
    ls = shift(l_ref, lp_ref, ln_ref, lmup_ref, lmun_ref)
    r = zs[:, 0:RW_WIDTH]
    k = zs[:, RW_WIDTH:2 * RW_WIDTH]
    v = zs[:, 2 * RW_WIDTH:3 * RW_WIDTH]

    hi = lax.Precision.HIGHEST
    ci = lax.broadcasted_iota(jnp.int32, (RW_WIDTH, RW_WIDTH), 0) // RW_HEAD
    cj = lax.broadcasted_iota(jnp.int32, (RW_WIDTH, RW_WIDTH), 1) // RW_HEAD
    head_ones = jnp.where(ci == cj, 1.0, 0.0).astype(BF16)
    kk = k * kk_ref_[...]
    ss = sum(_dot(part, head_ones) for part in _split3(kk * kk))
    kk = kk / jnp.maximum(jnp.sqrt(ss), 1e-12)

    ti = lax.broadcasted_iota(jnp.int32, (GRP, GRP), 0)
    tj = lax.broadcasted_iota(jnp.int32, (GRP, GRP), 1)
    same = (ti // SCAN_CHUNK) == (tj // SCAN_CHUNK)

    def heads_out(ref, val, lead):
        for h in range(RW_HEADS):
            ref[lead + (h,)] = val[:, h * RW_HEAD:(h + 1) * RW_HEAD]

    heads_out(r_out, r, (0,))
    heads_out(v_out, v, (0,))
    heads_out(kk_out, kk, (0,))
    gd = ls[:, 4 * LORA_PAD:]
    g_out[...] = _dot(jax.nn.sigmoid(gd).astype(BF16), gup_ref[...].astype(BF16))

    for d in range(2):
        wl = ls[:, d * LORA_PAD:(d + 1) * LORA_PAD]
        al = ls[:, (2 + d) * LORA_PAD:(3 + d) * LORA_PAD]
        w_raw = w0_ref[d] + _dot(jnp.tanh(wl), wup_ref[d], precision=hi)
        lw = -math.exp(-0.5) * jax.nn.sigmoid(w_raw)
        a = jax.nn.sigmoid(a0_ref[d] + _dot(al.astype(BF16), aup_ref[d].astype(BF16)))
        kd = k * (1.0 + (a - 1.0) * ka_ref[...])
        bd = kk * a
        tri = jnp.where(jnp.logical_and(same, (ti >= tj) if d == 0 else (ti <= tj)), 1.0, 0.0).astype(BF16)
        cs = sum(_dot(tri, part) for part in _split3(lw))
        heads_out(cs_out, cs, (d, 0))
        heads_out(lw_out, lw, (d, 0))
        heads_out(kd_out, kd, (d, 0))
        heads_out(bd_out, bd, (d, 0))


def _rw_prep(z, p, B, gps, ctx_groups):
    M = z.shape[0]
    T = gps * GRP
    nb8 = M // 8
    rb = GRP // 8
    zc = ZRKV_OFF // 1536
    lc = ZL_OFF // ZL_W

    def cur(c):
        return lambda b, t: (b * gps + t, c)

    def prev(c):
        return lambda b, t: (jnp.maximum((b * gps + t) * rb - 1, 0), c)

    def nxt(c):
        return lambda b, t: (jnp.minimum((b * gps + t + 1) * rb, nb8 - 1), c)

    def const(shape):
        nd = len(shape)
        return pl.BlockSpec(shape, lambda b, t: (0,) * nd)

    hm = pl.BlockSpec((1, RW_HEADS, GRP, RW_HEAD), lambda b, t: (b, 0, t, 0))
    hm2 = pl.BlockSpec((2, 1, RW_HEADS, GRP, RW_HEAD), lambda b, t: (0, b, 0, t, 0))
    hshape = jax.ShapeDtypeStruct((B, RW_HEADS, T, RW_HEAD), F32)
    hshape2 = jax.ShapeDtypeStruct((2, B, RW_HEADS, T, RW_HEAD), F32)
    return pl.pallas_call(
        functools.partial(_rw_prep_kernel, gps=gps, ctx_groups=ctx_groups),
        grid=(B, gps),
        in_specs=[pl.BlockSpec((GRP, 1536), cur(zc)),
                  pl.BlockSpec((8, 1536), prev(zc)),
                  pl.BlockSpec((8, 1536), nxt(zc)),
                  pl.BlockSpec((GRP, ZL_W), cur(lc)),
                  pl.BlockSpec((8, ZL_W), prev(lc)),
                  pl.BlockSpec((8, ZL_W), nxt(lc)),
                  const((1, 1536)), const((1, 1536)), const((1, ZL_W)), const((1, ZL_W)),
                  const((1, RW_WIDTH)), const((1, RW_WIDTH)),
                  const((2, 1, RW_WIDTH)), const((2, LORA_PAD, RW_WIDTH)),
                  const((2, 1, RW_WIDTH)), const((2, LORA_PAD, RW_WIDTH)),
                  const((256, RW_WIDTH))],
        out_specs=[hm, hm, hm, pl.BlockSpec((GRP, RW_WIDTH), lambda b, t: (b * gps + t, 0)),
                   hm2, hm2, hm2, hm2],
        out_shape=[hshape, hshape, hshape, jax.ShapeDtypeStruct((M, RW_WIDTH), F32),
                   hshape2, hshape2, hshape2, hshape2],
        compiler_params=_cparams(("parallel", "parallel")),
        name="rw_prep",
    )(z, z, z, z, z, z, p['mu_p'], p['mu_n'], p['lmu_p'], p['lmu_n'], p['k_k'], p['k_a'],
      p['w0'], p['w_up'], p['a0'], p['a_up'], p['g_up'])


def _bmm(a, b, nt=False):
    a = a.astype(BF16)
    b = b.astype(BF16)
    out = []
    for h in range(a.shape[0]):
        out.append(_dot_nt(a[h], b[h]) if nt else _dot(a[h], b[h]))
    return jnp.stack(out, axis=0)


def _bmm_tn(a, b):
    a = a.astype(BF16)
    b = b.astype(BF16)
    out = []
    for h in range(a.shape[0]):
        out.append(_dot(a[h], b[h], (((0,), (0,)), ((), ()))))
    return jnp.stack(out, axis=0)


def _rw_intra_kernel(r_ref, v_ref, kk_ref, cs_ref, lw_ref, kd_ref, bd_ref,
                     rh_ref, o0_ref, gm_ref, sd_ref, pc_ref, *, nck):
    d = pl.program_id(1)
    C, H, N = SCAN_CHUNK, RW_HEADS, RW_HEAD
    G = H * nck

    fwd = d == 0
    ti = lax.broadcasted_iota(jnp.int32, (C, C), 0)
    tj = lax.broadcasted_iota(jnp.int32, (C, C), 1)
    before = (ti - tj) * jnp.where(fwd, 1, -1)
    strict = (before > 0)[None]
    incl = (before >= 0)[None]
    eye = (ti == tj)[None]

    def chunks(x):
        return x.reshape(G, C, N)

    r = chunks(r_ref[0])
    v = chunks(v_ref[0])
    cs = chunks(cs_ref[0, 0])
    pinv = jnp.exp(-cs)
    ra = r * jnp.exp(cs)
    at = -chunks(kk_ref[0]) * jnp.exp(cs - chunks(lw_ref[0, 0]))
    bt = chunks(bd_ref[0, 0]) * pinv
    kt = chunks(kd_ref[0, 0]) * pinv
    pc = jnp.exp(jnp.where(fwd, cs[:, C - 1:C, :], cs[:, 0:1, :]))

    lhs = jnp.concatenate([at, ra], axis=1)
    rhs = jnp.concatenate([bt, kt], axis=1)
    gmat = _bmm(lhs, rhs, nt=True)
    a_ab = jnp.where(strict, gmat[:, :C, :C], 0.0)
    a_ak = jnp.where(strict, gmat[:, :C, C:], 0.0)
    a_rb = jnp.where(incl, gmat[:, C:, :C], 0.0)
    a_rk = jnp.where(incl, gmat[:, C:, C:], 0.0)

    tm = jnp.where(eye, 1.0, 0.0) + a_ab
    pw = a_ab
    for _ in range(int(math.log2(C)) - 1):
        pw = _bmm(pw, pw)
        tm = tm + _bmm(tm, pw)

    au = _bmm(tm, jnp.concatenate([at, _bmm(a_ak, v)], axis=2))
    ro = _bmm(a_rb, au)
    gs = _bmm_tn(au, bt * pc)

    def unchunk(x):
        return x.reshape(H, nck * C, N)

    rh_ref[0, 0] = unchunk(ra + ro[:, :, :N])
    o0_ref[0, 0] = unchunk(ro[:, :, N:] + _bmm(a_rk, v))
    gm_ref[0, 0] = unchunk(gs[:, :N])
    sd_ref[0, 0] = unchunk(gs[:, N:] + _bmm_tn(v, kt * pc))
    pc_ref[0, 0, 0] = pc.reshape(H, nck, N)


def _rw_seq_kernel(rhf_ref, o0f_ref, gmf_ref, sdf_ref, pcf_ref, rhb_ref, o0b_ref, gmb_ref, sdb_ref, pcb_ref,
                   of_ref, ob_ref, s_ref, *, nck):
    C = SCAN_CHUNK

    @pl.when(pl.program_id(1) == 0)
    def _():
        s_ref[...] = jnp.zeros(s_ref.shape, F32)

    dirs = ((rhf_ref, o0f_ref, gmf_ref, sdf_ref, pcf_ref, of_ref, range(nck)),
            (rhb_ref, o0b_ref, gmb_ref, sdb_ref, pcb_ref, ob_ref, range(nck - 1, -1, -1)))
    for d, (rh_ref, o0_ref, gm_ref, sd_ref, pc_ref, o_ref, order) in enumerate(dirs):
        s = s_ref[d]
        for c in order:
            rows = slice(c * C, (c + 1) * C)
            o_ref[0, :, rows, :] = _bmm(rh_ref[0, 0, :, rows, :], s, nt=True) + o0_ref[0, 0, :, rows, :]
            s = s * pc_ref[0, 0, 0, :, c:c + 1, :] + _bmm(s, gm_ref[0, 0, :, rows, :]) + sd_ref[0, 0, :, rows, :]
        s_ref[d] = s


def _rw_scan(r, v, kk, cs, lw, kd, bd, ctx_len, nck=4):
    B, H, T, N = r.shape
    ct = nck * SCAN_CHUNK
    nblk = T // ct
    ncb = ctx_len // ct
    assert ctx_len % ct == 0 and T % ct == 0

    shared = pl.BlockSpec((1, H, ct, N), lambda b, d, j: (b, 0, j, 0))
    per_dir = pl.BlockSpec((1, 1, H, ct, N), lambda b, d, j: (d, b, 0, j, 0))
    pc_spec = pl.BlockSpec((1, 1, 1, H, nck, N), lambda b, d, j: (d, b, j, 0, 0, 0))
    big = jax.ShapeDtypeStruct((2, B, H, T, N), F32)
    rh, o0, gm, sd, pc = pl.pallas_call(
        functools.partial(_rw_intra_kernel, nck=nck),
        grid=(B, 2, nblk),
        in_specs=[shared, shared, shared, per_dir, per_dir, per_dir, per_dir],
        out_specs=[per_dir, per_dir, per_dir, per_dir, pc_spec],
        out_shape=[big, big, big, big, jax.ShapeDtypeStruct((2, B, nblk, H, nck, N), F32)],
        compiler_params=_cparams(("parallel", "parallel", "parallel")),
        name="rw_intra",
    )(r, v, kk, cs, lw, kd, bd)

    def bwd_blk(j):
        return jnp.where(j < ncb, ncb - 1 - j, nblk - 1 - (j - ncb))

    f_big = pl.BlockSpec((1, 1, H, ct, N), lambda b, j: (0, b, 0, j, 0))
    b_big = pl.BlockSpec((1, 1, H, ct, N), lambda b, j: (1, b, 0, bwd_blk(j), 0))
    f_pc = pl.BlockSpec((1, 1, 1, H, nck, N), lambda b, j: (0, b, j, 0, 0, 0))
    b_pc = pl.BlockSpec((1, 1, 1, H, nck, N), lambda b, j: (1, b, bwd_blk(j), 0, 0, 0))
    out = jax.ShapeDtypeStruct((B, H, T, N), F32)
    return pl.pallas_call(
        functools.partial(_rw_seq_kernel, nck=nck),
        grid=(B, nblk),
        in_specs=[f_big, f_big, f_big, f_big, f_pc, b_big, b_big, b_big, b_big, b_pc],
        out_specs=[pl.BlockSpec((1, H, ct, N), lambda b, j: (b, 0, j, 0)),
                   pl.BlockSpec((1, H, ct, N), lambda b, j: (b, 0, bwd_blk(j), 0))],
        out_shape=[out, out],
        scratch_shapes=[pltpu.VMEM((2, H, N, N), F32)],
        compiler_params=_cparams(("parallel", "arbitrary")),
        name="rw_seq",
    )(rh, o0, gm, sd, pc, rh, o0, gm, sd, pc)


def _rw_out_kernel(of_ref, ob_ref, r_ref, v_ref, kd_ref, g_ref, rk_ref, lnw_ref, lnb_ref, y_ref):
    ys = []
    for h in range(RW_HEADS):
        wkv = of_ref[0, h] + ob_ref[0, h]
        mu = jnp.mean(wkv, axis=-1, keepdims=True)
        cen = wkv - mu
        var = jnp.mean(cen * cen, axis=-1, keepdims=True)
        y = cen * lax.rsqrt(var + RW_GN_EPS) * lnw_ref[h] + lnb_ref[h]
        r = r_ref[0, h]
        rk = rk_ref[h]
        bonus = (jnp.sum(r * kd_ref[0, 0, h] * rk, axis=-1, keepdims=True)
                 + jnp.sum(r * kd_ref[1, 0, h] * rk, axis=-1, keepdims=True)) * v_ref[0, h]
        ys.append(y + bonus)
    y_ref[...] = (jnp.concatenate(ys, axis=-1) * g_ref[...]).astype(y_ref.dtype)


def _rw_out(o_f, o_b, r, v, kd, g, rk, lnw, lnb, gps):
    B, H, T, N = r.shape
    M = B * T
    hm = pl.BlockSpec((1, H, GRP, N), lambda b, t: (b, 0, t, 0))
    hm2 = pl.BlockSpec((2, 1, H, GRP, N), lambda b, t: (0, b, 0, t, 0))
    par = pl.BlockSpec((H, 1, N), lambda b, t: (0, 0, 0))
    return pl.pallas_call(
        _rw_out_kernel,
        grid=(B, gps),
        in_specs=[hm, hm, hm, hm, hm2, pl.BlockSpec((GRP, RW_WIDTH), lambda b, t: (b * gps + t, 0)),
                  par, par, par],
        out_specs=pl.BlockSpec((GRP, RW_WIDTH), lambda b, t: (b * gps + t, 0)),
        out_shape=jax.ShapeDtypeStruct((M, RW_WIDTH), BF16),
        compiler_params=_cparams(("parallel", "parallel")),
        name="rw_out",
    )(o_f, o_b, r, v, kd, g, rk.reshape(H, 1, N), lnw.reshape(H, 1, N), lnb.reshape(H, 1, N))


def _pad_rows(w, n):
    return jnp.pad(w, [(0, 0)] * (w.ndim - 2) + [(0, n - w.shape[-2]), (0, 0)])


def _pad_cols(w, n):
    return jnp.pad(w, [(0, 0)] * (w.ndim - 1) + [(0, n - w.shape[-1])])


def _deinterleave(w):
    n = w.shape[-1] // GQ_HEAD
    w = w.reshape(w.shape[:-1] + (n, GQ_HEAD // 2, 2))
    return jnp.swapaxes(w, -1, -2).reshape(w.shape[:-3] + (n * GQ_HEAD,))


def _in_layout(w, lora):
    rkv = w[..., 0:1536]
    lo = [w[..., 1536 + i * lora: 1536 + (i + 1) * lora] for i in range(4)]
    gd = w[..., 1536 + 4 * lora: 1536 + 4 * lora + 256]
    o = 1536 + 4 * lora + 256
    zq = w[..., o:o + 1536]
    zn = w[..., o + 1536:o + 3072]
    zg = w[..., o + 3072:]
    zq = jnp.concatenate([_deinterleave(zq[..., :GQ_WIDTH + GQ_KV_WIDTH]), zq[..., GQ_WIDTH + GQ_KV_WIDTH:]], axis=-1)
    return jnp.concatenate([zg, rkv, zq, zn] + [_pad_cols(x, LORA_PAD) for x in lo] + [gd], axis=-1)


def _shift_layout(mu, lora):
    lo = [_pad_cols(mu[..., 1536 + i * lora: 1536 + (i + 1) * lora], LORA_PAD) for i in range(4)]
    return mu[..., 0:1536], jnp.concatenate(lo + [mu[..., 1536 + 4 * lora:]], axis=-1)


def _tile(n, pref):
    t = pref
    while n % t:
        t //= 2
    return t


def _axial_tables(ctx_len, seq):
    n_freq = GQ_HEAD // 4
    inv = ROPE_THETA ** (-jnp.arange(n_freq, dtype=F32) / n_freq)
    t = jnp.arange(seq, dtype=jnp.int32)
    row = (t // GRID_W).astype(F32)
    col = (t % GRID_W).astype(F32)
    ang = jnp.concatenate([row[:, None] * inv, col[:, None] * inv], axis=-1)
    cos, sin = jnp.cos(ang), jnp.sin(ang)
    cosf = jnp.concatenate([jnp.ones((ctx_len, GQ_HEAD), F32), jnp.concatenate([cos, cos], axis=-1)], axis=0)
    sinf = jnp.concatenate([jnp.zeros((ctx_len, GQ_HEAD), F32), jnp.concatenate([-sin, sin], axis=-1)], axis=0)
    return cosf, sinf


def kernel(x, c, ctx, c_ctx, ada_w, ada_b, norm1, norm2, w_in, rw_mu_prev, rw_mu_next, rw_w0, rw_w_up, rw_a0, rw_a_up, rw_g_up, rw_k_k, rw_k_a, rw_r_k, rw_ln_w, rw_ln_b, gq_q_norm, gq_k_norm, na_rpb, w_br_a, w_br_b, w_br_c, w_out, ffn_w1, ffn_w3, ffn_w2, final_norm):
    B, S, D = x.shape
    C = ctx.shape[1]
    L = ada_w.shape[0]
    T = C + S
    M = B * T
    gps = T // GRP
    cgr = C // GRP
    lora = rw_w_up.shape[2]
    assert C % GRP == 0 and S % GRP == 0 and S % GRID_W == 0
    assert w_in.shape[2] == 1536 + 4 * lora + 256 + 1536 + 1536 + 3 * D
    tm = _tile(M, 1024)

    w_in_p = _in_layout(w_in, lora).astype(BF16)
    mu_p, lmu_p = _shift_layout(rw_mu_prev, lora)
    mu_n, lmu_n = _shift_layout(rw_mu_next, lora)
    w_up = _pad_rows(rw_w_up, LORA_PAD)
    a_up = _pad_rows(rw_a_up, LORA_PAD)
    gq = _deinterleave(gq_q_norm)
    gk = _deinterleave(gq_k_norm)
    wa = w_br_a.astype(BF16)
    wb = w_br_b.astype(BF16)
    wc = w_br_c.astype(BF16)
    wo = w_out.astype(BF16)
    w1 = ffn_w1.astype(BF16)
    w3 = ffn_w3.astype(BF16)
    w2 = ffn_w2.astype(BF16)
    cosf, sinf = _axial_tables(C, S)
    na_bias = _na_bias_table(na_rpb, S // GRID_W)

    cc = jnp.concatenate([c, c_ctx[None, :], jnp.zeros((-(B + 1) % 8, D), F32)], axis=0)
    mod = _ada_mod(cc, ada_w, ada_b)
    modg = jnp.concatenate([jnp.broadcast_to(mod[:, B, None, None, :], (L, B, cgr, 6 * D)),
                            jnp.broadcast_to(mod[:, :B, None, :], (L, B, gps - cgr, 6 * D))], axis=2)
    modg = modg.reshape(L, B * gps, 1, 6 * D)

    xt = jnp.concatenate([ctx, x], axis=1).reshape(M, D)

    for l in range(L):
        mg = modg[l]
        h = _modnorm(xt, norm1[l], mg, 0, 1)
        z = _matmul(h, w_in_p[l], tm, 768, F32, "in_proj")

        pr = {'mu_p': mu_p[l][None], 'mu_n': mu_n[l][None], 'lmu_p': lmu_p[l][None], 'lmu_n': lmu_n[l][None],
              'k_k': rw_k_k[l][None], 'k_a': rw_k_a[l][None],
              'w0': rw_w0[l][:, None, :], 'w_up': w_up[l], 'a0': rw_a0[l][:, None, :], 'a_up': a_up[l],
              'g_up': rw_g_up[l]}
        r_, v_, kk_, g_, cs_, lw_, kd_, bd_ = _rw_prep(z, pr, B, gps, cgr)
        of_, ob_ = _rw_scan(r_, v_, kk_, cs_, lw_, kd_, bd_, C)
        ya = _rw_out(of_, ob_, r_, v_, kd_, g_, rw_r_k[l], rw_ln_w[l], rw_ln_b[l], gps)

        q, k, v = _gqa_prep(z, cosf, sinf, gq[l], gk[l], gps)
        yb = _gqa_attn(q.reshape(B, T, GQ_WIDTH), k.reshape(B, T, GQ_KV_WIDTH), v.reshape(B, T, GQ_KV_WIDTH), C)
        yb = yb.reshape(M, GQ_WIDTH)

        yc = _na_attn(z.reshape(B, T, Z_W), na_bias[l], C).reshape(M, NA_WIDTH)

        m = _merge(ya, yb, yc, wa[l], wb[l], wc[l], z, tm)
        xt = _matmul_res(m, wo[l], xt, mg, 2, tm, 512, "out_proj")
        h = _modnorm(xt, norm2[l], mg, 3, 4)
        u = _ffn_up(h, w1[l], w3[l], tm, 512)
        xt = _matmul_res(u, w2[l], xt, mg, 5, _tile(M, 512), 512, "ffn_down")

    return _final_norm(xt.reshape(B, T, D), final_norm, C)
```

```python
import functools
import math

import numpy as np
import jax
import jax.numpy as jnp
from jax import lax
from jax.experimental import pallas as pl
from jax.experimental.pallas import tpu as pltpu

F32 = jnp.float32
BF16 = jnp.bfloat16

GRP = 256
GRID_W = 64
NORM_EPS = 1e-6

RW_HEAD = 64
RW_HEADS = 8
RW_WIDTH = RW_HEAD * RW_HEADS
LORA_PAD = 128
RW_GN_EPS = 64e-5
SCAN_CHUNK = 64

GQ_HEAD = 128
GQ_HEADS = 8
GQ_KV_HEADS = 2
GQ_GROUP = GQ_HEADS // GQ_KV_HEADS
GQ_WIDTH = GQ_HEAD * GQ_HEADS
GQ_KV_WIDTH = GQ_HEAD * GQ_KV_HEADS
ROPE_THETA = 10000.0

NA_HEAD = 64
NA_HEADS = 8
NA_WIDTH = NA_HEAD * NA_HEADS
NA_WIN_ROWS = 8
NA_WIN_COLS = 16
NEG_BIG = -1e30

VMEM_LIMIT = 56 * 1024 * 1024

ZG_OFF, ZG_W = 0, 6144
ZRKV_OFF = 6144
ZQ_OFF = ZRKV_OFF + 1536
ZN_OFF = ZQ_OFF + 1536
ZL_OFF = ZN_OFF + 1536
ZL_W = 4 * LORA_PAD + 256
Z_W = ZL_OFF + ZL_W


def _cparams(sem):
    return pltpu.CompilerParams(dimension_semantics=sem, vmem_limit_bytes=VMEM_LIMIT)


def _dot(a, b, dims=None, precision=None):
    if dims is None:
        dims = (((a.ndim - 1,), (0,)), ((), ()))
    return lax.dot_general(a, b, dims, precision=precision, preferred_element_type=F32)


def _dot_nt(a, b, precision=None):
    return _dot(a, b, (((1,), (1,)), ((), ())), precision)


def _split3(x):
    x1 = x.astype(BF16)
    r1 = x - x1.astype(F32)
    x2 = r1.astype(BF16)
    x3 = (r1 - x2.astype(F32)).astype(BF16)
    return x1, x2, x3


def _ada_kernel(c_ref, w_ref, b_ref, o_ref):
    c = c_ref[...]
    a = (c * jax.nn.sigmoid(c)).astype(BF16)
    o_ref[0] = _dot(a, w_ref[0].astype(BF16)) + b_ref[0]


def _ada_mod(cc, ada_w, ada_b):
    L, D, N = ada_w.shape
    R = cc.shape[0]
    tn = 1536
    return pl.pallas_call(
        _ada_kernel,
        grid=(L, N // tn),
        in_specs=[pl.BlockSpec((R, D), lambda l, j: (0, 0)),
                  pl.BlockSpec((1, D, tn), lambda l, j: (l, 0, j)),
                  pl.BlockSpec((1, 1, tn), lambda l, j: (l, 0, j))],
        out_specs=pl.BlockSpec((1, R, tn), lambda l, j: (l, 0, j)),
        out_shape=jax.ShapeDtypeStruct((L, R, N), F32),
        compiler_params=_cparams(("parallel", "parallel")),
        name="ada_mod",
    )(cc, ada_w, ada_b.reshape(L, 1, N))


def _modnorm_kernel(x_ref, nw_ref, sh_ref, sc_ref, h_ref, *, groups):
    for g in range(groups):
        rows = pl.ds(g * GRP, GRP)
        x = x_ref[rows, :]
        y = x * lax.rsqrt(jnp.mean(x * x, axis=-1, keepdims=True) + NORM_EPS) * nw_ref[...]
        h_ref[rows, :] = (y * (1.0 + sc_ref[g]) + sh_ref[g]).astype(BF16)


def _modnorm(x, nw, modg, sh_blk, sc_blk):
    M, D = x.shape
    tm = _tile(M, 512)
    groups = tm // GRP
    return pl.pallas_call(
        functools.partial(_modnorm_kernel, groups=groups),
        grid=(M // tm,),
        in_specs=[pl.BlockSpec((tm, D), lambda i: (i, 0)),
                  pl.BlockSpec((1, D), lambda i: (0, 0)),
                  pl.BlockSpec((groups, 1, D), lambda i: (i, 0, sh_blk)),
                  pl.BlockSpec((groups, 1, D), lambda i: (i, 0, sc_blk))],
        out_specs=pl.BlockSpec((tm, D), lambda i: (i, 0)),
        out_shape=jax.ShapeDtypeStruct((M, D), BF16),
        compiler_params=_cparams(("parallel",)),
        name="modnorm",
    )(x, nw.reshape(1, D), modg, modg)


def _mm_kernel(a_ref, b_ref, o_ref):
    o_ref[...] = _dot(a_ref[...], b_ref[...]).astype(o_ref.dtype)


def _matmul(a, b, tm, tn, out_dtype, name):
    M, K = a.shape
    N = b.shape[1]
    return pl.pallas_call(
        _mm_kernel,
        grid=(M // tm, N // tn),
        in_specs=[pl.BlockSpec((tm, K), lambda i, j: (i, 0)),
                  pl.BlockSpec((K, tn), lambda i, j: (0, j))],
        out_specs=pl.BlockSpec((tm, tn), lambda i, j: (i, j)),
        out_shape=jax.ShapeDtypeStruct((M, N), out_dtype),
        compiler_params=_cparams(("parallel", "parallel")),
        name=name,
    )(a, b)


def _mm_res_kernel(a_ref, b_ref, res_ref, gate_ref, o_ref, *, groups):
    acc = _dot(a_ref[...], b_ref[...])
    for g in range(groups):
        rows = slice(g * GRP, (g + 1) * GRP)
        o_ref[rows, :] = res_ref[rows, :] + gate_ref[g] * acc[rows, :]


def _matmul_res(a, b, res, modg, gate_blk, tm, tn, name):
    M, K = a.shape
    N = b.shape[1]
    groups = tm // GRP
    nb = N // tn
    return pl.pallas_call(
        functools.partial(_mm_res_kernel, groups=groups),
        grid=(M // tm, nb),
        in_specs=[pl.BlockSpec((tm, K), lambda i, j: (i, 0)),
                  pl.BlockSpec((K, tn), lambda i, j: (0, j)),
                  pl.BlockSpec((tm, tn), lambda i, j: (i, j)),
                  pl.BlockSpec((groups, 1, tn), lambda i, j: (i, 0, gate_blk * nb + j))],
        out_specs=pl.BlockSpec((tm, tn), lambda i, j: (i, j)),
        out_shape=jax.ShapeDtypeStruct((M, N), F32),
        compiler_params=_cparams(("parallel", "parallel")),
        name=name,
    )(a, b, res, modg)


def _ffn_up_kernel(h_ref, w1_ref, w3_ref, o_ref):
    h = h_ref[...]
    a = _dot(h, w1_ref[...])
    b = _dot(h, w3_ref[...])
    o_ref[...] = (a * jax.nn.sigmoid(a) * b).astype(o_ref.dtype)


def _ffn_up(h, w1, w3, tm, tn):
    M, K = h.shape
    N = w1.shape[1]
    return pl.pallas_call(
        _ffn_up_kernel,
        grid=(M // tm, N // tn),
        in_specs=[pl.BlockSpec((tm, K), lambda i, j: (i, 0)),
                  pl.BlockSpec((K, tn), lambda i, j: (0, j)),
                  pl.BlockSpec((K, tn), lambda i, j: (0, j))],
        out_specs=pl.BlockSpec((tm, tn), lambda i, j: (i, j)),
        out_shape=jax.ShapeDtypeStruct((M, N), BF16),
        compiler_params=_cparams(("parallel", "parallel")),
        name="ffn_up",
    )(h, w1, w3)


def _merge_kernel(ya_ref, yb_ref, yc_ref, wa_ref, wb_ref, wc_ref, ga_ref, gb_ref, gc_ref, o_ref):
    def sig(x):
        return 0.5 * jnp.tanh(0.5 * x) + 0.5

    m = sig(ga_ref[...]) * _dot(ya_ref[...], wa_ref[...])
    m = m + sig(gb_ref[...]) * _dot(yb_ref[...], wb_ref[...])
    m = m + sig(gc_ref[...]) * _dot(yc_ref[...], wc_ref[...])
    o_ref[...] = m.astype(o_ref.dtype)


def _merge(ya, yb, yc, wa, wb, wc, z, tm, tn=512):
    M = ya.shape[0]
    D = wa.shape[1]
    nb = D // tn
    g0 = ZG_OFF // tn
    return pl.pallas_call(
        _merge_kernel,
        grid=(M // tm, nb),
        in_specs=[pl.BlockSpec((tm, ya.shape[1]), lambda i, j: (i, 0)),
                  pl.BlockSpec((tm, yb.shape[1]), lambda i, j: (i, 0)),
                  pl.BlockSpec((tm, yc.shape[1]), lambda i, j: (i, 0)),
                  pl.BlockSpec((wa.shape[0], tn), lambda i, j: (0, j)),
                  pl.BlockSpec((wb.shape[0], tn), lambda i, j: (0, j)),
                  pl.BlockSpec((wc.shape[0], tn), lambda i, j: (0, j)),
                  pl.BlockSpec((tm, tn), lambda i, j: (i, g0 + j)),
                  pl.BlockSpec((tm, tn), lambda i, j: (i, g0 + nb + j)),
                  pl.BlockSpec((tm, tn), lambda i, j: (i, g0 + 2 * nb + j))],
        out_specs=pl.BlockSpec((tm, tn), lambda i, j: (i, j)),
        out_shape=jax.ShapeDtypeStruct((M, D), BF16),
        compiler_params=_cparams(("parallel", "parallel")),
        name="merge",
    )(ya, yb, yc, wa, wb, wc, z, z, z)


def _final_norm_kernel(x_ref, w_ref, o_ref):
    x = x_ref[0]
    o_ref[0] = x * lax.rsqrt(jnp.mean(x * x, axis=-1, keepdims=True) + NORM_EPS) * w_ref[...]


def _final_norm(xt, w, ctx_len):
    B, T, D = xt.shape
    S = T - ctx_len
    cg = ctx_len // GRP
    return pl.pallas_call(
        _final_norm_kernel,
        grid=(B, S // GRP),
        in_specs=[pl.BlockSpec((1, GRP, D), lambda b, i: (b, cg + i, 0)),
                  pl.BlockSpec((1, D), lambda b, i: (0, 0))],
        out_specs=pl.BlockSpec((1, GRP, D), lambda b, i: (b, i, 0)),
        out_shape=jax.ShapeDtypeStruct((B, S, D), F32),
        compiler_params=_cparams(("parallel", "parallel")),
        name="final_norm",
    )(xt, w.reshape(1, D))


def _gqa_prep_kernel(z_ref, cos_ref, sin_ref, gq_ref, gk_ref, q_ref, k_ref, v_ref):
    cosf = cos_ref[...]
    sinf = sin_ref[...]

    def norm_rope(x, gain):
        y = x * lax.rsqrt(jnp.mean(x * x, axis=-1, keepdims=True) + NORM_EPS) * gain
        return y * cosf + pltpu.roll(y, GQ_HEAD // 2, axis=1) * sinf

    scale = GQ_HEAD ** -0.5 * math.log2(math.e)
    for h in range(GQ_HEADS):
        cols = slice(h * GQ_HEAD, (h + 1) * GQ_HEAD)
        q_ref[:, cols] = (norm_rope(z_ref[:, cols], gq_ref[...]) * scale).astype(BF16)
    for h in range(GQ_KV_HEADS):
        cols = slice(h * GQ_HEAD, (h + 1) * GQ_HEAD)
        zc = slice(GQ_WIDTH + h * GQ_HEAD, GQ_WIDTH + (h + 1) * GQ_HEAD)
        k_ref[:, cols] = norm_rope(z_ref[:, zc], gk_ref[...]).astype(BF16)
        vz = slice(GQ_WIDTH + GQ_KV_WIDTH + h * GQ_HEAD, GQ_WIDTH + GQ_KV_WIDTH + (h + 1) * GQ_HEAD)
        v_ref[:, 2 * h * GQ_HEAD:(2 * h + 1) * GQ_HEAD] = z_ref[:, vz].astype(BF16)
        v_ref[:, (2 * h + 1) * GQ_HEAD:(2 * h + 2) * GQ_HEAD] = jnp.ones((GRP, GQ_HEAD), BF16)


def _gqa_prep(z, cosf, sinf, gq, gk, gps):
    M = z.shape[0]
    zw = GQ_WIDTH + 2 * GQ_KV_WIDTH
    return pl.pallas_call(
        _gqa_prep_kernel,
        grid=(M // GRP,),
        in_specs=[pl.BlockSpec((GRP, zw), lambda i: (i, ZQ_OFF // zw)),
                  pl.BlockSpec((GRP, GQ_HEAD), lambda i: (i % gps, 0)),
                  pl.BlockSpec((GRP, GQ_HEAD), lambda i: (i % gps, 0)),
                  pl.BlockSpec((1, GQ_HEAD), lambda i: (0, 0)),
                  pl.BlockSpec((1, GQ_HEAD), lambda i: (0, 0))],
        out_specs=[pl.BlockSpec((GRP, GQ_WIDTH), lambda i: (i, 0)),
                   pl.BlockSpec((GRP, GQ_KV_WIDTH), lambda i: (i, 0)),
                   pl.BlockSpec((GRP, 2 * GQ_KV_WIDTH), lambda i: (i, 0))],
        out_shape=[jax.ShapeDtypeStruct((M, GQ_WIDTH), BF16),
                   jax.ShapeDtypeStruct((M, GQ_KV_WIDTH), BF16),
                   jax.ShapeDtypeStruct((M, 2 * GQ_KV_WIDTH), BF16)],
        compiler_params=_cparams(("parallel",)),
        name="gqa_prep",
    )(z, cosf, sinf, gq.reshape(1, GQ_HEAD), gk.reshape(1, GQ_HEAD))


def _gqa_attn_kernel(q_ref, k_ref, v_ref, o_ref, *, ctx_tiles, ctx_len, n_keys):
    qi = pl.program_id(2)

    def attend(nk):
        k = k_ref[0, 0:nk, :]
        v = v_ref[0, 0:nk, :]
        for h in range(GQ_GROUP):
            cols = slice(h * GQ_HEAD, (h + 1) * GQ_HEAD)
            s = _dot_nt(q_ref[0, :, cols], k)
            p = jnp.exp2(s - jnp.max(s, axis=-1, keepdims=True))
            ol = _dot(p.astype(BF16), v)
            o = ol[:, :GQ_HEAD] / ol[:, GQ_HEAD:GQ_HEAD + 1]
            o_ref[0, :, cols] = o.astype(o_ref.dtype)

    @pl.when(qi < ctx_tiles)
    def _():
        attend(ctx_len)

    @pl.when(qi >= ctx_tiles)
    def _():
        attend(n_keys)


def _gqa_attn(q, k, v, ctx_len, tq=256):
    B, T, _ = q.shape
    gw = GQ_GROUP * GQ_HEAD
    kern = functools.partial(_gqa_attn_kernel, ctx_tiles=ctx_len // tq, ctx_len=ctx_len, n_keys=T)
    return pl.pallas_call(
        kern,
        grid=(B, GQ_KV_HEADS, T // tq),
        in_specs=[pl.BlockSpec((1, tq, gw), lambda b, g, i: (b, i, g)),
                  pl.BlockSpec((1, T, GQ_HEAD), lambda b, g, i: (b, 0, g)),
                  pl.BlockSpec((1, T, 2 * GQ_HEAD), lambda b, g, i: (b, 0, g))],
        out_specs=pl.BlockSpec((1, tq, gw), lambda b, g, i: (b, i, g)),
        out_shape=jax.ShapeDtypeStruct((B, T, GQ_WIDTH), BF16),
        compiler_params=_cparams(("parallel", "parallel", "arbitrary")),
        name="gqa_attn",
    )(q, k, v)


def _na_kernel(q_ref, k_ref, v_ref, bias_ref, o_ref, qs_ref, ks_ref, vs_ref, *, ctx_len, rows):
    scale = NA_HEAD ** -0.5
    wr = min(NA_WIN_ROWS, rows)
    nwin = wr * GRID_W
    rpg = math.gcd(rows, 8)

    for hh in range(2):
        cols = slice(hh * NA_HEAD, (hh + 1) * NA_HEAD)
        qs_ref[hh] = (q_ref[0, :, cols] * scale).astype(BF16)
        ks_ref[hh] = k_ref[0, :, cols].astype(BF16)
        vs_ref[hh] = v_ref[0, :, cols].astype(BF16)

    for hh in range(2):
        cols = slice(hh * NA_HEAD, (hh + 1) * NA_HEAD)
        kc = ks_ref[hh, 0:ctx_len, :]
        vc = vs_ref[hh, 0:ctx_len, :]

        s = _dot_nt(qs_ref[hh, 0:ctx_len, :], kc)
        p = jnp.exp(s - jnp.max(s, axis=-1, keepdims=True))
        o = _dot(p.astype(BF16), vc) / jnp.sum(p, axis=-1, keepdims=True)
        o_ref[0, 0:ctx_len, cols] = o.astype(o_ref.dtype)

        def group_body(g, carry):
            i0 = g * rpg
            qoff = pl.multiple_of(ctx_len + i0 * GRID_W, rpg * GRID_W)
            qg = qs_ref[hh, pl.ds(qoff, rpg * GRID_W), :]
            s2 = _dot_nt(qg, kc)
            koffs, s1 = [], []
            for r in range(rpg):
                rs = jnp.clip(i0 + r - wr // 2, 0, rows - wr)
                koffs.append(pl.multiple_of(ctx_len + rs * GRID_W, GRID_W))
                kw = ks_ref[hh, pl.ds(koffs[r], nwin), :]
                s1.append(_dot_nt(qg[r * GRID_W:(r + 1) * GRID_W], kw) + bias_ref[hh, i0 + r - rs])
            s1 = jnp.concatenate(s1, axis=0)
            m = jnp.maximum(jnp.max(s1, axis=-1, keepdims=True), jnp.max(s2, axis=-1, keepdims=True))
            p1 = jnp.exp(s1 - m)
            p2 = jnp.exp(s2 - m)
            l = jnp.sum(p1, axis=-1, keepdims=True) + jnp.sum(p2, axis=-1, keepdims=True)
            p1 = p1.astype(BF16)
            o1 = [_dot(p1[r * GRID_W:(r + 1) * GRID_W], vs_ref[hh, pl.ds(koffs[r], nwin), :]) for r in range(rpg)]
            o = (jnp.concatenate(o1, axis=0) + _dot(p2.astype(BF16), vc)) / l
            o_ref[0, pl.ds(qoff, rpg * GRID_W), cols] = o.astype(o_ref.dtype)
            return carry

        lax.fori_loop(0, rows // rpg, group_body, 0)


def _na_bias_table(rpb, rows):
    wr = min(NA_WIN_ROWS, rows)
    cols = np.arange(GRID_W)
    cstart = np.clip(cols - NA_WIN_COLS // 2, 0, GRID_W - NA_WIN_COLS)
    c = np.arange(GRID_W)
    inside = (c[None, :] >= cstart[:, None]) & (c[None, :] < cstart[:, None] + NA_WIN_COLS)
    col_off = np.clip(c[None, :] - cols[:, None] + (NA_WIN_COLS - 1), 0, 2 * NA_WIN_COLS - 2)
    onehot = (np.arange(2 * NA_WIN_COLS - 1)[:, None, None] == col_off[None]).astype(np.float32)
    tc = jnp.einsum('...ro,ojc->...rjc', rpb, onehot, precision=lax.Precision.HIGHEST)
    tc = jnp.where(inside, tc, NEG_BIG)
    t = jnp.stack([tc[..., NA_WIN_ROWS - 1 - d:NA_WIN_ROWS - 1 - d + wr, :, :] for d in range(wr)], axis=-4)
    t = jnp.swapaxes(t, -3, -2)
    return t.reshape(rpb.shape[:-2] + (wr, GRID_W, wr * GRID_W))


def _na_attn(z, bias, ctx_len):
    B, T, _ = z.shape
    rows = (T - ctx_len) // GRID_W
    wr = bias.shape[1]
    pw = 2 * NA_HEAD
    c0 = ZN_OFF // pw
    per = NA_WIDTH // pw
    scr = pltpu.VMEM((2, T, NA_HEAD), BF16)
    return pl.pallas_call(
        functools.partial(_na_kernel, ctx_len=ctx_len, rows=rows),
        grid=(B, per),
        in_specs=[pl.BlockSpec((1, T, pw), lambda b, h: (b, 0, c0 + h)),
                  pl.BlockSpec((1, T, pw), lambda b, h: (b, 0, c0 + per + h)),
                  pl.BlockSpec((1, T, pw), lambda b, h: (b, 0, c0 + 2 * per + h)),
                  pl.BlockSpec((2, wr, GRID_W, wr * GRID_W), lambda b, h: (h, 0, 0, 0))],
        out_specs=pl.BlockSpec((1, T, pw), lambda b, h: (b, 0, h)),
        out_shape=jax.ShapeDtypeStruct((B, T, NA_WIDTH), BF16),
        scratch_shapes=[scr, scr, scr],
        compiler_params=_cparams(("parallel", "parallel")),
        name="na_attn",
    )(z, z, z, bias)


def _rw_prep_kernel(z_ref, zp_ref, zn_ref, l_ref, lp_ref, ln_ref,
                    mup_ref, mun_ref, lmup_ref, lmun_ref, kk_ref_, ka_ref, w0_ref, wup_ref, a0_ref, aup_ref,
                    gup_ref,
                    r_out, v_out, kk_out, g_out, cs_out, lw_out, kd_out, bd_out, *, gps, ctx_groups):
    t = pl.program_id(1)
    first = jnp.logical_or(t == 0, t == ctx_groups)
    last = jnp.logical_or(t == ctx_groups - 1, t == gps - 1)
    pv = jnp.where(first, 0.0, 1.0)
    nv = jnp.where(last, 0.0, 1.0)
    row = lax.broadcasted_iota(jnp.int32, (GRP, 1), 0)

    def shift(x_ref, p_ref, n_ref, mp_ref, mn_ref):
        x = x_ref[...]
        xp = jnp.where(row == 0, p_ref[7:8, :] * pv, pltpu.roll(x, 1, axis=0))
        xn = jnp.where(row == GRP - 1, n_ref[0:1, :] * nv, pltpu.roll(x, GRP - 1, axis=0))
        return x + mp_ref[...] * (xp - x) + mn_ref[...] * (xn - x)

    zs = shift(z_ref, zp_ref, zn_ref, mup_ref, mun_ref)
    ls = shift(l_ref, lp_ref, ln_ref, lmup_ref, lmun_ref)
    r = zs[:, 0:RW_WIDTH]
    k = zs[:, RW_WIDTH:2 * RW_WIDTH]
    v = zs[:, 2 * RW_WIDTH:3 * RW_WIDTH]

    hi = lax.Precision.HIGHEST
    ci = lax.broadcasted_iota(jnp.int32, (RW_WIDTH, RW_WIDTH), 0) // RW_HEAD
    cj = lax.broadcasted_iota(jnp.int32, (RW_WIDTH, RW_WIDTH), 1) // RW_HEAD
    head_ones = jnp.where(ci == cj, 1.0, 0.0).astype(BF16)
    kk = k * kk_ref_[...]
    ss = sum(_dot(part, head_ones) for part in _split3(kk * kk))
    kk = kk / jnp.maximum(jnp.sqrt(ss), 1e-12)

    ti = lax.broadcasted_iota(jnp.int32, (GRP, GRP), 0)
    tj = lax.broadcasted_iota(jnp.int32, (GRP, GRP), 1)
    same = (ti // SCAN_CHUNK) == (tj // SCAN_CHUNK)

    def heads_out(ref, val, lead):
        for h in range(RW_HEADS):
            ref[lead + (h,)] = val[:, h * RW_HEAD:(h + 1) * RW_HEAD]

    heads_out(r_out, r, (0,))
    heads_out(v_out, v, (0,))
    heads_out(kk_out, kk, (0,))
    gd = ls[:, 4 * LORA_PAD:]
    g_out[...] = _dot(jax.nn.sigmoid(gd).astype(BF16), gup_ref[...].astype(BF16))

    for d in range(2):
        wl = ls[:, d * LORA_PAD:(d + 1) * LORA_PAD]
        al = ls[:, (2 + d) * LORA_PAD:(3 + d) * LORA_PAD]
        w_raw = w0_ref[d] + _dot(jnp.tanh(wl), wup_ref[d], precision=hi)
        lw = -math.exp(-0.5) * jax.nn.sigmoid(w_raw)
        a = jax.nn.sigmoid(a0_ref[d] + _dot(al.astype(BF16), aup_ref[d].astype(BF16)))
        kd = k * (1.0 + (a - 1.0) * ka_ref[...])
        bd = kk * a
        tri = jnp.where(jnp.logical_and(same, (ti >= tj) if d == 0 else (ti <= tj)), 1.0, 0.0).astype(BF16)
        cs = sum(_dot(tri, part) for part in _split3(lw))
        heads_out(cs_out, cs, (d, 0))
        heads_out(lw_out, lw, (d, 0))
        heads_out(kd_out, kd, (d, 0))
        heads_out(bd_out, bd, (d, 0))


def _rw_prep(z, p, B, gps, ctx_groups):
    M = z.shape[0]
    T = gps * GRP
    nb8 = M // 8
    rb = GRP // 8
    zc = ZRKV_OFF // 1536
    lc = ZL_OFF // ZL_W

    def cur(c):
        return lambda b, t: (b * gps + t, c)

    def prev(c):
        return lambda b, t: (jnp.maximum((b * gps + t) * rb - 1, 0), c)

    def nxt(c):
        return lambda b, t: (jnp.minimum((b * gps + t + 1) * rb, nb8 - 1), c)

    def const(shape):
        nd = len(shape)
        return pl.BlockSpec(shape, lambda b, t: (0,) * nd)

    hm = pl.BlockSpec((1, RW_HEADS, GRP, RW_HEAD), lambda b, t: (b, 0, t, 0))
    hm2 = pl.BlockSpec((2, 1, RW_HEADS, GRP, RW_HEAD), lambda b, t: (0, b, 0, t, 0))
    hshape = jax.ShapeDtypeStruct((B, RW_HEADS, T, RW_HEAD), F32)
    hshape2 = jax.ShapeDtypeStruct((2, B, RW_HEADS, T, RW_HEAD), F32)
    return pl.pallas_call(
        functools.partial(_rw_prep_kernel, gps=gps, ctx_groups=ctx_groups),
        grid=(B, gps),
        in_specs=[pl.BlockSpec((GRP, 1536), cur(zc)),
                  pl.BlockSpec((8, 1536), prev(zc)),
                  pl.BlockSpec((8, 1536), nxt(zc)),
                  pl.BlockSpec((GRP, ZL_W), cur(lc)),
                  pl.BlockSpec((8, ZL_W), prev(lc)),
                  pl.BlockSpec((8, ZL_W), nxt(lc)),
                  const((1, 1536)), const((1, 1536)), const((1, ZL_W)), const((1, ZL_W)),
                  const((1, RW_WIDTH)), const((1, RW_WIDTH)),
                  const((2, 1, RW_WIDTH)), const((2, LORA_PAD, RW_WIDTH)),
                  const((2, 1, RW_WIDTH)), const((2, LORA_PAD, RW_WIDTH)),
                  const((256, RW_WIDTH))],
        out_specs=[hm, hm, hm, pl.BlockSpec((GRP, RW_WIDTH), lambda b, t: (b * gps + t, 0)),
                   hm2, hm2, hm2, hm2],
        out_shape=[hshape, hshape, hshape, jax.ShapeDtypeStruct((M, RW_WIDTH), F32),
                   hshape2, hshape2, hshape2, hshape2],
        compiler_params=_cparams(("parallel", "parallel")),
        name="rw_prep",
    )(z, z, z, z, z, z, p['mu_p'], p['mu_n'], p['lmu_p'], p['lmu_n'], p['k_k'], p['k_a'],
      p['w0'], p['w_up'], p['a0'], p['a_up'], p['g_up'])


def _bmm(a, b, nt=False):
    a = a.astype(BF16)
    b = b.astype(BF16)
    out = []
    for h in range(a.shape[0]):
        out.append(_dot_nt(a[h], b[h]) if nt else _dot(a[h], b[h]))
    return jnp.stack(out, axis=0)


def _bmm_tn(a, b):
    a = a.astype(BF16)
    b = b.astype(BF16)
    out = []
    for h in range(a.shape[0]):
        out.append(_dot(a[h], b[h], (((0,), (0,)), ((), ()))))
    return jnp.stack(out, axis=0)


def _rw_intra_kernel(r_ref, v_ref, kk_ref, cs_ref, lw_ref, kd_ref, bd_ref,
                     rh_ref, o0_ref, gm_ref, sd_ref, pc_ref, *, nck):
    d = pl.program_id(1)
    C, H, N = SCAN_CHUNK, RW_HEADS, RW_HEAD
    G = H * nck

    fwd = d == 0
    ti = lax.broadcasted_iota(jnp.int32, (C, C), 0)
    tj = lax.broadcasted_iota(jnp.int32, (C, C), 1)
    before = (ti - tj) * jnp.where(fwd, 1, -1)
    strict = (before > 0)[None]
    incl = (before >= 0)[None]
    eye = (ti == tj)[None]

    def chunks(x):
        return x.reshape(G, C, N)

    r = chunks(r_ref[0])
    v = chunks(v_ref[0])
    cs = chunks(cs_ref[0, 0])
    pinv = jnp.exp(-cs)
    ra = r * jnp.exp(cs)
    at = -chunks(kk_ref[0]) * jnp.exp(cs - chunks(lw_ref[0, 0]))
    bt = chunks(bd_ref[0, 0]) * pinv
    kt = chunks(kd_ref[0, 0]) * pinv
    pc = jnp.exp(jnp.where(fwd, cs[:, C - 1:C, :], cs[:, 0:1, :]))

    lhs = jnp.concatenate([at, ra], axis=1)
    rhs = jnp.concatenate([bt, kt], axis=1)
    gmat = _bmm(lhs, rhs, nt=True)
    a_ab = jnp.where(strict, gmat[:, :C, :C], 0.0)
    a_ak = jnp.where(strict, gmat[:, :C, C:], 0.0)
    a_rb = jnp.where(incl, gmat[:, C:, :C], 0.0)
    a_rk = jnp.where(incl, gmat[:, C:, C:], 0.0)

    tm = jnp.where(eye, 1.0, 0.0) + a_ab
    pw = a_ab
    for _ in range(int(math.log2(C)) - 1):
        pw = _bmm(pw, pw)
        tm = tm + _bmm(tm, pw)

    au = _bmm(tm, jnp.concatenate([at, _bmm(a_ak, v)], axis=2))
    ro = _bmm(a_rb, au)
    gs = _bmm_tn(au, bt * pc)

    def unchunk(x):
        return x.reshape(H, nck * C, N)

    rh_ref[0, 0] = unchunk(ra + ro[:, :, :N])
    o0_ref[0, 0] = unchunk(ro[:, :, N:] + _bmm(a_rk, v))
    gm_ref[0, 0] = unchunk(gs[:, :N])
    sd_ref[0, 0] = unchunk(gs[:, N:] + _bmm_tn(v, kt * pc))
    pc_ref[0, 0, 0] = pc.reshape(H, nck, N)


def _rw_seq_kernel(rhf_ref, o0f_ref, gmf_ref, sdf_ref, pcf_ref, rhb_ref, o0b_ref, gmb_ref, sdb_ref, pcb_ref,
                   of_ref, ob_ref, s_ref, *, nck):
    C = SCAN_CHUNK

    @pl.when(pl.program_id(1) == 0)
    def _():
        s_ref[...] = jnp.zeros(s_ref.shape, F32)

    dirs = ((rhf_ref, o0f_ref, gmf_ref, sdf_ref, pcf_ref, of_ref, range(nck)),
            (rhb_ref, o0b_ref, gmb_ref, sdb_ref, pcb_ref, ob_ref, range(nck - 1, -1, -1)))
    nb = s_ref.shape[1]
    chains = [(d, b) for d in range(2) for b in range(nb)]
    state = {ch: s_ref[ch[0], ch[1]] for ch in chains}
    for step in range(nck):
        for d, b in chains:
            rh_ref, o0_ref, gm_ref, sd_ref, pc_ref, o_ref, order = dirs[d]
            c = order[step]
            rows = slice(c * C, (c + 1) * C)
            s = state[(d, b)]
            o_ref[b, :, rows, :] = _bmm(rh_ref[0, b, :, rows, :], s, nt=True) + o0_ref[0, b, :, rows, :]
            state[(d, b)] = (s * pc_ref[0, b, 0, :, c:c + 1, :] + _bmm(s, gm_ref[0, b, :, rows, :])
                             + sd_ref[0, b, :, rows, :])
    for d, b in chains:
        s_ref[d, b] = state[(d, b)]


def _rw_scan(r, v, kk, cs, lw, kd, bd, ctx_len, nck=4):
    B, H, T, N = r.shape
    ct = nck * SCAN_CHUNK
    nblk = T // ct
    ncb = ctx_len // ct
    assert ctx_len % ct == 0 and T % ct == 0

    shared = pl.BlockSpec((1, H, ct, N), lambda b, d, j: (b, 0, j, 0))
    per_dir = pl.BlockSpec((1, 1, H, ct, N), lambda b, d, j: (d, b, 0, j, 0))
    pc_spec = pl.BlockSpec((1, 1, 1, H, nck, N), lambda b, d, j: (d, b, j, 0, 0, 0))
    big = jax.ShapeDtypeStruct((2, B, H, T, N), F32)
    rh, o0, gm, sd, pc = pl.pallas_call(
        functools.partial(_rw_intra_kernel, nck=nck),
        grid=(B, 2, nblk),
        in_specs=[shared, shared, shared, per_dir, per_dir, per_dir, per_dir],
        out_specs=[per_dir, per_dir, per_dir, per_dir, pc_spec],
        out_shape=[big, big, big, big, jax.ShapeDtypeStruct((2, B, nblk, H, nck, N), F32)],
        compiler_params=_cparams(("parallel", "parallel", "parallel")),
        name="rw_intra",
    )(r, v, kk, cs, lw, kd, bd)

    def bwd_blk(j):
        return jnp.where(j < ncb, ncb - 1 - j, nblk - 1 - (j - ncb))

    bb = 2 if B % 2 == 0 else 1
    f_big = pl.BlockSpec((1, bb, H, ct, N), lambda b, j: (0, b, 0, j, 0))
    b_big = pl.BlockSpec((1, bb, H, ct, N), lambda b, j: (1, b, 0, bwd_blk(j), 0))
    f_pc = pl.BlockSpec((1, bb, 1, H, nck, N), lambda b, j: (0, b, j, 0, 0, 0))
    b_pc = pl.BlockSpec((1, bb, 1, H, nck, N), lambda b, j: (1, b, bwd_blk(j), 0, 0, 0))
    out = jax.ShapeDtypeStruct((B, H, T, N), F32)
    return pl.pallas_call(
        functools.partial(_rw_seq_kernel, nck=nck),
        grid=(B // bb, nblk),
        in_specs=[f_big, f_big, f_big, f_big, f_pc, b_big, b_big, b_big, b_big, b_pc],
        out_specs=[pl.BlockSpec((bb, H, ct, N), lambda b, j: (b, 0, j, 0)),
                   pl.BlockSpec((bb, H, ct, N), lambda b, j: (b, 0, bwd_blk(j), 0))],
        out_shape=[out, out],
        scratch_shapes=[pltpu.VMEM((2, bb, H, N, N), F32)],
        compiler_params=_cparams(("parallel", "arbitrary")),
        name="rw_seq",
    )(rh, o0, gm, sd, pc, rh, o0, gm, sd, pc)


def _rw_out_kernel(of_ref, ob_ref, r_ref, v_ref, kd_ref, g_ref, rk_ref, lnw_ref, lnb_ref, y_ref):
    ys = []
    for h in range(RW_HEADS):
        wkv = of_ref[0, h] + ob_ref[0, h]
        mu = jnp.mean(wkv, axis=-1, keepdims=True)
        cen = wkv - mu
        var = jnp.mean(cen * cen, axis=-1, keepdims=True)
        y = cen * lax.rsqrt(var + RW_GN_EPS) * lnw_ref[h] + lnb_ref[h]
        r = r_ref[0, h]
        rk = rk_ref[h]
        bonus = (jnp.sum(r * kd_ref[0, 0, h] * rk, axis=-1, keepdims=True)
                 + jnp.sum(r * kd_ref[1, 0, h] * rk, axis=-1, keepdims=True)) * v_ref[0, h]
        ys.append(y + bonus)
    y_ref[...] = (jnp.concatenate(ys, axis=-1) * g_ref[...]).astype(y_ref.dtype)


def _rw_out(o_f, o_b, r, v, kd, g, rk, lnw, lnb, gps):
    B, H, T, N = r.shape
    M = B * T
    hm = pl.BlockSpec((1, H, GRP, N), lambda b, t: (b, 0, t, 0))
    hm2 = pl.BlockSpec((2, 1, H, GRP, N), lambda b, t: (0, b, 0, t, 0))
    par = pl.BlockSpec((H, 1, N), lambda b, t: (0, 0, 0))
    return pl.pallas_call(
        _rw_out_kernel,
        grid=(B, gps),
        in_specs=[hm, hm, hm, hm, hm2, pl.BlockSpec((GRP, RW_WIDTH), lambda b, t: (b * gps + t, 0)),
                  par, par, par],
        out_specs=pl.BlockSpec((GRP, RW_WIDTH), lambda b, t: (b * gps + t, 0)),
        out_shape=jax.ShapeDtypeStruct((M, RW_WIDTH), BF16),
        compiler_params=_cparams(("parallel", "parallel")),
        name="rw_out",
    )(o_f, o_b, r, v, kd, g, rk.reshape(H, 1, N), lnw.reshape(H, 1, N), lnb.reshape(H, 1, N))


def _pad_rows(w, n):
    return jnp.pad(w, [(0, 0)] * (w.ndim - 2) + [(0, n - w.shape[-2]), (0, 0)])


def _pad_cols(w, n):
    return jnp.pad(w, [(0, 0)] * (w.ndim - 1) + [(0, n - w.shape[-1])])


def _deinterleave(w):
    n = w.shape[-1] // GQ_HEAD
    w = w.reshape(w.shape[:-1] + (n, GQ_HEAD // 2, 2))
    return jnp.swapaxes(w, -1, -2).reshape(w.shape[:-3] + (n * GQ_HEAD,))


def _in_layout(w, lora):
    rkv = w[..., 0:1536]
    lo = [w[..., 1536 + i * lora: 1536 + (i + 1) * lora] for i in range(4)]
    gd = w[..., 1536 + 4 * lora: 1536 + 4 * lora + 256]
    o = 1536 + 4 * lora + 256
    zq = w[..., o:o + 1536]
    zn = w[..., o + 1536:o + 3072]
    zg = w[..., o + 3072:]
    zq = jnp.concatenate([_deinterleave(zq[..., :GQ_WIDTH + GQ_KV_WIDTH]), zq[..., GQ_WIDTH + GQ_KV_WIDTH:]], axis=-1)
    return jnp.concatenate([zg, rkv, zq, zn] + [_pad_cols(x, LORA_PAD) for x in lo] + [gd], axis=-1)


def _shift_layout(mu, lora):
    lo = [_pad_cols(mu[..., 1536 + i * lora: 1536 + (i + 1) * lora], LORA_PAD) for i in range(4)]
    return mu[..., 0:1536], jnp.concatenate(lo + [mu[..., 1536 + 4 * lora:]], axis=-1)


def _tile(n, pref):
    t = pref
    while n % t:
        t //= 2
    return t


def _axial_tables(ctx_len, seq):
    n_freq = GQ_HEAD // 4
    inv = ROPE_THETA ** (-jnp.arange(n_freq, dtype=F32) / n_freq)
    t = jnp.arange(seq, dtype=jnp.int32)
    row = (t // GRID_W).astype(F32)
    col = (t % GRID_W).astype(F32)
    ang = jnp.concatenate([row[:, None] * inv, col[:, None] * inv], axis=-1)
    cos, sin = jnp.cos(ang), jnp.sin(ang)
    cosf = jnp.concatenate([jnp.ones((ctx_len, GQ_HEAD), F32), jnp.concatenate([cos, cos], axis=-1)], axis=0)
    sinf = jnp.concatenate([jnp.zeros((ctx_len, GQ_HEAD), F32), jnp.concatenate([-sin, sin], axis=-1)], axis=0)
    return cosf, sinf


def kernel(x, c, ctx, c_ctx, ada_w, ada_b, norm1, norm2, w_in, rw_mu_prev, rw_mu_next, rw_w0, rw_w_up, rw_a0, rw_a_up, rw_g_up, rw_k_k, rw_k_a, rw_r_k, rw_ln_w, rw_ln_b, gq_q_norm, gq_k_norm, na_rpb, w_br_a, w_br_b, w_br_c, w_out, ffn_w1, ffn_w3, ffn_w2, final_norm):
    B, S, D = x.shape
    C = ctx.shape[1]
    L = ada_w.shape[0]
    T = C + S
    M = B * T
    gps = T // GRP
    cgr = C // GRP
    lora = rw_w_up.shape[2]
    assert C % GRP == 0 and S % GRP == 0 and S % GRID_W == 0
    assert w_in.shape[2] == 1536 + 4 * lora + 256 + 1536 + 1536 + 3 * D
    tm = _tile(M, 1024)
    tm_in = next(t for t in (2176, 1024, 512, 256) if M % t == 0)

    w_in_p = _in_layout(w_in, lora).astype(BF16)
    mu_p, lmu_p = _shift_layout(rw_mu_prev, lora)
    mu_n, lmu_n = _shift_layout(rw_mu_next, lora)
    w_up = _pad_rows(rw_w_up, LORA_PAD)
    a_up = _pad_rows(rw_a_up, LORA_PAD)
    gq = _deinterleave(gq_q_norm)
    gk = _deinterleave(gq_k_norm)
    wa = w_br_a.astype(BF16)
    wb = w_br_b.astype(BF16)
    wc = w_br_c.astype(BF16)
    wo = w_out.astype(BF16)
    w1 = ffn_w1.astype(BF16)
    w3 = ffn_w3.astype(BF16)
    w2 = ffn_w2.astype(BF16)
    cosf, sinf = _axial_tables(C, S)
    na_bias = _na_bias_table(na_rpb, S // GRID_W)

    cc = jnp.concatenate([c, c_ctx[None, :], jnp.zeros((-(B + 1) % 8, D), F32)], axis=0)
    mod = _ada_mod(cc, ada_w, ada_b)
    modg = jnp.concatenate([jnp.broadcast_to(mod[:, B, None, None, :], (L, B, cgr, 6 * D)),
                            jnp.broadcast_to(mod[:, :B, None, :], (L, B, gps - cgr, 6 * D))], axis=2)
    modg = modg.reshape(L, B * gps, 1, 6 * D)

    xt = jnp.concatenate([ctx, x], axis=1).reshape(M, D)

    for l in range(L):
        mg = modg[l]
        h = _modnorm(xt, norm1[l], mg, 0, 1)
        z = _matmul(h, w_in_p[l], tm_in, 768, F32, "in_proj")

        pr = {'mu_p': mu_p[l][None], 'mu_n': mu_n[l][None], 'lmu_p': lmu_p[l][None], 'lmu_n': lmu_n[l][None],
              'k_k': rw_k_k[l][None], 'k_a': rw_k_a[l][None],
              'w0': rw_w0[l][:, None, :], 'w_up': w_up[l], 'a0': rw_a0[l][:, None, :], 'a_up': a_up[l],
              'g_up': rw_g_up[l]}
        r_, v_, kk_, g_, cs_, lw_, kd_, bd_ = _rw_prep(z, pr, B, gps, cgr)
        of_, ob_ = _rw_scan(r_, v_, kk_, cs_, lw_, kd_, bd_, C)
        ya = _rw_out(of_, ob_, r_, v_, kd_, g_, rw_r_k[l], rw_ln_w[l], rw_ln_b[l], gps)

        q, k, v = _gqa_prep(z, cosf, sinf, gq[l], gk[l], gps)
        yb = _gqa_attn(q.reshape(B, T, GQ_WIDTH), k.reshape(B, T, GQ_KV_WIDTH), v.reshape(B, T, 2 * GQ_KV_WIDTH), C)
        yb = yb.reshape(M, GQ_WIDTH)

        yc = _na_attn(z.reshape(B, T, Z_W), na_bias[l], C).reshape(M, NA_WIDTH)

        m = _merge(ya, yb, yc, wa[l], wb[l], wc[l], z, tm)
        xt = _matmul_res(m, wo[l], xt, mg, 2, tm, 512, "out_proj")
        h = _modnorm(xt, norm2[l], mg, 3, 4)
        u = _ffn_up(h, w1[l], w3[l], tm, 512)
        xt = _matmul_res(u, w2[l], xt, mg, 5, tm, 512, "ffn_down")

    return _final_norm(xt.reshape(B, T, D), final_norm, C)
```

```python
import functools
import math

import numpy as np
import jax
import jax.numpy as jnp
from jax import lax
from jax.experimental import pallas as pl
from jax.experimental.pallas import tpu as pltpu

F32 = jnp.float32
BF16 = jnp.bfloat16

GRP = 256
GRID_W = 64
NORM_EPS = 1e-6

RW_HEAD = 64
RW_HEADS = 8
RW_WIDTH = RW_HEAD * RW_HEADS
LORA_PAD = 128
RW_GN_EPS = 64e-5
SCAN_CHUNK = 64

GQ_HEAD = 128
GQ_HEADS = 8
GQ_KV_HEADS = 2
GQ_GROUP = GQ_HEADS // GQ_KV_HEADS
GQ_WIDTH = GQ_HEAD * GQ_HEADS
GQ_KV_WIDTH = GQ_HEAD * GQ_KV_HEADS
ROPE_THETA = 10000.0

NA_HEAD = 64
NA_HEADS = 8
NA_WIDTH = NA_HEAD * NA_HEADS
NA_WIN_ROWS = 8
NA_WIN_COLS = 16
NEG_BIG = -1e30

VMEM_LIMIT = 56 * 1024 * 1024

ZG_OFF, ZG_W = 0, 6144
ZRKV_OFF = 6144
ZQ_OFF = ZRKV_OFF + 1536
ZN_OFF = ZQ_OFF + 1536
ZL_OFF = ZN_OFF + 1536
ZL_W = 4 * LORA_PAD + 256
Z_W = ZL_OFF + ZL_W


def _cparams(sem):
    return pltpu.CompilerParams(dimension_semantics=sem, vmem_limit_bytes=VMEM_LIMIT)


def _dot(a, b, dims=None, precision=None):
    if dims is None:
        dims = (((a.ndim - 1,), (0,)), ((), ()))
    return lax.dot_general(a, b, dims, precision=precision, preferred_element_type=F32)


def _dot_nt(a, b, precision=None):
    return _dot(a, b, (((1,), (1,)), ((), ())), precision)


def _split3(x):
    x1 = x.astype(BF16)
    r1 = x - x1.astype(F32)
    x2 = r1.astype(BF16)
    x3 = (r1 - x2.astype(F32)).astype(BF16)
    return x1, x2, x3


def _ada_kernel(c_ref, w_ref, b_ref, o_ref):
    c = c_ref[...]
    a = (c * jax.nn.sigmoid(c)).astype(BF16)
    o_ref[0] = _dot(a, w_ref[0].astype(BF16)) + b_ref[0]


def _ada_mod(cc, ada_w, ada_b):
    L, D, N = ada_w.shape
    R = cc.shape[0]
    tn = 1536
    return pl.pallas_call(
        _ada_kernel,
        grid=(L, N // tn),
        in_specs=[pl.BlockSpec((R, D), lambda l, j: (0, 0)),
                  pl.BlockSpec((1, D, tn), lambda l, j: (l, 0, j)),
                  pl.BlockSpec((1, 1, tn), lambda l, j: (l, 0, j))],
        out_specs=pl.BlockSpec((1, R, tn), lambda l, j: (l, 0, j)),
        out_shape=jax.ShapeDtypeStruct((L, R, N), F32),
        compiler_params=_cparams(("parallel", "parallel")),
        name="ada_mod",
    )(cc, ada_w, ada_b.reshape(L, 1, N))


def _modnorm_kernel(x_ref, nw_ref, sh_ref, sc_ref, h_ref, *, groups):
    for g in range(groups):
        rows = pl.ds(g * GRP, GRP)
        x = x_ref[rows, :]
        y = x * lax.rsqrt(jnp.mean(x * x, axis=-1, keepdims=True) + NORM_EPS) * nw_ref[...]
        h_ref[rows, :] = (y * (1.0 + sc_ref[g]) + sh_ref[g]).astype(BF16)


def _modnorm(x, nw, modg, sh_blk, sc_blk):
    M, D = x.shape
    tm = _tile(M, 512)
    groups = tm // GRP
    return pl.pallas_call(
        functools.partial(_modnorm_kernel, groups=groups),
        grid=(M // tm,),
        in_specs=[pl.BlockSpec((tm, D), lambda i: (i, 0)),
                  pl.BlockSpec((1, D), lambda i: (0, 0)),
                  pl.BlockSpec((groups, 1, D), lambda i: (i, 0, sh_blk)),
                  pl.BlockSpec((groups, 1, D), lambda i: (i, 0, sc_blk))],
        out_specs=pl.BlockSpec((tm, D), lambda i: (i, 0)),
        out_shape=jax.ShapeDtypeStruct((M, D), BF16),
        compiler_params=_cparams(("parallel",)),
        name="modnorm",
    )(x, nw.reshape(1, D), modg, modg)


def _mm_kernel(a_ref, b_ref, o_ref):
    o_ref[...] = _dot(a_ref[...], b_ref[...]).astype(o_ref.dtype)


def _matmul(a, b, tm, tn, out_dtype, name):
    M, K = a.shape
    N = b.shape[1]
    return pl.pallas_call(
        _mm_kernel,
        grid=(M // tm, N // tn),
        in_specs=[pl.BlockSpec((tm, K), lambda i, j: (i, 0)),
                  pl.BlockSpec((K, tn), lambda i, j: (0, j))],
        out_specs=pl.BlockSpec((tm, tn), lambda i, j: (i, j)),
        out_shape=jax.ShapeDtypeStruct((M, N), out_dtype),
        compiler_params=_cparams(("parallel", "parallel")),
        name=name,
    )(a, b)


def _mm_res_kernel(a_ref, b_ref, res_ref, gate_ref, o_ref, *, groups):
    acc = _dot(a_ref[...], b_ref[...])
    for g in range(groups):
        rows = slice(g * GRP, (g + 1) * GRP)
        o_ref[rows, :] = res_ref[rows, :] + gate_ref[g] * acc[rows, :]


def _matmul_res(a, b, res, modg, gate_blk, tm, tn, name):
    M, K = a.shape
    N = b.shape[1]
    groups = tm // GRP
    nb = N // tn
    return pl.pallas_call(
        functools.partial(_mm_res_kernel, groups=groups),
        grid=(M // tm, nb),
        in_specs=[pl.BlockSpec((tm, K), lambda i, j: (i, 0)),
                  pl.BlockSpec((K, tn), lambda i, j: (0, j)),
                  pl.BlockSpec((tm, tn), lambda i, j: (i, j)),
                  pl.BlockSpec((groups, 1, tn), lambda i, j: (i, 0, gate_blk * nb + j))],
        out_specs=pl.BlockSpec((tm, tn), lambda i, j: (i, j)),
        out_shape=jax.ShapeDtypeStruct((M, N), F32),
        compiler_params=_cparams(("parallel", "parallel")),
        name=name,
    )(a, b, res, modg)


def _ffn_up_kernel(h_ref, w1_ref, w3_ref, o_ref):
    h = h_ref[...]
    a = _dot(h, w1_ref[...])
    b = _dot(h, w3_ref[...])
    o_ref[...] = (a * jax.nn.sigmoid(a) * b).astype(o_ref.dtype)


def _ffn_up(h, w1, w3, tm, tn):
    M, K = h.shape
    N = w1.shape[1]
    return pl.pallas_call(
        _ffn_up_kernel,
        grid=(M // tm, N // tn),
        in_specs=[pl.BlockSpec((tm, K), lambda i, j: (i, 0)),
                  pl.BlockSpec((K, tn), lambda i, j: (0, j)),
                  pl.BlockSpec((K, tn), lambda i, j: (0, j))],
        out_specs=pl.BlockSpec((tm, tn), lambda i, j: (i, j)),
        out_shape=jax.ShapeDtypeStruct((M, N), BF16),
        compiler_params=_cparams(("parallel", "parallel")),
        name="ffn_up",
    )(h, w1, w3)


def _merge_kernel(ya_ref, yb_ref, yc_ref, wa_ref, wb_ref, wc_ref, ga_ref, gb_ref, gc_ref, o_ref):
    def sig(x):
        return 0.5 * jnp.tanh(0.5 * x) + 0.5

    m = sig(ga_ref[...]) * _dot(ya_ref[...], wa_ref[...])
    m = m + sig(gb_ref[...]) * _dot(yb_ref[...], wb_ref[...])
    m = m + sig(gc_ref[...]) * _dot(yc_ref[...], wc_ref[...])
    o_ref[...] = m.astype(o_ref.dtype)


def _merge(ya, yb, yc, wa, wb, wc, z, tm, tn=512):
    M = ya.shape[0]
    D = wa.shape[1]
    nb = D // tn
    g0 = ZG_OFF // tn
    return pl.pallas_call(
        _merge_kernel,
        grid=(M // tm, nb),
        in_specs=[pl.BlockSpec((tm, ya.shape[1]), lambda i, j: (i, 0)),
                  pl.BlockSpec((tm, yb.shape[1]), lambda i, j: (i, 0)),
                  pl.BlockSpec((tm, yc.shape[1]), lambda i, j: (i, 0)),
                  pl.BlockSpec((wa.shape[0], tn), lambda i, j: (0, j)),
                  pl.BlockSpec((wb.shape[0], tn), lambda i, j: (0, j)),
                  pl.BlockSpec((wc.shape[0], tn), lambda i, j: (0, j)),
                  pl.BlockSpec((tm, tn), lambda i, j: (i, g0 + j)),
                  pl.BlockSpec((tm, tn), lambda i, j: (i, g0 + nb + j)),
                  pl.BlockSpec((tm, tn), lambda i, j: (i, g0 + 2 * nb + j))],
        out_specs=pl.BlockSpec((tm, tn), lambda i, j: (i, j)),
        out_shape=jax.ShapeDtypeStruct((M, D), BF16),
        compiler_params=_cparams(("parallel", "parallel")),
        name="merge",
    )(ya, yb, yc, wa, wb, wc, z, z, z)


def _final_norm_kernel(x_ref, w_ref, o_ref):
    x = x_ref[0]
    o_ref[0] = x * lax.rsqrt(jnp.mean(x * x, axis=-1, keepdims=True) + NORM_EPS) * w_ref[...]


def _final_norm(xt, w, ctx_len):
    B, T, D = xt.shape
    S = T - ctx_len
    cg = ctx_len // GRP
    return pl.pallas_call(
        _final_norm_kernel,
        grid=(B, S // GRP),
        in_specs=[pl.BlockSpec((1, GRP, D), lambda b, i: (b, cg + i, 0)),
                  pl.BlockSpec((1, D), lambda b, i: (0, 0))],
        out_specs=pl.BlockSpec((1, GRP, D), lambda b, i: (b, i, 0)),
        out_shape=jax.ShapeDtypeStruct((B, S, D), F32),
        compiler_params=_cparams(("parallel", "parallel")),
        name="final_norm",
    )(xt, w.reshape(1, D))


def _gqa_prep_kernel(z_ref, cos_ref, sin_ref, gq_ref, gk_ref, q_ref, k_ref, v_ref):
    cosf = cos_ref[...]
    sinf = sin_ref[...]

    def norm_rope(x, gain):
        y = x * lax.rsqrt(jnp.mean(x * x, axis=-1, keepdims=True) + NORM_EPS) * gain
        return y * cosf + pltpu.roll(y, GQ_HEAD // 2, axis=1) * sinf

    scale = GQ_HEAD ** -0.5 * math.log2(math.e)
    for h in range(GQ_HEADS):
        cols = slice(h * GQ_HEAD, (h + 1) * GQ_HEAD)
        q_ref[:, cols] = (norm_rope(z_ref[:, cols], gq_ref[...]) * scale).astype(BF16)
    for h in range(GQ_KV_HEADS):
        cols = slice(h * GQ_HEAD, (h + 1) * GQ_HEAD)
        zc = slice(GQ_WIDTH + h * GQ_HEAD, GQ_WIDTH + (h + 1) * GQ_HEAD)
        k_ref[:, cols] = norm_rope(z_ref[:, zc], gk_ref[...]).astype(BF16)
        vz = slice(GQ_WIDTH + GQ_KV_WIDTH + h * GQ_HEAD, GQ_WIDTH + GQ_KV_WIDTH + (h + 1) * GQ_HEAD)
        v_ref[:, 2 * h * GQ_HEAD:(2 * h + 1) * GQ_HEAD] = z_ref[:, vz].astype(BF16)
        v_ref[:, (2 * h + 1) * GQ_HEAD:(2 * h + 2) * GQ_HEAD] = jnp.ones((GRP, GQ_HEAD), BF16)


def _gqa_prep(z, cosf, sinf, gq, gk, gps):
    M = z.shape[0]
    zw = GQ_WIDTH + 2 * GQ_KV_WIDTH
    return pl.pallas_call(
        _gqa_prep_kernel,
        grid=(M // GRP,),
        in_specs=[pl.BlockSpec((GRP, zw), lambda i: (i, ZQ_OFF // zw)),
                  pl.BlockSpec((GRP, GQ_HEAD), lambda i: (i % gps, 0)),
                  pl.BlockSpec((GRP, GQ_HEAD), lambda i: (i % gps, 0)),
                  pl.BlockSpec((1, GQ_HEAD), lambda i: (0, 0)),
                  pl.BlockSpec((1, GQ_HEAD), lambda i: (0, 0))],
        out_specs=[pl.BlockSpec((GRP, GQ_WIDTH), lambda i: (i, 0)),
                   pl.BlockSpec((GRP, GQ_KV_WIDTH), lambda i: (i, 0)),
                   pl.BlockSpec((GRP, 2 * GQ_KV_WIDTH), lambda i: (i, 0))],
        out_shape=[jax.ShapeDtypeStruct((M, GQ_WIDTH), BF16),
                   jax.ShapeDtypeStruct((M, GQ_KV_WIDTH), BF16),
                   jax.ShapeDtypeStruct((M, 2 * GQ_KV_WIDTH), BF16)],
        compiler_params=_cparams(("parallel",)),
        name="gqa_prep",
    )(z, cosf, sinf, gq.reshape(1, GQ_HEAD), gk.reshape(1, GQ_HEAD))


def _gqa_attn_kernel(q_ref, k_ref, v_ref, o_ref, *, ctx_tiles, ctx_len, n_keys):
    qi = pl.program_id(2)

    def attend(nk):
        k = k_ref[0, 0:nk, :]
        v = v_ref[0, 0:nk, :]
        for h in range(GQ_GROUP):
            cols = slice(h * GQ_HEAD, (h + 1) * GQ_HEAD)
            s = _dot_nt(q_ref[0, :, cols], k)
            p = jnp.exp2(s - jnp.max(s, axis=-1, keepdims=True))
            ol = _dot(p.astype(BF16), v)
            o = ol[:, :GQ_HEAD] / ol[:, GQ_HEAD:GQ_HEAD + 1]
            o_ref[0, :, cols] = o.astype(o_ref.dtype)

    @pl.when(qi < ctx_tiles)
    def _():
        attend(ctx_len)

    @pl.when(qi >= ctx_tiles)
    def _():
        attend(n_keys)


def _gqa_attn(q, k, v, ctx_len, tq=256):
    B, T, _ = q.shape
    gw = GQ_GROUP * GQ_HEAD
    kern = functools.partial(_gqa_attn_kernel, ctx_tiles=ctx_len // tq, ctx_len=ctx_len, n_keys=T)
    return pl.pallas_call(
        kern,
        grid=(B, GQ_KV_HEADS, T // tq),
        in_specs=[pl.BlockSpec((1, tq, gw), lambda b, g, i: (b, i, g)),
                  pl.BlockSpec((1, T, GQ_HEAD), lambda b, g, i: (b, 0, g)),
                  pl.BlockSpec((1, T, 2 * GQ_HEAD), lambda b, g, i: (b, 0, g))],
        out_specs=pl.BlockSpec((1, tq, gw), lambda b, g, i: (b, i, g)),
        out_shape=jax.ShapeDtypeStruct((B, T, GQ_WIDTH), BF16),
        compiler_params=_cparams(("parallel", "parallel", "arbitrary")),
        name="gqa_attn",
    )(q, k, v)


def _na_kernel(q_ref, k_ref, v_ref, bias_ref, o_ref, qs_ref, ks_ref, vs_ref, *, ctx_len, rows):
    scale = NA_HEAD ** -0.5
    wr = min(NA_WIN_ROWS, rows)
    nwin = wr * GRID_W
    rpg = math.gcd(rows, 8)

    for hh in range(2):
        cols = slice(hh * NA_HEAD, (hh + 1) * NA_HEAD)
        qs_ref[hh] = (q_ref[0, :, cols] * scale).astype(BF16)
        ks_ref[hh] = k_ref[0, :, cols].astype(BF16)
        vs_ref[hh] = v_ref[0, :, cols].astype(BF16)

    for hh in range(2):
        cols = slice(hh * NA_HEAD, (hh + 1) * NA_HEAD)
        kc = ks_ref[hh, 0:ctx_len, :]
        vc = vs_ref[hh, 0:ctx_len, :]

        s = _dot_nt(qs_ref[hh, 0:ctx_len, :], kc)
        p = jnp.exp(s - jnp.max(s, axis=-1, keepdims=True))
        o = _dot(p.astype(BF16), vc) / jnp.sum(p, axis=-1, keepdims=True)
        o_ref[0, 0:ctx_len, cols] = o.astype(o_ref.dtype)

        def group_body(g, carry):
            i0 = g * rpg
            qoff = pl.multiple_of(ctx_len + i0 * GRID_W, rpg * GRID_W)
            qg = qs_ref[hh, pl.ds(qoff, rpg * GRID_W), :]
            s2 = _dot_nt(qg, kc)
            koffs, s1 = [], []
            for r in range(rpg):
                rs = jnp.clip(i0 + r - wr // 2, 0, rows - wr)
                koffs.append(pl.multiple_of(ctx_len + rs * GRID_W, GRID_W))
                kw = ks_ref[hh, pl.ds(koffs[r], nwin), :]
                s1.append(_dot_nt(qg[r * GRID_W:(r + 1) * GRID_W], kw) + bias_ref[hh, i0 + r - rs])
            s1 = jnp.concatenate(s1, axis=0)
            m = jnp.maximum(jnp.max(s1, axis=-1, keepdims=True), jnp.max(s2, axis=-1, keepdims=True))
            p1 = jnp.exp(s1 - m)
            p2 = jnp.exp(s2 - m)
            l = jnp.sum(p1, axis=-1, keepdims=True) + jnp.sum(p2, axis=-1, keepdims=True)
            p1 = p1.astype(BF16)
            o1 = [_dot(p1[r * GRID_W:(r + 1) * GRID_W], vs_ref[hh, pl.ds(koffs[r], nwin), :]) for r in range(rpg)]
            o = (jnp.concatenate(o1, axis=0) + _dot(p2.astype(BF16), vc)) / l
            o_ref[0, pl.ds(qoff, rpg * GRID_W), cols] = o.astype(o_ref.dtype)
            return carry

        lax.fori_loop(0, rows // rpg, group_body, 0)


def _na_bias_table(rpb, rows):
    wr = min(NA_WIN_ROWS, rows)
    cols = np.arange(GRID_W)
    cstart = np.clip(cols - NA_WIN_COLS // 2, 0, GRID_W - NA_WIN_COLS)
    c = np.arange(GRID_W)
    inside = (c[None, :] >= cstart[:, None]) & (c[None, :] < cstart[:, None] + NA_WIN_COLS)
    col_off = np.clip(c[None, :] - cols[:, None] + (NA_WIN_COLS - 1), 0, 2 * NA_WIN_COLS - 2)
    onehot = (np.arange(2 * NA_WIN_COLS - 1)[:, None, None] == col_off[None]).astype(np.float32)
    tc = jnp.einsum('...ro,ojc->...rjc', rpb, onehot, precision=lax.Precision.HIGHEST)
    tc = jnp.where(inside, tc, NEG_BIG)
    t = jnp.stack([tc[..., NA_WIN_ROWS - 1 - d:NA_WIN_ROWS - 1 - d + wr, :, :] for d in range(wr)], axis=-4)
    t = jnp.swapaxes(t, -3, -2)
    return t.reshape(rpb.shape[:-2] + (wr, GRID_W, wr * GRID_W))


def _na_attn(z, bias, ctx_len):
    B, T, _ = z.shape
    rows = (T - ctx_len) // GRID_W
    wr = bias.shape[1]
    pw = 2 * NA_HEAD
    c0 = ZN_OFF // pw
    per = NA_WIDTH // pw
    scr = pltpu.VMEM((2, T, NA_HEAD), BF16)
    return pl.pallas_call(
        functools.partial(_na_kernel, ctx_len=ctx_len, rows=rows),
        grid=(B, per),
        in_specs=[pl.BlockSpec((1, T, pw), lambda b, h: (b, 0, c0 + h)),
                  pl.BlockSpec((1, T, pw), lambda b, h: (b, 0, c0 + per + h)),
                  pl.BlockSpec((1, T, pw), lambda b, h: (b, 0, c0 + 2 * per + h)),
                  pl.BlockSpec((2, wr, GRID_W, wr * GRID_W), lambda b, h: (h, 0, 0, 0))],
        out_specs=pl.BlockSpec((1, T, pw), lambda b, h: (b, 0, h)),
        out_shape=jax.ShapeDtypeStruct((B, T, NA_WIDTH), BF16),
        scratch_shapes=[scr, scr, scr],
        compiler_params=_cparams(("parallel", "parallel")),
        name="na_attn",
    )(z, z, z, bias)


def _rw_prep_kernel(z_ref, zp_ref, zn_ref, l_ref, lp_ref, ln_ref,
                    mup_ref, mun_ref, lmup_ref, lmun_ref, kk_ref_, ka_ref, w0_ref, wup_ref, a0_ref, aup_ref,
                    gup_ref,
                    r_out, v_out, kk_out, g_out, cs_out, lw_out, kd_out, bd_out, *, gps, ctx_groups):
    t = pl.program_id(1)
    first = jnp.logical_or(t == 0, t == ctx_groups)
    last = jnp.logical_or(t == ctx_groups - 1, t == gps - 1)
    pv = jnp.where(first, 0.0, 1.0)
    nv = jnp.where(last, 0.0, 1.0)
    row = lax.broadcasted_iota(jnp.int32, (GRP, 1), 0)

    def shift(x_ref, p_ref, n_ref, mp_ref, mn_ref):
        x = x_ref[...]
        xp = jnp.where(row == 0, p_ref[7:8, :] * pv, pltpu.roll(x, 1, axis=0))
        xn = jnp.where(row == GRP - 1, n_ref[0:1, :] * nv, pltpu.roll(x, GRP - 1, axis=0))
        return x + mp_ref[...] * (xp - x) + mn_ref[...] * (xn - x)

    zs = shift(z_ref, zp_ref, zn_ref, mup_ref, mun_ref)
    ls = shift(l_ref, lp_ref, ln_ref, lmup_ref, lmun_ref)
    r = zs[:, 0:RW_WIDTH]
    k = zs[:, RW_WIDTH:2 * RW_WIDTH]
    v = zs[:, 2 * RW_WIDTH:3 * RW_WIDTH]

    hi = lax.Precision.HIGHEST
    ci = lax.broadcasted_iota(jnp.int32, (RW_WIDTH, RW_WIDTH), 0) // RW_HEAD
    cj = lax.broadcasted_iota(jnp.int32, (RW_WIDTH, RW_WIDTH), 1) // RW_HEAD
    head_ones = jnp.where(ci == cj, 1.0, 0.0).astype(BF16)
    kk = k * kk_ref_[...]
    ss = sum(_dot(part, head_ones) for part in _split3(kk * kk))
    kk = kk / jnp.maximum(jnp.sqrt(ss), 1e-12)

    ti = lax.broadcasted_iota(jnp.int32, (GRP, GRP), 0)
    tj = lax.broadcasted_iota(jnp.int32, (GRP, GRP), 1)
    same = (ti // SCAN_CHUNK) == (tj // SCAN_CHUNK)

    def heads_out(ref, val, lead):
        for h in range(RW_HEADS):
            ref[lead + (h,)] = val[:, h * RW_HEAD:(h + 1) * RW_HEAD]

    heads_out(r_out, r, (0,))
    heads_out(v_out, v, (0,))
    heads_out(kk_out, kk, (0,))
    gd = ls[:, 4 * LORA_PAD:]
    g_out[...] = _dot(jax.nn.sigmoid(gd).astype(BF16), gup_ref[...].astype(BF16))

    for d in range(2):
        wl = ls[:, d * LORA_PAD:(d + 1) * LORA_PAD]
        al = ls[:, (2 + d) * LORA_PAD:(3 + d) * LORA_PAD]
        w_raw = w0_ref[d] + _dot(jnp.tanh(wl), wup_ref[d], precision=hi)
        lw = -math.exp(-0.5) * jax.nn.sigmoid(w_raw)
        a = jax.nn.sigmoid(a0_ref[d] + _dot(al.astype(BF16), aup_ref[d].astype(BF16)))
        kd = k * (1.0 + (a - 1.0) * ka_ref[...])
        bd = kk * a
        tri = jnp.where(jnp.logical_and(same, (ti >= tj) if d == 0 else (ti <= tj)), 1.0, 0.0).astype(BF16)
        cs = sum(_dot(tri, part) for part in _split3(lw))
        heads_out(cs_out, cs, (d, 0))
        heads_out(lw_out, lw, (d, 0))
        heads_out(kd_out, kd, (d, 0))
        heads_out(bd_out, bd, (d, 0))


def _rw_prep(z, p, B, gps, ctx_groups):
    M = z.shape[0]
    T = gps * GRP
    nb8 = M // 8
    rb = GRP // 8
    zc = ZRKV_OFF // 1536
    lc = ZL_OFF // ZL_W

    def cur(c):
        return lambda b, t: (b * gps + t, c)

    def prev(c):
        return lambda b, t: (jnp.maximum((b * gps + t) * rb - 1, 0), c)

    def nxt(c):
        return lambda b, t: (jnp.minimum((b * gps + t + 1) * rb, nb8 - 1), c)

    def const(shape):
        nd = len(shape)
        return pl.BlockSpec(shape, lambda b, t: (0,) * nd)

    hm = pl.BlockSpec((1, RW_HEADS, GRP, RW_HEAD), lambda b, t: (b, 0, t, 0))
    hm2 = pl.BlockSpec((2, 1, RW_HEADS, GRP, RW_HEAD), lambda b, t: (0, b, 0, t, 0))
    hshape = jax.ShapeDtypeStruct((B, RW_HEADS, T, RW_HEAD), F32)
    hshape2 = jax.ShapeDtypeStruct((2, B, RW_HEADS, T, RW_HEAD), F32)
    return pl.pallas_call(
        functools.partial(_rw_prep_kernel, gps=gps, ctx_groups=ctx_groups),
        grid=(B, gps),
        in_specs=[pl.BlockSpec((GRP, 1536), cur(zc)),
                  pl.BlockSpec((8, 1536), prev(zc)),
                  pl.BlockSpec((8, 1536), nxt(zc)),
                  pl.BlockSpec((GRP, ZL_W), cur(lc)),
                  pl.BlockSpec((8, ZL_W), prev(lc)),
                  pl.BlockSpec((8, ZL_W), nxt(lc)),
                  const((1, 1536)), const((1, 1536)), const((1, ZL_W)), const((1, ZL_W)),
                  const((1, RW_WIDTH)), const((1, RW_WIDTH)),
                  const((2, 1, RW_WIDTH)), const((2, LORA_PAD, RW_WIDTH)),
                  const((2, 1, RW_WIDTH)), const((2, LORA_PAD, RW_WIDTH)),
                  const((256, RW_WIDTH))],
        out_specs=[hm, hm, hm, pl.BlockSpec((GRP, RW_WIDTH), lambda b, t: (b * gps + t, 0)),
                   hm2, hm2, hm2, hm2],
        out_shape=[hshape, hshape, hshape, jax.ShapeDtypeStruct((M, RW_WIDTH), F32),
                   hshape2, hshape2, hshape2, hshape2],
        compiler_params=_cparams(("parallel", "parallel")),
        name="rw_prep",
    )(z, z, z, z, z, z, p['mu_p'], p['mu_n'], p['lmu_p'], p['lmu_n'], p['k_k'], p['k_a'],
      p['w0'], p['w_up'], p['a0'], p['a_up'], p['g_up'])


def _bmm(a, b, nt=False):
    a = a.astype(BF16)
    b = b.astype(BF16)
    out = []
    for h in range(a.shape[0]):
        out.append(_dot_nt(a[h], b[h]) if nt else _dot(a[h], b[h]))
    return jnp.stack(out, axis=0)


def _bmm_tn(a, b):
    a = a.astype(BF16)
    b = b.astype(BF16)
    out = []
    for h in range(a.shape[0]):
        out.append(_dot(a[h], b[h], (((0,), (0,)), ((), ()))))
    return jnp.stack(out, axis=0)


def _rw_intra_kernel(r_ref, v_ref, kk_ref, cs_ref, lw_ref, kd_ref, bd_ref,
                     rh_ref, o0_ref, gm_ref, sd_ref, pc_ref, *, nck):
    d = pl.program_id(1)
    C, H, N = SCAN_CHUNK, RW_HEADS, RW_HEAD
    G = H * nck

    fwd = d == 0
    ti = lax.broadcasted_iota(jnp.int32, (C, C), 0)
    tj = lax.broadcasted_iota(jnp.int32, (C, C), 1)
    before = (ti - tj) * jnp.where(fwd, 1, -1)
    strict = (before > 0)[None]
    incl = (before >= 0)[None]
    eye = (ti == tj)[None]

    def chunks(x):
        return x.reshape(G, C, N)

    r = chunks(r_ref[0])
    v = chunks(v_ref[0])
    cs = chunks(cs_ref[0, 0])
    pinv = jnp.exp(-cs)
    ra = r * jnp.exp(cs)
    at = -chunks(kk_ref[0]) * jnp.exp(cs - chunks(lw_ref[0, 0]))
    bt = chunks(bd_ref[0, 0]) * pinv
    kt = chunks(kd_ref[0, 0]) * pinv
    pc = jnp.exp(jnp.where(fwd, cs[:, C - 1:C, :], cs[:, 0:1, :]))

    lhs = jnp.concatenate([at, ra], axis=1)
    rhs = jnp.concatenate([bt, kt], axis=1)
    gmat = _bmm(lhs, rhs, nt=True)
    a_ab = jnp.where(strict, gmat[:, :C, :C], 0.0)
    a_ak = jnp.where(strict, gmat[:, :C, C:], 0.0)
    a_rb = jnp.where(incl, gmat[:, C:, :C], 0.0)
    a_rk = jnp.where(incl, gmat[:, C:, C:], 0.0)

    tm = jnp.where(eye, 1.0, 0.0) + a_ab
    pw = a_ab
    for _ in range(int(math.log2(C)) - 1):
        pw = _bmm(pw, pw)
        tm = tm + _bmm(tm, pw)

    au = _bmm(tm, jnp.concatenate([at, _bmm(a_ak, v)], axis=2))
    ro = _bmm(a_rb, au)
    gs = _bmm_tn(au, bt * pc)

    def unchunk(x):
        return x.reshape(H, nck * C, N)

    rh_ref[0, 0] = unchunk(ra + ro[:, :, :N])
    o0_ref[0, 0] = unchunk(ro[:, :, N:] + _bmm(a_rk, v))
    gm_ref[0, 0] = unchunk(gs[:, :N])
    sd_ref[0, 0] = unchunk(gs[:, N:] + _bmm_tn(v, kt * pc))
    pc_ref[0, 0, 0] = pc.reshape(H, nck, N)


def _rw_seq_kernel(rhf_ref, o0f_ref, gmf_ref, sdf_ref, pcf_ref, rhb_ref, o0b_ref, gmb_ref, sdb_ref, pcb_ref,
                   of_ref, ob_ref, s_ref, *, nck):
    C = SCAN_CHUNK

    @pl.when(pl.program_id(1) == 0)
    def _():
        s_ref[...] = jnp.zeros(s_ref.shape, F32)

    dirs = ((rhf_ref, o0f_ref, gmf_ref, sdf_ref, pcf_ref, of_ref, range(nck)),
            (rhb_ref, o0b_ref, gmb_ref, sdb_ref, pcb_ref, ob_ref, range(nck - 1, -1, -1)))
    nb = s_ref.shape[1]
    chains = [(d, b) for d in range(2) for b in range(nb)]
    state = {ch: s_ref[ch[0], ch[1]] for ch in chains}
    for step in range(nck):
        for d, b in chains:
            rh_ref, o0_ref, gm_ref, sd_ref, pc_ref, o_ref, order = dirs[d]
            c = order[step]
            rows = slice(c * C, (c + 1) * C)
            s = state[(d, b)]
            o_ref[b, :, rows, :] = _bmm(rh_ref[0, b, :, rows, :], s, nt=True) + o0_ref[0, b, :, rows, :]
            state[(d, b)] = (s * pc_ref[0, b, 0, :, c:c + 1, :] + _bmm(s, gm_ref[0, b, :, rows, :])
                             + sd_ref[0, b, :, rows, :])
    for d, b in chains:
        s_ref[d, b] = state[(d, b)]


def _rw_scan(r, v, kk, cs, lw, kd, bd, ctx_len, nck=4):
    B, H, T, N = r.shape
    ct = nck * SCAN_CHUNK
    nblk = T // ct
    ncb = ctx_len // ct
    assert ctx_len % ct == 0 and T % ct == 0

    shared = pl.BlockSpec((1, H, ct, N), lambda b, d, j: (b, 0, j, 0))
    per_dir = pl.BlockSpec((1, 1, H, ct, N), lambda b, d, j: (d, b, 0, j, 0))
    pc_spec = pl.BlockSpec((1, 1, 1, H, nck, N), lambda b, d, j: (d, b, j, 0, 0, 0))
    big = jax.ShapeDtypeStruct((2, B, H, T, N), F32)
    rh, o0, gm, sd, pc = pl.pallas_call(
        functools.partial(_rw_intra_kernel, nck=nck),
        grid=(B, 2, nblk),
        in_specs=[shared, shared, shared, per_dir, per_dir, per_dir, per_dir],
        out_specs=[per_dir, per_dir, per_dir, per_dir, pc_spec],
        out_shape=[big, big, big, big, jax.ShapeDtypeStruct((2, B, nblk, H, nck, N), F32)],
        compiler_params=_cparams(("parallel", "parallel", "parallel")),
        name="rw_intra",
    )(r, v, kk, cs, lw, kd, bd)

    def bwd_blk(j):
        return jnp.where(j < ncb, ncb - 1 - j, nblk - 1 - (j - ncb))

    bb = 2 if B % 2 == 0 else 1
    f_big = pl.BlockSpec((1, bb, H, ct, N), lambda b, j: (0, b, 0, j, 0))
    b_big = pl.BlockSpec((1, bb, H, ct, N), lambda b, j: (1, b, 0, bwd_blk(j), 0))
    f_pc = pl.BlockSpec((1, bb, 1, H, nck, N), lambda b, j: (0, b, j, 0, 0, 0))
    b_pc = pl.BlockSpec((1, bb, 1, H, nck, N), lambda b, j: (1, b, bwd_blk(j), 0, 0, 0))
    out = jax.ShapeDtypeStruct((B, H, T, N), F32)
    return pl.pallas_call(
        functools.partial(_rw_seq_kernel, nck=nck),
        grid=(B // bb, nblk),
        in_specs=[f_big, f_big, f_big, f_big, f_pc, b_big, b_big, b_big, b_big, b_pc],
        out_specs=[pl.BlockSpec((bb, H, ct, N), lambda b, j: (b, 0, j, 0)),
                   pl.BlockSpec((bb, H, ct, N), lambda b, j: (b, 0, bwd_blk(j), 0))],
        out_shape=[out, out],
        scratch_shapes=[pltpu.VMEM((2, bb, H, N, N), F32)],
        compiler_params=_cparams(("parallel", "arbitrary")),
        name="rw_seq",
    )(rh, o0, gm, sd, pc, rh, o0, gm, sd, pc)


def _rw_out_kernel(of_ref, ob_ref, r_ref, v_ref, kd_ref, g_ref, rk_ref, lnw_ref, lnb_ref, y_ref):
    ys = []
    for h in range(RW_HEADS):
        wkv = of_ref[0, h] + ob_ref[0, h]
        mu = jnp.mean(wkv, axis=-1, keepdims=True)
        cen = wkv - mu
        var = jnp.mean(cen * cen, axis=-1, keepdims=True)
        y = cen * lax.rsqrt(var + RW_GN_EPS) * lnw_ref[h] + lnb_ref[h]
        r = r_ref[0, h]
        rk = rk_ref[h]
        bonus = (jnp.sum(r * kd_ref[0, 0, h] * rk, axis=-1, keepdims=True)
                 + jnp.sum(r * kd_ref[1, 0, h] * rk, axis=-1, keepdims=True)) * v_ref[0, h]
        ys.append(y + bonus)
    y_ref[...] = (jnp.concatenate(ys, axis=-1) * g_ref[...]).astype(y_ref.dtype)


def _rw_out(o_f, o_b, r, v, kd, g, rk, lnw, lnb, gps):
    B, H, T, N = r.shape
    M = B * T
    hm = pl.BlockSpec((1, H, GRP, N), lambda b, t: (b, 0, t, 0))
    hm2 = pl.BlockSpec((2, 1, H, GRP, N), lambda b, t: (0, b, 0, t, 0))
    par = pl.BlockSpec((H, 1, N), lambda b, t: (0, 0, 0))
    return pl.pallas_call(
        _rw_out_kernel,
        grid=(B, gps),
        in_specs=[hm, hm, hm, hm, hm2, pl.BlockSpec((GRP, RW_WIDTH), lambda b, t: (b * gps + t, 0)),
                  par, par, par],
        out_specs=pl.BlockSpec((GRP, RW_WIDTH), lambda b, t: (b * gps + t, 0)),
        out_shape=jax.ShapeDtypeStruct((M, RW_WIDTH), BF16),
        compiler_params=_cparams(("parallel", "parallel")),
        name="rw_out",
    )(o_f, o_b, r, v, kd, g, rk.reshape(H, 1, N), lnw.reshape(H, 1, N), lnb.reshape(H, 1, N))


RW_PAIRS = RW_HEADS // 2
PAIR_W = 2 * RW_HEAD
PC_ROWS = 8


def _head0_lanes(shape):
    return lax.broadcasted_iota(jnp.int32, shape, len(shape) - 1) < RW_HEAD


def _per_head_rows(x):
    m0 = _head0_lanes(x.shape)
    return jnp.concatenate([jnp.where(m0, x, 0.0), jnp.where(m0, 0.0, x)], axis=-2)


def _pair_diag(full):
    half = full.shape[-2] // 2
    top, bot = full[..., :half, :], full[..., half:, :]
    return jnp.where(_head0_lanes(top.shape), top, bot)


def _rw_prep_kernel(z_ref, zp_ref, zn_ref, l_ref, lp_ref, ln_ref,
                    mup_ref, mun_ref, lmup_ref, lmun_ref, kk_ref_, ka_ref, rk_ref, w0_ref, wup_ref, a0_ref,
                    aup_ref, gup_ref,
                    v_out, g_out, bonus_out, ra_out, at_out, bt_out, kt_out, pc_out, *, gps, ctx_groups):
    t = pl.program_id(1)
    first = jnp.logical_or(t == 0, t == ctx_groups)
    last = jnp.logical_or(t == ctx_groups - 1, t == gps - 1)
    pv = jnp.where(first, 0.0, 1.0)
    nv = jnp.where(last, 0.0, 1.0)
    row = lax.broadcasted_iota(jnp.int32, (GRP, 1), 0)
    C = SCAN_CHUNK

    def shift(x_ref, p_ref, n_ref, mp_ref, mn_ref):
        x = x_ref[...]
        xp = jnp.where(row == 0, p_ref[7:8, :] * pv, pltpu.roll(x, 1, axis=0))
        xn = jnp.where(row == GRP - 1, n_ref[0:1, :] * nv, pltpu.roll(x, GRP - 1, axis=0))
        return x + mp_ref[...] * (xp - x) + mn_ref[...] * (xn - x)

    zs = shift(z_ref, zp_ref, zn_ref, mup_ref, mun_ref)
    ls = shift(l_ref, lp_ref, ln_ref, lmup_ref, lmun_ref)
    r = zs[:, 0:RW_WIDTH]
    k = zs[:, RW_WIDTH:2 * RW_WIDTH]
    v = zs[:, 2 * RW_WIDTH:3 * RW_WIDTH]

    ci = lax.broadcasted_iota(jnp.int32, (PAIR_W, PAIR_W), 0) // RW_HEAD
    cj = lax.broadcasted_iota(jnp.int32, (PAIR_W, PAIR_W), 1) // RW_HEAD
    head_ones = jnp.where(ci == cj, 1.0, 0.0).astype(BF16)

    def head_sum(x):
        parts = _split3(x)
        return jnp.concatenate([sum(_dot(part[:, p * PAIR_W:(p + 1) * PAIR_W], head_ones) for part in parts)
                                for p in range(RW_PAIRS)], axis=1)

    def dot_f32(x, w):
        x1, x2, _ = _split3(x)
        w1, w2, _ = _split3(w)
        return _dot(x1, w1) + (_dot(x2, w1) + _dot(x1, w2))

    kk = k * kk_ref_[...]
    kk = kk / jnp.maximum(jnp.sqrt(head_sum(kk * kk)), 1e-12)

    ti = lax.broadcasted_iota(jnp.int32, (GRP, GRP), 0)
    tj = lax.broadcasted_iota(jnp.int32, (GRP, GRP), 1)
    same = (ti // C) == (tj // C)

    v_out[...] = v.astype(BF16)
    gd = ls[:, 4 * LORA_PAD:]
    g_out[...] = _dot(jax.nn.sigmoid(gd).astype(BF16), gup_ref[...].astype(BF16))

    kd_sum = jnp.zeros_like(k)
    for d in range(2):
        wl = ls[:, d * LORA_PAD:(d + 1) * LORA_PAD]
        al = ls[:, (2 + d) * LORA_PAD:(3 + d) * LORA_PAD]
        w_raw = w0_ref[d] + dot_f32(jnp.tanh(wl), wup_ref[d])
        lw = -math.exp(-0.5) * jax.nn.sigmoid(w_raw)
        a = jax.nn.sigmoid(a0_ref[d] + _dot(al.astype(BF16), aup_ref[d].astype(BF16)))
        kd = k * (1.0 + (a - 1.0) * ka_ref[...])
        kd_sum = kd_sum + kd
        tri = jnp.where(jnp.logical_and(same, (ti >= tj) if d == 0 else (ti <= tj)), 1.0, 0.0).astype(BF16)
        cs = sum(_dot(tri, part) for part in _split3(lw))
        pinv = jnp.exp(-cs)
        ra_out[d] = (r * jnp.exp(cs)).astype(BF16)
        at_out[d] = (-kk * jnp.exp(cs - lw)).astype(BF16)
        bt_out[d] = (kk * a * pinv).astype(BF16)
        kt_out[d] = (kd * pinv).astype(BF16)
        for c in range(GRP // C):
            end = c * C + (C - 1 if d == 0 else 0)
            pc_out[d, c * PC_ROWS:(c + 1) * PC_ROWS, :] = jnp.broadcast_to(jnp.exp(cs[end:end + 1, :]),
                                                                          (PC_ROWS, RW_WIDTH))
    bonus_out[...] = head_sum(r * kd_sum * rk_ref[...]) * v


def _rw_prep(z, p, B, gps, ctx_groups):
    M = z.shape[0]
    nb8 = M // 8
    rb = GRP // 8
    zc = ZRKV_OFF // 1536
    lc = ZL_OFF // ZL_W
    pcr = GRP // SCAN_CHUNK * PC_ROWS

    def cur(c):
        return lambda b, t: (b * gps + t, c)

    def prev(c):
        return lambda b, t: (jnp.maximum((b * gps + t) * rb - 1, 0), c)

    def nxt(c):
        return lambda b, t: (jnp.minimum((b * gps + t + 1) * rb, nb8 - 1), c)

    def const(shape):
        nd = len(shape)
        return pl.BlockSpec(shape, lambda b, t: (0,) * nd)

    tok = pl.BlockSpec((GRP, RW_WIDTH), lambda b, t: (b * gps + t, 0))
    tok2 = pl.BlockSpec((2, GRP, RW_WIDTH), lambda b, t: (0, b * gps + t, 0))
    s1 = jax.ShapeDtypeStruct((M, RW_WIDTH), F32)
    s2 = jax.ShapeDtypeStruct((2, M, RW_WIDTH), BF16)
    return pl.pallas_call(
        functools.partial(_rw_prep_kernel, gps=gps, ctx_groups=ctx_groups),
        grid=(B, gps),
        in_specs=[pl.BlockSpec((GRP, 1536), cur(zc)),
                  pl.BlockSpec((8, 1536), prev(zc)),
                  pl.BlockSpec((8, 1536), nxt(zc)),
                  pl.BlockSpec((GRP, ZL_W), cur(lc)),
                  pl.BlockSpec((8, ZL_W), prev(lc)),
                  pl.BlockSpec((8, ZL_W), nxt(lc)),
                  const((1, 1536)), const((1, 1536)), const((1, ZL_W)), const((1, ZL_W)),
                  const((1, RW_WIDTH)), const((1, RW_WIDTH)), const((1, RW_WIDTH)),
                  const((2, 1, RW_WIDTH)), const((2, LORA_PAD, RW_WIDTH)),
                  const((2, 1, RW_WIDTH)), const((2, LORA_PAD, RW_WIDTH)),
                  const((256, RW_WIDTH))],
        out_specs=[tok, tok, tok, tok2, tok2, tok2, tok2,
                   pl.BlockSpec((2, pcr, RW_WIDTH), lambda b, t: (0, b * gps + t, 0))],
        out_shape=[jax.ShapeDtypeStruct((M, RW_WIDTH), BF16), s1, s1, s2, s2, s2, s2,
                   jax.ShapeDtypeStruct((2, M // GRP * pcr, RW_WIDTH), F32)],
        compiler_params=_cparams(("parallel", "parallel")),
        name="rw_prep",
    )(z, z, z, z, z, z, p['mu_p'], p['mu_n'], p['lmu_p'], p['lmu_n'], p['k_k'], p['k_a'], p['r_k'],
      p['w0'], p['w_up'], p['a0'], p['a_up'], p['g_up'])


def _bmm(a, b, nt=False):
    a = a.astype(BF16)
    b = b.astype(BF16)
    out = []
    for h in range(a.shape[0]):
        out.append(_dot_nt(a[h], b[h]) if nt else _dot(a[h], b[h]))
    return jnp.stack(out, axis=0)


def _bmm_tn(a, b):
    a = a.astype(BF16)
    b = b.astype(BF16)
    out = []
    for h in range(a.shape[0]):
        out.append(_dot(a[h], b[h], (((0,), (0,)), ((), ()))))
    return jnp.stack(out, axis=0)


def _rw_intra_kernel(ra_ref, at_ref, bt_ref, kt_ref, v_ref, pc_ref, rh_ref, o0_ref, gm_ref, sd_ref, *, nck):
    d = pl.program_id(1)
    C, P = SCAN_CHUNK, RW_PAIRS

    def pairs(x):
        x = x.reshape(nck, x.shape[0] // nck, RW_WIDTH)
        return jnp.concatenate([x[:, :, p * PAIR_W:(p + 1) * PAIR_W] for p in range(P)], axis=0)

    def unpairs(y):
        rows = y.shape[1] * nck
        return jnp.concatenate([y[p * nck:(p + 1) * nck].reshape(rows, PAIR_W) for p in range(P)], axis=1)

    ti = lax.broadcasted_iota(jnp.int32, (C, PAIR_W), 0)
    tj = lax.broadcasted_iota(jnp.int32, (C, PAIR_W), 1) % C
    before = (ti - tj) * jnp.where(d == 0, 1, -1)
    strict = (before > 0)[None]
    incl = (before >= 0)[None]
    eye = (ti == tj)[None]

    ra = pairs(ra_ref[0])
    at = pairs(at_ref[0])
    bt = pairs(bt_ref[0])
    kt = pairs(kt_ref[0])
    v = pairs(v_ref[...])
    pc = pairs(pc_ref[0])[:, 0:1, :]

    lhs = jnp.concatenate([at, ra], axis=1)
    rhs = jnp.concatenate([_per_head_rows(bt), _per_head_rows(kt)], axis=1)
    gmat = _bmm(lhs, rhs, nt=True)
    a_ab = jnp.where(strict, gmat[:, :C, :PAIR_W], 0.0)
    a_ak = jnp.where(strict, gmat[:, :C, PAIR_W:], 0.0)
    a_rb = jnp.where(incl, gmat[:, C:, :PAIR_W], 0.0)
    a_rk = jnp.where(incl, gmat[:, C:, PAIR_W:], 0.0)

    tm = jnp.where(eye, 1.0, 0.0) + a_ab
    pw = a_ab
    for _ in range(int(math.log2(C)) - 1):
        pw = _bmm(pw, _per_head_rows(pw))
        tm = tm + _bmm(tm, _per_head_rows(pw))

    vrows = _per_head_rows(v)
    x0 = _bmm(a_ak, vrows)
    au = _bmm(tm, jnp.concatenate([_per_head_rows(at), _per_head_rows(x0)], axis=2))
    ah, u0 = au[:, :, :PAIR_W], au[:, :, PAIR_W:]
    upper = jnp.concatenate([_per_head_rows(ah), _per_head_rows(u0)], axis=2)
    lower = jnp.concatenate([jnp.zeros_like(vrows), vrows], axis=2)
    ro = _bmm(jnp.concatenate([a_rb, a_rk], axis=2),
              jnp.concatenate([upper, lower], axis=1))
    bh = bt * pc
    kh = kt * pc
    rh_ref[0] = unpairs(ra + ro[:, :, :PAIR_W]).astype(rh_ref.dtype)
    o0_ref[0] = unpairs(ro[:, :, PAIR_W:])
    gm_ref[0] = unpairs(_pair_diag(_bmm_tn(ah, bh))).astype(gm_ref.dtype)
    sd_ref[0] = unpairs(_pair_diag(_bmm_tn(jnp.concatenate([u0, v], axis=1), jnp.concatenate([bh, kh], axis=1))))


def _rw_seq_kernel(rhf_ref, o0f_ref, gmf_ref, sdf_ref, pcf_ref, rhb_ref, o0b_ref, gmb_ref, sdb_ref, pcb_ref,
                   of_ref, ob_ref, s_ref, *, nck):
    C = SCAN_CHUNK

    @pl.when(pl.program_id(0) == 0)
    def _():
        s_ref[...] = jnp.zeros(s_ref.shape, F32)

    dirs = ((rhf_ref, o0f_ref, gmf_ref, sdf_ref, pcf_ref, of_ref, range(nck)),
            (rhb_ref, o0b_ref, gmb_ref, sdb_ref, pcb_ref, ob_ref, range(nck - 1, -1, -1)))
    chains = [(d, b, p) for d in range(2) for b in range(s_ref.shape[1]) for p in range(RW_PAIRS)]
    state = {ch: s_ref[ch] for ch in chains}
    for step in range(nck):
        for ch in chains:
            d, b, p = ch
            rh_ref, o0_ref, gm_ref, sd_ref, pc_ref, o_ref, order = dirs[d]
            c = order[step]
            rows = slice(c * C, (c + 1) * C)
            cols = slice(p * PAIR_W, (p + 1) * PAIR_W)
            s = state[ch]
            sb = s.astype(BF16)
            o_ref[b, rows, cols] = _dot_nt(rh_ref[0, b, rows, cols], _per_head_rows(sb)) + o0_ref[0, b, rows, cols]
            state[ch] = (s * pc_ref[0, b, c * PC_ROWS:c * PC_ROWS + 1, cols]
                         + _dot(sb, _per_head_rows(gm_ref[0, b, rows, cols])) + sd_ref[0, b, rows, cols])
    for ch in chains:
        s_ref[ch] = state[ch]


def _rw_scan(ra, at, bt, kt, v, pc, B, ctx_len, nck=4):
    M = v.shape[0]
    T = M // B
    W = RW_WIDTH
    ct = nck * SCAN_CHUNK
    nblk = T // ct
    ncb = ctx_len // ct
    pcr = nck * PC_ROWS
    assert ctx_len % ct == 0 and T % ct == 0

    per_dir = pl.BlockSpec((1, ct, W), lambda b, d, j: (d, b * nblk + j, 0))
    outs = [jax.ShapeDtypeStruct((2, M, W), BF16), jax.ShapeDtypeStruct((2, M, W), F32),
            jax.ShapeDtypeStruct((2, M, W), BF16), jax.ShapeDtypeStruct((2, M, W), F32)]
    rh, o0, gm, sd = pl.pallas_call(
        functools.partial(_rw_intra_kernel, nck=nck),
        grid=(B, 2, nblk),
        in_specs=[per_dir, per_dir, per_dir, per_dir,
                  pl.BlockSpec((ct, W), lambda b, d, j: (b * nblk + j, 0)),
                  pl.BlockSpec((1, pcr, W), lambda b, d, j: (d, b * nblk + j, 0))],
        out_specs=[per_dir, per_dir, per_dir, per_dir],
        out_shape=outs,
        compiler_params=_cparams(("parallel", "parallel", "parallel")),
        name="rw_intra",
    )(ra, at, bt, kt, v, pc)

    def bwd_blk(j):
        return jnp.where(j < ncb, ncb - 1 - j, nblk - 1 - (j - ncb))

    def view(x, rows):
        return x.reshape(2, B, rows, W)

    f_big = pl.BlockSpec((1, B, ct, W), lambda j: (0, 0, j, 0))
    b_big = pl.BlockSpec((1, B, ct, W), lambda j: (1, 0, bwd_blk(j), 0))
    f_pc = pl.BlockSpec((1, B, pcr, W), lambda j: (0, 0, j, 0))
    b_pc = pl.BlockSpec((1, B, pcr, W), lambda j: (1, 0, bwd_blk(j), 0))
    out = jax.ShapeDtypeStruct((B, T, W), F32)
    big = [view(x, T) for x in (rh, o0, gm, sd)]
    pcv = view(pc, T // SCAN_CHUNK * PC_ROWS)
    return pl.pallas_call(
        functools.partial(_rw_seq_kernel, nck=nck),
        grid=(nblk,),
        in_specs=[f_big, f_big, f_big, f_big, f_pc, b_big, b_big, b_big, b_big, b_pc],
        out_specs=[pl.BlockSpec((B, ct, W), lambda j: (0, j, 0)),
                   pl.BlockSpec((B, ct, W), lambda j: (0, bwd_blk(j), 0))],
        out_shape=[out, out],
        scratch_shapes=[pltpu.VMEM((2, B, RW_PAIRS, RW_HEAD, PAIR_W), F32)],
        compiler_params=_cparams(("arbitrary",)),
        name="rw_seq",
    )(*big, pcv, *big, pcv)


def _rw_out_kernel(of_ref, ob_ref, g_ref, bonus_ref, lnw_ref, lnb_ref, y_ref):
    ci = lax.broadcasted_iota(jnp.int32, (PAIR_W, PAIR_W), 0) // RW_HEAD
    cj = lax.broadcasted_iota(jnp.int32, (PAIR_W, PAIR_W), 1) // RW_HEAD
    head_mean = jnp.where(ci == cj, 1.0 / RW_HEAD, 0.0).astype(BF16)

    def mean(x):
        return sum(_dot(part, head_mean) for part in _split3(x))

    for p in range(RW_PAIRS):
        cols = slice(p * PAIR_W, (p + 1) * PAIR_W)
        wkv = of_ref[:, cols] + ob_ref[:, cols]
        cen = wkv - mean(wkv)
        y = cen * lax.rsqrt(mean(cen * cen) + RW_GN_EPS) * lnw_ref[:, cols] + lnb_ref[:, cols] + bonus_ref[:, cols]
        y_ref[:, cols] = (y * g_ref[:, cols]).astype(y_ref.dtype)


def _rw_out(o_f, o_b, g, bonus, lnw, lnb):
    M, W = g.shape
    tm = _tile(M, 512)
    tok = pl.BlockSpec((tm, W), lambda i: (i, 0))
    par = pl.BlockSpec((1, W), lambda i: (0, 0))
    return pl.pallas_call(
        _rw_out_kernel,
        grid=(M // tm,),
        in_specs=[tok, tok, tok, tok, par, par],
        out_specs=tok,
        out_shape=jax.ShapeDtypeStruct((M, W), BF16),
        compiler_params=_cparams(("parallel",)),
        name="rw_out",
    )(o_f, o_b, g, bonus, lnw.reshape(1, W), lnb.reshape(1, W))


def _pad_rows(w, n):
    return jnp.pad(w, [(0, 0)] * (w.ndim - 2) + [(0, n - w.shape[-2]), (0, 0)])


def _pad_cols(w, n):
    return jnp.pad(w, [(0, 0)] * (w.ndim - 1) + [(0, n - w.shape[-1])])


def _deinterleave(w):
    n = w.shape[-1] // GQ_HEAD
    w = w.reshape(w.shape[:-1] + (n, GQ_HEAD // 2, 2))
    return jnp.swapaxes(w, -1, -2).reshape(w.shape[:-3] + (n * GQ_HEAD,))


def _in_layout(w, lora):
    rkv = w[..., 0:1536]
    lo = [w[..., 1536 + i * lora: 1536 + (i + 1) * lora] for i in range(4)]
    gd = w[..., 1536 + 4 * lora: 1536 + 4 * lora + 256]
    o = 1536 + 4 * lora + 256
    zq = w[..., o:o + 1536]
    zn = w[..., o + 1536:o + 3072]
    zg = w[..., o + 3072:]
    zq = jnp.concatenate([_deinterleave(zq[..., :GQ_WIDTH + GQ_KV_WIDTH]), zq[..., GQ_WIDTH + GQ_KV_WIDTH:]], axis=-1)
    return jnp.concatenate([zg, rkv, zq, zn] + [_pad_cols(x, LORA_PAD) for x in lo] + [gd], axis=-1)


def _shift_layout(mu, lora):
    lo = [_pad_cols(mu[..., 1536 + i * lora: 1536 + (i + 1) * lora], LORA_PAD) for i in range(4)]
    return mu[..., 0:1536], jnp.concatenate(lo + [mu[..., 1536 + 4 * lora:]], axis=-1)


def _tile(n, pref):
    t = pref
    while n % t:
        t //= 2
    return t


def _axial_tables(ctx_len, seq):
    n_freq = GQ_HEAD // 4
    inv = ROPE_THETA ** (-jnp.arange(n_freq, dtype=F32) / n_freq)
    t = jnp.arange(seq, dtype=jnp.int32)
    row = (t // GRID_W).astype(F32)
    col = (t % GRID_W).astype(F32)
    ang = jnp.concatenate([row[:, None] * inv, col[:, None] * inv], axis=-1)
    cos, sin = jnp.cos(ang), jnp.sin(ang)
    cosf = jnp.concatenate([jnp.ones((ctx_len, GQ_HEAD), F32), jnp.concatenate([cos, cos], axis=-1)], axis=0)
    sinf = jnp.concatenate([jnp.zeros((ctx_len, GQ_HEAD), F32), jnp.concatenate([-sin, sin], axis=-1)], axis=0)
    return cosf, sinf


def kernel(x, c, ctx, c_ctx, ada_w, ada_b, norm1, norm2, w_in, rw_mu_prev, rw_mu_next, rw_w0, rw_w_up, rw_a0, rw_a_up, rw_g_up, rw_k_k, rw_k_a, rw_r_k, rw_ln_w, rw_ln_b, gq_q_norm, gq_k_norm, na_rpb, w_br_a, w_br_b, w_br_c, w_out, ffn_w1, ffn_w3, ffn_w2, final_norm):
    B, S, D = x.shape
    C = ctx.shape[1]
    L = ada_w.shape[0]
    T = C + S
    M = B * T
    gps = T // GRP
    cgr = C // GRP
    lora = rw_w_up.shape[2]
    assert C % GRP == 0 and S % GRP == 0 and S % GRID_W == 0
    assert w_in.shape[2] == 1536 + 4 * lora + 256 + 1536 + 1536 + 3 * D
    tm = _tile(M, 1024)
    tm_in = next(t for t in (2176, 1024, 512, 256) if M % t == 0)

    w_in_p = _in_layout(w_in, lora).astype(BF16)
    mu_p, lmu_p = _shift_layout(rw_mu_prev, lora)
    mu_n, lmu_n = _shift_layout(rw_mu_next, lora)
    w_up = _pad_rows(rw_w_up, LORA_PAD)
    a_up = _pad_rows(rw_a_up, LORA_PAD)
    gq = _deinterleave(gq_q_norm)
    gk = _deinterleave(gq_k_norm)
    wa = w_br_a.astype(BF16)
    wb = w_br_b.astype(BF16)
    wc = w_br_c.astype(BF16)
    wo = w_out.astype(BF16)
    w1 = ffn_w1.astype(BF16)
    w3 = ffn_w3.astype(BF16)
    w2 = ffn_w2.astype(BF16)
    cosf, sinf = _axial_tables(C, S)
    na_bias = _na_bias_table(na_rpb, S // GRID_W)

    cc = jnp.concatenate([c, c_ctx[None, :], jnp.zeros((-(B + 1) % 8, D), F32)], axis=0)
    mod = _ada_mod(cc, ada_w, ada_b)
    modg = jnp.concatenate([jnp.broadcast_to(mod[:, B, None, None, :], (L, B, cgr, 6 * D)),
                            jnp.broadcast_to(mod[:, :B, None, :], (L, B, gps - cgr, 6 * D))], axis=2)
    modg = modg.reshape(L, B * gps, 1, 6 * D)

    xt = jnp.concatenate([ctx, x], axis=1).reshape(M, D)

    for l in range(L):
        mg = modg[l]
        h = _modnorm(xt, norm1[l], mg, 0, 1)
        z = _matmul(h, w_in_p[l], tm_in, 768, F32, "in_proj")

        pr = {'mu_p': mu_p[l][None], 'mu_n': mu_n[l][None], 'lmu_p': lmu_p[l][None], 'lmu_n': lmu_n[l][None],
              'k_k': rw_k_k[l][None], 'k_a': rw_k_a[l][None], 'r_k': rw_r_k[l].reshape(1, RW_WIDTH),
              'w0': rw_w0[l][:, None, :], 'w_up': w_up[l], 'a0': rw_a0[l][:, None, :], 'a_up': a_up[l],
              'g_up': rw_g_up[l]}
        v_, g_, bonus_, ra_, at_, bt_, kt_, pc_ = _rw_prep(z, pr, B, gps, cgr)
        of_, ob_ = _rw_scan(ra_, at_, bt_, kt_, v_, pc_, B, C)
        ya = _rw_out(of_.reshape(M, RW_WIDTH), ob_.reshape(M, RW_WIDTH), g_, bonus_, rw_ln_w[l], rw_ln_b[l])

        q, k, v = _gqa_prep(z, cosf, sinf, gq[l], gk[l], gps)
        yb = _gqa_attn(q.reshape(B, T, GQ_WIDTH), k.reshape(B, T, GQ_KV_WIDTH), v.reshape(B, T, 2 * GQ_KV_WIDTH), C)
        yb = yb.reshape(M, GQ_WIDTH)

        yc = _na_attn(z.reshape(B, T, Z_W), na_bias[l], C).reshape(M, NA_WIDTH)

        m = _merge(ya, yb, yc, wa[l], wb[l], wc[l], z, tm)
        xt = _matmul_res(m, wo[l], xt, mg, 2, tm, 512, "out_proj")
        h = _modnorm(xt, norm2[l], mg, 3, 4)
        u = _ffn_up(h, w1[l], w3[l], tm, 512)
        xt = _matmul_res(u, w2[l], xt, mg, 5, tm, 512, "ffn_down")

    return _final_norm(xt.reshape(B, T, D), final_norm, C)
```

```python
import functools
import math

import numpy as np
import jax
import jax.numpy as jnp
from jax import lax
from jax.experimental import pallas as pl
from jax.experimental.pallas import tpu as pltpu

F32 = jnp.float32
BF16 = jnp.bfloat16

GRP = 256
GRID_W = 64
NORM_EPS = 1e-6

RW_HEAD = 64
RW_HEADS = 8
RW_WIDTH = RW_HEAD * RW_HEADS
LORA_PAD = 128
RW_GN_EPS = 64e-5
SCAN_CHUNK = 64

GQ_HEAD = 128
GQ_HEADS = 8
GQ_KV_HEADS = 2
GQ_GROUP = GQ_HEADS // GQ_KV_HEADS
GQ_WIDTH = GQ_HEAD * GQ_HEADS
GQ_KV_WIDTH = GQ_HEAD * GQ_KV_HEADS
ROPE_THETA = 10000.0

NA_HEAD = 64
NA_HEADS = 8
NA_WIDTH = NA_HEAD * NA_HEADS
NA_WIN_ROWS = 8
NA_WIN_COLS = 16
NEG_BIG = -1e30

VMEM_LIMIT = 56 * 1024 * 1024

ZG_OFF, ZG_W = 0, 6144
ZRKV_OFF = 6144
ZQ_OFF = ZRKV_OFF + 1536
ZN_OFF = ZQ_OFF + 1536
ZL_OFF = ZN_OFF + 1536
ZL_W = 4 * LORA_PAD + 256
Z_W = ZL_OFF + ZL_W


def _cparams(sem):
    return pltpu.CompilerParams(dimension_semantics=sem, vmem_limit_bytes=VMEM_LIMIT)


def _dot(a, b, dims=None, precision=None):
    if dims is None:
        dims = (((a.ndim - 1,), (0,)), ((), ()))
    return lax.dot_general(a, b, dims, precision=precision, preferred_element_type=F32)


def _dot_nt(a, b, precision=None):
    return _dot(a, b, (((1,), (1,)), ((), ())), precision)


def _split3(x):
    x1 = x.astype(BF16)
    r1 = x - x1.astype(F32)
    x2 = r1.astype(BF16)
    x3 = (r1 - x2.astype(F32)).astype(BF16)
    return x1, x2, x3


def _ada_kernel(c_ref, w_ref, b_ref, o_ref):
    c = c_ref[...]
    a = (c * jax.nn.sigmoid(c)).astype(BF16)
    o_ref[0] = _dot(a, w_ref[0].astype(BF16)) + b_ref[0]


def _ada_mod(cc, ada_w, ada_b):
    L, D, N = ada_w.shape
    R = cc.shape[0]
    tn = 1536
    return pl.pallas_call(
        _ada_kernel,
        grid=(L, N // tn),
        in_specs=[pl.BlockSpec((R, D), lambda l, j: (0, 0)),
                  pl.BlockSpec((1, D, tn), lambda l, j: (l, 0, j)),
                  pl.BlockSpec((1, 1, tn), lambda l, j: (l, 0, j))],
        out_specs=pl.BlockSpec((1, R, tn), lambda l, j: (l, 0, j)),
        out_shape=jax.ShapeDtypeStruct((L, R, N), F32),
        compiler_params=_cparams(("parallel", "parallel")),
        name="ada_mod",
    )(cc, ada_w, ada_b.reshape(L, 1, N))


def _modnorm_kernel(x_ref, nw_ref, sh_ref, sc_ref, h_ref, *, groups):
    for g in range(groups):
        rows = pl.ds(g * GRP, GRP)
        x = x_ref[rows, :]
        y = x * lax.rsqrt(jnp.mean(x * x, axis=-1, keepdims=True) + NORM_EPS) * nw_ref[...]
        h_ref[rows, :] = (y * (1.0 + sc_ref[g]) + sh_ref[g]).astype(BF16)


def _modnorm(x, nw, modg, sh_blk, sc_blk):
    M, D = x.shape
    tm = _tile(M, 512)
    groups = tm // GRP
    return pl.pallas_call(
        functools.partial(_modnorm_kernel, groups=groups),
        grid=(M // tm,),
        in_specs=[pl.BlockSpec((tm, D), lambda i: (i, 0)),
                  pl.BlockSpec((1, D), lambda i: (0, 0)),
                  pl.BlockSpec((groups, 1, D), lambda i: (i, 0, sh_blk)),
                  pl.BlockSpec((groups, 1, D), lambda i: (i, 0, sc_blk))],
        out_specs=pl.BlockSpec((tm, D), lambda i: (i, 0)),
        out_shape=jax.ShapeDtypeStruct((M, D), BF16),
        compiler_params=_cparams(("parallel",)),
        name="modnorm",
    )(x, nw.reshape(1, D), modg, modg)


def _mm_kernel(a_ref, b_ref, o_ref):
    o_ref[...] = _dot(a_ref[...], b_ref[...]).astype(o_ref.dtype)


def _matmul(a, b, tm, tn, out_dtype, name):
    M, K = a.shape
    N = b.shape[1]
    return pl.pallas_call(
        _mm_kernel,
        grid=(M // tm, N // tn),
        in_specs=[pl.BlockSpec((tm, K), lambda i, j: (i, 0)),
                  pl.BlockSpec((K, tn), lambda i, j: (0, j))],
        out_specs=pl.BlockSpec((tm, tn), lambda i, j: (i, j)),
        out_shape=jax.ShapeDtypeStruct((M, N), out_dtype),
        compiler_params=_cparams(("parallel", "parallel")),
        name=name,
    )(a, b)


def _mm_res_kernel(a_ref, b_ref, res_ref, gate_ref, o_ref, *, groups):
    acc = _dot(a_ref[...], b_ref[...])
    for g in range(groups):
        rows = slice(g * GRP, (g + 1) * GRP)
        o_ref[rows, :] = res_ref[rows, :] + gate_ref[g] * acc[rows, :]


def _matmul_res(a, b, res, modg, gate_blk, tm, tn, name):
    M, K = a.shape
    N = b.shape[1]
    groups = tm // GRP
    nb = N // tn
    return pl.pallas_call(
        functools.partial(_mm_res_kernel, groups=groups),
        grid=(M // tm, nb),
        in_specs=[pl.BlockSpec((tm, K), lambda i, j: (i, 0)),
                  pl.BlockSpec((K, tn), lambda i, j: (0, j)),
                  pl.BlockSpec((tm, tn), lambda i, j: (i, j)),
                  pl.BlockSpec((groups, 1, tn), lambda i, j: (i, 0, gate_blk * nb + j))],
        out_specs=pl.BlockSpec((tm, tn), lambda i, j: (i, j)),
        out_shape=jax.ShapeDtypeStruct((M, N), F32),
        compiler_params=_cparams(("parallel", "parallel")),
        name=name,
    )(a, b, res, modg)


def _ffn_up_kernel(h_ref, w1_ref, w3_ref, o_ref):
    h = h_ref[...]
    a = _dot(h, w1_ref[...])
    b = _dot(h, w3_ref[...])
    o_ref[...] = (a * jax.nn.sigmoid(a) * b).astype(o_ref.dtype)


def _ffn_up(h, w1, w3, tm, tn):
    M, K = h.shape
    N = w1.shape[1]
    return pl.pallas_call(
        _ffn_up_kernel,
        grid=(M // tm, N // tn),
        in_specs=[pl.BlockSpec((tm, K), lambda i, j: (i, 0)),
                  pl.BlockSpec((K, tn), lambda i, j: (0, j)),
                  pl.BlockSpec((K, tn), lambda i, j: (0, j))],
        out_specs=pl.BlockSpec((tm, tn), lambda i, j: (i, j)),
        out_shape=jax.ShapeDtypeStruct((M, N), BF16),
        compiler_params=_cparams(("parallel", "parallel")),
        name="ffn_up",
    )(h, w1, w3)


def _merge_kernel(ya_ref, yb_ref, yc_ref, wa_ref, wb_ref, wc_ref, ga_ref, gb_ref, gc_ref, o_ref):
    def sig(x):
        return 0.5 * jnp.tanh(0.5 * x) + 0.5

    m = sig(ga_ref[...].astype(F32)) * _dot(ya_ref[...], wa_ref[...])
    m = m + sig(gb_ref[...].astype(F32)) * _dot(yb_ref[...], wb_ref[...])
    m = m + sig(gc_ref[...].astype(F32)) * _dot(yc_ref[...], wc_ref[...])
    o_ref[...] = m.astype(o_ref.dtype)


def _merge(ya, yb, yc, wa, wb, wc, z, tm, tn=512):
    M = ya.shape[0]
    D = wa.shape[1]
    nb = D // tn
    g0 = ZG_OFF // tn
    return pl.pallas_call(
        _merge_kernel,
        grid=(M // tm, nb),
        in_specs=[pl.BlockSpec((tm, ya.shape[1]), lambda i, j: (i, 0)),
                  pl.BlockSpec((tm, yb.shape[1]), lambda i, j: (i, 0)),
                  pl.BlockSpec((tm, yc.shape[1]), lambda i, j: (i, 0)),
                  pl.BlockSpec((wa.shape[0], tn), lambda i, j: (0, j)),
                  pl.BlockSpec((wb.shape[0], tn), lambda i, j: (0, j)),
                  pl.BlockSpec((wc.shape[0], tn), lambda i, j: (0, j)),
                  pl.BlockSpec((tm, tn), lambda i, j: (i, g0 + j)),
                  pl.BlockSpec((tm, tn), lambda i, j: (i, g0 + nb + j)),
                  pl.BlockSpec((tm, tn), lambda i, j: (i, g0 + 2 * nb + j))],
        out_specs=pl.BlockSpec((tm, tn), lambda i, j: (i, j)),
        out_shape=jax.ShapeDtypeStruct((M, D), BF16),
        compiler_params=_cparams(("parallel", "parallel")),
        name="merge",
    )(ya, yb, yc, wa, wb, wc, z, z, z)


def _final_norm_kernel(x_ref, w_ref, o_ref):
    x = x_ref[0]
    o_ref[0] = x * lax.rsqrt(jnp.mean(x * x, axis=-1, keepdims=True) + NORM_EPS) * w_ref[...]


def _final_norm(xt, w, ctx_len):
    B, T, D = xt.shape
    S = T - ctx_len
    cg = ctx_len // GRP
    return pl.pallas_call(
        _final_norm_kernel,
        grid=(B, S // GRP),
        in_specs=[pl.BlockSpec((1, GRP, D), lambda b, i: (b, cg + i, 0)),
                  pl.BlockSpec((1, D), lambda b, i: (0, 0))],
        out_specs=pl.BlockSpec((1, GRP, D), lambda b, i: (b, i, 0)),
        out_shape=jax.ShapeDtypeStruct((B, S, D), F32),
        compiler_params=_cparams(("parallel", "parallel")),
        name="final_norm",
    )(xt, w.reshape(1, D))


def _gqa_prep_kernel(z_ref, cos_ref, sin_ref, gq_ref, gk_ref, q_ref, k_ref, v_ref):
    cosf = cos_ref[...]
    sinf = sin_ref[...]

    def norm_rope(x, gain):
        x = x.astype(F32)
        y = x * lax.rsqrt(jnp.mean(x * x, axis=-1, keepdims=True) + NORM_EPS) * gain
        return y * cosf + pltpu.roll(y, GQ_HEAD // 2, axis=1) * sinf

    scale = GQ_HEAD ** -0.5 * math.log2(math.e)
    for h in range(GQ_HEADS):
        cols = slice(h * GQ_HEAD, (h + 1) * GQ_HEAD)
        q_ref[:, cols] = (norm_rope(z_ref[:, cols], gq_ref[...]) * scale).astype(BF16)
    for h in range(GQ_KV_HEADS):
        cols = slice(h * GQ_HEAD, (h + 1) * GQ_HEAD)
        zc = slice(GQ_WIDTH + h * GQ_HEAD, GQ_WIDTH + (h + 1) * GQ_HEAD)
        k_ref[:, cols] = norm_rope(z_ref[:, zc], gk_ref[...]).astype(BF16)
        vz = slice(GQ_WIDTH + GQ_KV_WIDTH + h * GQ_HEAD, GQ_WIDTH + GQ_KV_WIDTH + (h + 1) * GQ_HEAD)
        v_ref[:, 2 * h * GQ_HEAD:(2 * h + 1) * GQ_HEAD] = z_ref[:, vz].astype(BF16)
        v_ref[:, (2 * h + 1) * GQ_HEAD:(2 * h + 2) * GQ_HEAD] = jnp.ones((GRP, GQ_HEAD), BF16)


def _gqa_prep(z, cosf, sinf, gq, gk, gps):
    M = z.shape[0]
    zw = GQ_WIDTH + 2 * GQ_KV_WIDTH
    return pl.pallas_call(
        _gqa_prep_kernel,
        grid=(M // GRP,),
        in_specs=[pl.BlockSpec((GRP, zw), lambda i: (i, ZQ_OFF // zw)),
                  pl.BlockSpec((GRP, GQ_HEAD), lambda i: (i % gps, 0)),
                  pl.BlockSpec((GRP, GQ_HEAD), lambda i: (i % gps, 0)),
                  pl.BlockSpec((1, GQ_HEAD), lambda i: (0, 0)),
                  pl.BlockSpec((1, GQ_HEAD), lambda i: (0, 0))],
        out_specs=[pl.BlockSpec((GRP, GQ_WIDTH), lambda i: (i, 0)),
                   pl.BlockSpec((GRP, GQ_KV_WIDTH), lambda i: (i, 0)),
                   pl.BlockSpec((GRP, 2 * GQ_KV_WIDTH), lambda i: (i, 0))],
        out_shape=[jax.ShapeDtypeStruct((M, GQ_WIDTH), BF16),
                   jax.ShapeDtypeStruct((M, GQ_KV_WIDTH), BF16),
                   jax.ShapeDtypeStruct((M, 2 * GQ_KV_WIDTH), BF16)],
        compiler_params=_cparams(("parallel",)),
        name="gqa_prep",
    )(z, cosf, sinf, gq.reshape(1, GQ_HEAD), gk.reshape(1, GQ_HEAD))


def _gqa_attn_kernel(q_ref, k_ref, v_ref, o_ref, *, ctx_tiles, ctx_len, n_keys):
    qi = pl.program_id(2)

    def attend(nk):
        k = k_ref[0, 0:nk, :]
        v = v_ref[0, 0:nk, :]
        for h in range(GQ_GROUP):
            cols = slice(h * GQ_HEAD, (h + 1) * GQ_HEAD)
            s = _dot_nt(q_ref[0, :, cols], k)
            p = jnp.exp2(s - jnp.max(s, axis=-1, keepdims=True))
            ol = _dot(p.astype(BF16), v)
            o = ol[:, :GQ_HEAD] / ol[:, GQ_HEAD:GQ_HEAD + 1]
            o_ref[0, :, cols] = o.astype(o_ref.dtype)

    @pl.when(qi < ctx_tiles)
    def _():
        attend(ctx_len)

    @pl.when(qi >= ctx_tiles)
    def _():
        attend(n_keys)


def _gqa_attn(q, k, v, ctx_len, tq=256):
    B, T, _ = q.shape
    gw = GQ_GROUP * GQ_HEAD
    kern = functools.partial(_gqa_attn_kernel, ctx_tiles=ctx_len // tq, ctx_len=ctx_len, n_keys=T)
    return pl.pallas_call(
        kern,
        grid=(B, GQ_KV_HEADS, T // tq),
        in_specs=[pl.BlockSpec((1, tq, gw), lambda b, g, i: (b, i, g)),
                  pl.BlockSpec((1, T, GQ_HEAD), lambda b, g, i: (b, 0, g)),
                  pl.BlockSpec((1, T, 2 * GQ_HEAD), lambda b, g, i: (b, 0, g))],
        out_specs=pl.BlockSpec((1, tq, gw), lambda b, g, i: (b, i, g)),
        out_shape=jax.ShapeDtypeStruct((B, T, GQ_WIDTH), BF16),
        compiler_params=_cparams(("parallel", "parallel", "arbitrary")),
        name="gqa_attn",
    )(q, k, v)


def _na_kernel(q_ref, k_ref, v_ref, bias_ref, o_ref, qs_ref, ks_ref, vs_ref, *, ctx_len, rows):
    scale = NA_HEAD ** -0.5
    wr = min(NA_WIN_ROWS, rows)
    nwin = wr * GRID_W
    rpg = math.gcd(rows, 8)

    for hh in range(2):
        cols = slice(hh * NA_HEAD, (hh + 1) * NA_HEAD)
        qs_ref[hh] = (q_ref[0, :, cols] * scale).astype(BF16)
        ks_ref[hh] = k_ref[0, :, cols].astype(BF16)
        vs_ref[hh] = v_ref[0, :, cols].astype(BF16)

    for hh in range(2):
        cols = slice(hh * NA_HEAD, (hh + 1) * NA_HEAD)
        kc = ks_ref[hh, 0:ctx_len, :]
        vc = vs_ref[hh, 0:ctx_len, :]

        s = _dot_nt(qs_ref[hh, 0:ctx_len, :], kc)
        p = jnp.exp(s - jnp.max(s, axis=-1, keepdims=True))
        o = _dot(p.astype(BF16), vc) / jnp.sum(p, axis=-1, keepdims=True)
        o_ref[0, 0:ctx_len, cols] = o.astype(o_ref.dtype)

        def group_body(g, carry):
            i0 = g * rpg
            qoff = pl.multiple_of(ctx_len + i0 * GRID_W, rpg * GRID_W)
            qg = qs_ref[hh, pl.ds(qoff, rpg * GRID_W), :]
            s2 = _dot_nt(qg, kc)
            koffs, s1 = [], []
            for r in range(rpg):
                rs = jnp.clip(i0 + r - wr // 2, 0, rows - wr)
                koffs.append(pl.multiple_of(ctx_len + rs * GRID_W, GRID_W))
                kw = ks_ref[hh, pl.ds(koffs[r], nwin), :]
                s1.append(_dot_nt(qg[r * GRID_W:(r + 1) * GRID_W], kw) + bias_ref[hh, i0 + r - rs])
            s1 = jnp.concatenate(s1, axis=0)
            m = jnp.maximum(jnp.max(s1, axis=-1, keepdims=True), jnp.max(s2, axis=-1, keepdims=True))
            p1 = jnp.exp(s1 - m)
            p2 = jnp.exp(s2 - m)
            l = jnp.sum(p1, axis=-1, keepdims=True) + jnp.sum(p2, axis=-1, keepdims=True)
            p1 = p1.astype(BF16)
            o1 = [_dot(p1[r * GRID_W:(r + 1) * GRID_W], vs_ref[hh, pl.ds(koffs[r], nwin), :]) for r in range(rpg)]
            o = (jnp.concatenate(o1, axis=0) + _dot(p2.astype(BF16), vc)) / l
            o_ref[0, pl.ds(qoff, rpg * GRID_W), cols] = o.astype(o_ref.dtype)
            return carry

        lax.fori_loop(0, rows // rpg, group_body, 0, unroll=math.gcd(rows // rpg, 4))


def _na_bias_table(rpb, rows):
    wr = min(NA_WIN_ROWS, rows)
    cols = np.arange(GRID_W)
    cstart = np.clip(cols - NA_WIN_COLS // 2, 0, GRID_W - NA_WIN_COLS)
    c = np.arange(GRID_W)
    inside = (c[None, :] >= cstart[:, None]) & (c[None, :] < cstart[:, None] + NA_WIN_COLS)
    col_off = np.clip(c[None, :] - cols[:, None] + (NA_WIN_COLS - 1), 0, 2 * NA_WIN_COLS - 2)
    onehot = (np.arange(2 * NA_WIN_COLS - 1)[:, None, None] == col_off[None]).astype(np.float32)
    tc = jnp.einsum('...ro,ojc->...rjc', rpb, onehot, precision=lax.Precision.HIGHEST)
    tc = jnp.where(inside, tc, NEG_BIG)
    t = jnp.stack([tc[..., NA_WIN_ROWS - 1 - d:NA_WIN_ROWS - 1 - d + wr, :, :] for d in range(wr)], axis=-4)
    t = jnp.swapaxes(t, -3, -2)
    return t.reshape(rpb.shape[:-2] + (wr, GRID_W, wr * GRID_W))


def _na_attn(z, bias, ctx_len):
    B, T, _ = z.shape
    rows = (T - ctx_len) // GRID_W
    wr = bias.shape[1]
    pw = 2 * NA_HEAD
    c0 = ZN_OFF // pw
    per = NA_WIDTH // pw
    scr = pltpu.VMEM((2, T, NA_HEAD), BF16)
    return pl.pallas_call(
        functools.partial(_na_kernel, ctx_len=ctx_len, rows=rows),
        grid=(B, per),
        in_specs=[pl.BlockSpec((1, T, pw), lambda b, h: (b, 0, c0 + h)),
                  pl.BlockSpec((1, T, pw), lambda b, h: (b, 0, c0 + per + h)),
                  pl.BlockSpec((1, T, pw), lambda b, h: (b, 0, c0 + 2 * per + h)),
                  pl.BlockSpec((2, wr, GRID_W, wr * GRID_W), lambda b, h: (h, 0, 0, 0))],
        out_specs=pl.BlockSpec((1, T, pw), lambda b, h: (b, 0, h)),
        out_shape=jax.ShapeDtypeStruct((B, T, NA_WIDTH), BF16),
        scratch_shapes=[scr, scr, scr],
        compiler_params=_cparams(("parallel", "parallel")),
        name="na_attn",
    )(z, z, z, bias)


RW_PAIRS = RW_HEADS // 2
PAIR_W = 2 * RW_HEAD
PC_ROWS = 8
HALO = 16


def _head0_lanes(shape):
    return lax.broadcasted_iota(jnp.int32, shape, len(shape) - 1) < RW_HEAD


def _per_head_rows(x):
    m0 = _head0_lanes(x.shape)
    return jnp.concatenate([jnp.where(m0, x, 0.0), jnp.where(m0, 0.0, x)], axis=-2)


def _pair_diag(full):
    half = full.shape[-2] // 2
    top, bot = full[..., :half, :], full[..., half:, :]
    return jnp.where(_head0_lanes(top.shape), top, bot)


def _rw_prep_kernel(z_ref, zp_ref, zn_ref, l_ref, lp_ref, ln_ref,
                    mup_ref, mun_ref, lmup_ref, lmun_ref, kk_ref_, ka_ref, rk_ref, w0_ref, wup_ref, a0_ref,
                    aup_ref, gup_ref,
                    v_out, g_out, bonus_out, ra_out, at_out, bt_out, kt_out, pc_out, *, gps, ctx_groups):
    t = pl.program_id(1)
    first = jnp.logical_or(t == 0, t == ctx_groups)
    last = jnp.logical_or(t == ctx_groups - 1, t == gps - 1)
    pv = jnp.where(first, 0.0, 1.0)
    nv = jnp.where(last, 0.0, 1.0)
    row = lax.broadcasted_iota(jnp.int32, (GRP, 1), 0)
    C = SCAN_CHUNK

    def shift(x_ref, p_ref, n_ref, mp_ref, mn_ref):
        x = x_ref[...].astype(F32)
        xp = jnp.where(row == 0, p_ref[HALO - 1:HALO, :].astype(F32) * pv, pltpu.roll(x, 1, axis=0))
        xn = jnp.where(row == GRP - 1, n_ref[0:1, :].astype(F32) * nv, pltpu.roll(x, GRP - 1, axis=0))
        return x + mp_ref[...] * (xp - x) + mn_ref[...] * (xn - x)

    zs = shift(z_ref, zp_ref, zn_ref, mup_ref, mun_ref)
    ls = shift(l_ref, lp_ref, ln_ref, lmup_ref, lmun_ref)
    r = zs[:, 0:RW_WIDTH]
    k = zs[:, RW_WIDTH:2 * RW_WIDTH]
    v = zs[:, 2 * RW_WIDTH:3 * RW_WIDTH]

    ci = lax.broadcasted_iota(jnp.int32, (PAIR_W, PAIR_W), 0) // RW_HEAD
    cj = lax.broadcasted_iota(jnp.int32, (PAIR_W, PAIR_W), 1) // RW_HEAD
    head_ones = jnp.where(ci == cj, 1.0, 0.0).astype(BF16)

    def head_sum(x):
        parts = _split3(x)
        return jnp.concatenate([sum(_dot(part[:, p * PAIR_W:(p + 1) * PAIR_W], head_ones) for part in parts)
                                for p in range(RW_PAIRS)], axis=1)

    def dot_f32(x, w):
        x1, x2, _ = _split3(x)
        w1, w2, _ = _split3(w)
        return _dot(x1, w1) + (_dot(x2, w1) + _dot(x1, w2))

    kk = k * kk_ref_[...]
    kk = kk / jnp.maximum(jnp.sqrt(head_sum(kk * kk)), 1e-12)

    ti = lax.broadcasted_iota(jnp.int32, (GRP, GRP), 0)
    tj = lax.broadcasted_iota(jnp.int32, (GRP, GRP), 1)
    same = (ti // C) == (tj // C)

    v_out[...] = v.astype(BF16)
    gd = ls[:, 4 * LORA_PAD:]
    g_out[...] = _dot(jax.nn.sigmoid(gd).astype(BF16), gup_ref[...].astype(BF16))

    kd_sum = jnp.zeros_like(k)
    for d in range(2):
        wl = ls[:, d * LORA_PAD:(d + 1) * LORA_PAD]
        al = ls[:, (2 + d) * LORA_PAD:(3 + d) * LORA_PAD]
        w_raw = w0_ref[d] + dot_f32(jnp.tanh(wl), wup_ref[d])
        lw = -math.exp(-0.5) * jax.nn.sigmoid(w_raw)
        a = jax.nn.sigmoid(a0_ref[d] + _dot(al.astype(BF16), aup_ref[d].astype(BF16)))
        kd = k * (1.0 + (a - 1.0) * ka_ref[...])
        kd_sum = kd_sum + kd
        tri = jnp.where(jnp.logical_and(same, (ti >= tj) if d == 0 else (ti <= tj)), 1.0, 0.0).astype(BF16)
        cs = sum(_dot(tri, part) for part in _split3(lw))
        pinv = jnp.exp(-cs)
        ra_out[d] = (r * jnp.exp(cs)).astype(BF16)
        at_out[d] = (-kk * jnp.exp(cs - lw)).astype(BF16)
        bt_out[d] = (kk * a * pinv).astype(BF16)
        kt_out[d] = (kd * pinv).astype(BF16)
        for c in range(GRP // C):
            end = c * C + (C - 1 if d == 0 else 0)
            pc_out[d, c * PC_ROWS:(c + 1) * PC_ROWS, :] = jnp.broadcast_to(jnp.exp(cs[end:end + 1, :]),
                                                                          (PC_ROWS, RW_WIDTH))
    bonus_out[...] = head_sum(r * kd_sum * rk_ref[...]) * v


def _rw_prep(z, p, B, gps, ctx_groups):
    M = z.shape[0]
    nb8 = M // HALO
    rb = GRP // HALO
    zc = ZRKV_OFF // 1536
    lc = ZL_OFF // ZL_W
    pcr = GRP // SCAN_CHUNK * PC_ROWS

    def cur(c):
        return lambda b, t: (b * gps + t, c)

    def prev(c):
        return lambda b, t: (jnp.maximum((b * gps + t) * rb - 1, 0), c)

    def nxt(c):
        return lambda b, t: (jnp.minimum((b * gps + t + 1) * rb, nb8 - 1), c)

    def const(shape):
        nd = len(shape)
        return pl.BlockSpec(shape, lambda b, t: (0,) * nd)

    tok = pl.BlockSpec((GRP, RW_WIDTH), lambda b, t: (b * gps + t, 0))
    tok2 = pl.BlockSpec((2, GRP, RW_WIDTH), lambda b, t: (0, b * gps + t, 0))
    s1 = jax.ShapeDtypeStruct((M, RW_WIDTH), F32)
    s2 = jax.ShapeDtypeStruct((2, M, RW_WIDTH), BF16)
    return pl.pallas_call(
        functools.partial(_rw_prep_kernel, gps=gps, ctx_groups=ctx_groups),
        grid=(B, gps),
        in_specs=[pl.BlockSpec((GRP, 1536), cur(zc)),
                  pl.BlockSpec((HALO, 1536), prev(zc)),
                  pl.BlockSpec((HALO, 1536), nxt(zc)),
                  pl.BlockSpec((GRP, ZL_W), cur(lc)),
                  pl.BlockSpec((HALO, ZL_W), prev(lc)),
                  pl.BlockSpec((HALO, ZL_W), nxt(lc)),
                  const((1, 1536)), const((1, 1536)), const((1, ZL_W)), const((1, ZL_W)),
                  const((1, RW_WIDTH)), const((1, RW_WIDTH)), const((1, RW_WIDTH)),
                  const((2, 1, RW_WIDTH)), const((2, LORA_PAD, RW_WIDTH)),
                  const((2, 1, RW_WIDTH)), const((2, LORA_PAD, RW_WIDTH)),
                  const((256, RW_WIDTH))],
        out_specs=[tok, tok, tok, tok2, tok2, tok2, tok2,
                   pl.BlockSpec((2, pcr, RW_WIDTH), lambda b, t: (0, b * gps + t, 0))],
        out_shape=[jax.ShapeDtypeStruct((M, RW_WIDTH), BF16), s1, s1, s2, s2, s2, s2,
                   jax.ShapeDtypeStruct((2, M // GRP * pcr, RW_WIDTH), F32)],
        compiler_params=_cparams(("parallel", "parallel")),
        name="rw_prep",
    )(z, z, z, z, z, z, p['mu_p'], p['mu_n'], p['lmu_p'], p['lmu_n'], p['k_k'], p['k_a'], p['r_k'],
      p['w0'], p['w_up'], p['a0'], p['a_up'], p['g_up'])


def _bmm(a, b, nt=False):
    a = a.astype(BF16)
    b = b.astype(BF16)
    out = []
    for h in range(a.shape[0]):
        out.append(_dot_nt(a[h], b[h]) if nt else _dot(a[h], b[h]))
    return jnp.stack(out, axis=0)


def _bmm_tn(a, b):
    a = a.astype(BF16)
    b = b.astype(BF16)
    out = []
    for h in range(a.shape[0]):
        out.append(_dot(a[h], b[h], (((0,), (0,)), ((), ()))))
    return jnp.stack(out, axis=0)


def _rw_intra_kernel(ra_ref, at_ref, bt_ref, kt_ref, v_ref, pc_ref, rh_ref, o0_ref, gm_ref, sd_ref, *, nck):
    d = pl.program_id(1)
    C, P = SCAN_CHUNK, RW_PAIRS

    def pairs(x):
        x = x.reshape(nck, x.shape[0] // nck, RW_WIDTH)
        return jnp.concatenate([x[:, :, p * PAIR_W:(p + 1) * PAIR_W] for p in range(P)], axis=0)

    def unpairs(y):
        rows = y.shape[1] * nck
        return jnp.concatenate([y[p * nck:(p + 1) * nck].reshape(rows, PAIR_W) for p in range(P)], axis=1)

    ti = lax.broadcasted_iota(jnp.int32, (C, PAIR_W), 0)
    tj = lax.broadcasted_iota(jnp.int32, (C, PAIR_W), 1) % C
    before = (ti - tj) * jnp.where(d == 0, 1, -1)
    strict = (before > 0)[None]
    incl = (before >= 0)[None]
    eye = (ti == tj)[None]

    ra = pairs(ra_ref[0])
    at = pairs(at_ref[0])
    bt = pairs(bt_ref[0])
    kt = pairs(kt_ref[0])
    v = pairs(v_ref[...])
    pc = pairs(pc_ref[0])[:, 0:1, :]

    lhs = jnp.concatenate([at, ra], axis=1)
    rhs = jnp.concatenate([_per_head_rows(bt), _per_head_rows(kt)], axis=1)
    gmat = _bmm(lhs, rhs, nt=True)
    a_ab = jnp.where(strict, gmat[:, :C, :PAIR_W], 0.0)
    a_ak = jnp.where(strict, gmat[:, :C, PAIR_W:], 0.0)
    a_rb = jnp.where(incl, gmat[:, C:, :PAIR_W], 0.0)
    a_rk = jnp.where(incl, gmat[:, C:, PAIR_W:], 0.0)

    tm = jnp.where(eye, 1.0, 0.0) + a_ab
    pw = a_ab
    for _ in range(int(math.log2(C)) - 1):
        pw = _bmm(pw, _per_head_rows(pw))
        tm = tm + _bmm(tm, _per_head_rows(pw))

    vrows = _per_head_rows(v)
    x0 = _bmm(a_ak, vrows)
    au = _bmm(tm, jnp.concatenate([_per_head_rows(at), _per_head_rows(x0)], axis=2))
    ah, u0 = au[:, :, :PAIR_W], au[:, :, PAIR_W:]
    upper = jnp.concatenate([_per_head_rows(ah), _per_head_rows(u0)], axis=2)
    lower = jnp.concatenate([jnp.zeros_like(vrows), vrows], axis=2)
    ro = _bmm(jnp.concatenate([a_rb, a_rk], axis=2),
              jnp.concatenate([upper, lower], axis=1))
    bh = bt * pc
    kh = kt * pc
    rh_ref[0] = unpairs(ra + ro[:, :, :PAIR_W]).astype(rh_ref.dtype)
    o0_ref[0] = unpairs(ro[:, :, PAIR_W:])
    gm_ref[0] = unpairs(_pair_diag(_bmm_tn(ah, bh))).astype(gm_ref.dtype)
    sd_ref[0] = unpairs(_pair_diag(_bmm_tn(jnp.concatenate([u0, v], axis=1), jnp.concatenate([bh, kh], axis=1))))


def _rw_seq_kernel(rhf_ref, o0f_ref, gmf_ref, sdf_ref, pcf_ref, rhb_ref, o0b_ref, gmb_ref, sdb_ref, pcb_ref,
                   of_ref, ob_ref, s_ref, *, nck):
    C = SCAN_CHUNK

    @pl.when(pl.program_id(0) == 0)
    def _():
        s_ref[...] = jnp.zeros(s_ref.shape, F32)

    dirs = ((rhf_ref, o0f_ref, gmf_ref, sdf_ref, pcf_ref, of_ref, range(nck)),
            (rhb_ref, o0b_ref, gmb_ref, sdb_ref, pcb_ref, ob_ref, range(nck - 1, -1, -1)))
    chains = [(d, b, p) for d in range(2) for b in range(s_ref.shape[1]) for p in range(RW_PAIRS)]
    state = {ch: s_ref[ch] for ch in chains}
    for step in range(nck):
        for ch in chains:
            d, b, p = ch
            rh_ref, o0_ref, gm_ref, sd_ref, pc_ref, o_ref, order = dirs[d]
            c = order[step]
            rows = slice(c * C, (c + 1) * C)
            cols = slice(p * PAIR_W, (p + 1) * PAIR_W)
            s = state[ch]
            sb = s.astype(BF16)
            o_ref[b, rows, cols] = _dot_nt(rh_ref[0, b, rows, cols], _per_head_rows(sb)) + o0_ref[0, b, rows, cols]
            state[ch] = (s * pc_ref[0, b, c * PC_ROWS:c * PC_ROWS + 1, cols]
                         + _dot(sb, _per_head_rows(gm_ref[0, b, rows, cols])) + sd_ref[0, b, rows, cols])
    for ch in chains:
        s_ref[ch] = state[ch]


def _rw_scan(ra, at, bt, kt, v, pc, B, ctx_len, nck=4):
    M = v.shape[0]
    T = M // B
    W = RW_WIDTH
    ct = nck * SCAN_CHUNK
    nblk = T // ct
    ncb = ctx_len // ct
    pcr = nck * PC_ROWS
    assert ctx_len % ct == 0 and T % ct == 0

    per_dir = pl.BlockSpec((1, ct, W), lambda b, d, j: (d, b * nblk + j, 0))
    outs = [jax.ShapeDtypeStruct((2, M, W), BF16), jax.ShapeDtypeStruct((2, M, W), F32),
            jax.ShapeDtypeStruct((2, M, W), BF16), jax.ShapeDtypeStruct((2, M, W), F32)]
    rh, o0, gm, sd = pl.pallas_call(
        functools.partial(_rw_intra_kernel, nck=nck),
        grid=(B, 2, nblk),
        in_specs=[per_dir, per_dir, per_dir, per_dir,
                  pl.BlockSpec((ct, W), lambda b, d, j: (b * nblk + j, 0)),
                  pl.BlockSpec((1, pcr, W), lambda b, d, j: (d, b * nblk + j, 0))],
        out_specs=[per_dir, per_dir, per_dir, per_dir],
        out_shape=outs,
        compiler_params=_cparams(("parallel", "parallel", "parallel")),
        name="rw_intra",
    )(ra, at, bt, kt, v, pc)

    def bwd_blk(j):
        return jnp.where(j < ncb, ncb - 1 - j, nblk - 1 - (j - ncb))

    def view(x, rows):
        return x.reshape(2, B, rows, W)

    f_big = pl.BlockSpec((1, B, ct, W), lambda j: (0, 0, j, 0))
    b_big = pl.BlockSpec((1, B, ct, W), lambda j: (1, 0, bwd_blk(j), 0))
    f_pc = pl.BlockSpec((1, B, pcr, W), lambda j: (0, 0, j, 0))
    b_pc = pl.BlockSpec((1, B, pcr, W), lambda j: (1, 0, bwd_blk(j), 0))
    out = jax.ShapeDtypeStruct((B, T, W), F32)
    big = [view(x, T) for x in (rh, o0, gm, sd)]
    pcv = view(pc, T // SCAN_CHUNK * PC_ROWS)
    return pl.pallas_call(
        functools.partial(_rw_seq_kernel, nck=nck),
        grid=(nblk,),
        in_specs=[f_big, f_big, f_big, f_big, f_pc, b_big, b_big, b_big, b_big, b_pc],
        out_specs=[pl.BlockSpec((B, ct, W), lambda j: (0, j, 0)),
                   pl.BlockSpec((B, ct, W), lambda j: (0, bwd_blk(j), 0))],
        out_shape=[out, out],
        scratch_shapes=[pltpu.VMEM((2, B, RW_PAIRS, RW_HEAD, PAIR_W), F32)],
        compiler_params=_cparams(("arbitrary",)),
        name="rw_seq",
    )(*big, pcv, *big, pcv)


def _rw_out_kernel(of_ref, ob_ref, g_ref, bonus_ref, lnw_ref, lnb_ref, y_ref):
    ci = lax.broadcasted_iota(jnp.int32, (PAIR_W, PAIR_W), 0) // RW_HEAD
    cj = lax.broadcasted_iota(jnp.int32, (PAIR_W, PAIR_W), 1) // RW_HEAD
    head_mean = jnp.where(ci == cj, 1.0 / RW_HEAD, 0.0).astype(BF16)

    def mean(x):
        return sum(_dot(part, head_mean) for part in _split3(x))

    for p in range(RW_PAIRS):
        cols = slice(p * PAIR_W, (p + 1) * PAIR_W)
        wkv = of_ref[:, cols] + ob_ref[:, cols]
        cen = wkv - mean(wkv)
        y = cen * lax.rsqrt(mean(cen * cen) + RW_GN_EPS) * lnw_ref[:, cols] + lnb_ref[:, cols] + bonus_ref[:, cols]
        y_ref[:, cols] = (y * g_ref[:, cols]).astype(y_ref.dtype)


def _rw_out(o_f, o_b, g, bonus, lnw, lnb):
    M, W = g.shape
    tm = _tile(M, 512)
    tok = pl.BlockSpec((tm, W), lambda i: (i, 0))
    par = pl.BlockSpec((1, W), lambda i: (0, 0))
    return pl.pallas_call(
        _rw_out_kernel,
        grid=(M // tm,),
        in_specs=[tok, tok, tok, tok, par, par],
        out_specs=tok,
        out_shape=jax.ShapeDtypeStruct((M, W), BF16),
        compiler_params=_cparams(("parallel",)),
        name="rw_out",
    )(o_f, o_b, g, bonus, lnw.reshape(1, W), lnb.reshape(1, W))


def _pad_rows(w, n):
    return jnp.pad(w, [(0, 0)] * (w.ndim - 2) + [(0, n - w.shape[-2]), (0, 0)])


def _pad_cols(w, n):
    return jnp.pad(w, [(0, 0)] * (w.ndim - 1) + [(0, n - w.shape[-1])])


def _deinterleave(w):
    n = w.shape[-1] // GQ_HEAD
    w = w.reshape(w.shape[:-1] + (n, GQ_HEAD // 2, 2))
    return jnp.swapaxes(w, -1, -2).reshape(w.shape[:-3] + (n * GQ_HEAD,))


def _in_layout(w, lora):
    rkv = w[..., 0:1536]
    lo = [w[..., 1536 + i * lora: 1536 + (i + 1) * lora] for i in range(4)]
    gd = w[..., 1536 + 4 * lora: 1536 + 4 * lora + 256]
    o = 1536 + 4 * lora + 256
    zq = w[..., o:o + 1536]
    zn = w[..., o + 1536:o + 3072]
    zg = w[..., o + 3072:]
    zq = jnp.concatenate([_deinterleave(zq[..., :GQ_WIDTH + GQ_KV_WIDTH]), zq[..., GQ_WIDTH + GQ_KV_WIDTH:]], axis=-1)
    return jnp.concatenate([zg, rkv, zq, zn] + [_pad_cols(x, LORA_PAD) for x in lo] + [gd], axis=-1)


def _shift_layout(mu, lora):
    lo = [_pad_cols(mu[..., 1536 + i * lora: 1536 + (i + 1) * lora], LORA_PAD) for i in range(4)]
    return mu[..., 0:1536], jnp.concatenate(lo + [mu[..., 1536 + 4 * lora:]], axis=-1)


def _tile(n, pref):
    t = pref
    while n % t:
        t //= 2
    return t


def _axial_tables(ctx_len, seq):
    n_freq = GQ_HEAD // 4
    inv = ROPE_THETA ** (-jnp.arange(n_freq, dtype=F32) / n_freq)
    t = jnp.arange(seq, dtype=jnp.int32)
    row = (t // GRID_W).astype(F32)
    col = (t % GRID_W).astype(F32)
    ang = jnp.concatenate([row[:, None] * inv, col[:, None] * inv], axis=-1)
    cos, sin = jnp.cos(ang), jnp.sin(ang)
    cosf = jnp.concatenate([jnp.ones((ctx_len, GQ_HEAD), F32), jnp.concatenate([cos, cos], axis=-1)], axis=0)
    sinf = jnp.concatenate([jnp.zeros((ctx_len, GQ_HEAD), F32), jnp.concatenate([-sin, sin], axis=-1)], axis=0)
    return cosf, sinf


def kernel(x, c, ctx, c_ctx, ada_w, ada_b, norm1, norm2, w_in, rw_mu_prev, rw_mu_next, rw_w0, rw_w_up, rw_a0, rw_a_up, rw_g_up, rw_k_k, rw_k_a, rw_r_k, rw_ln_w, rw_ln_b, gq_q_norm, gq_k_norm, na_rpb, w_br_a, w_br_b, w_br_c, w_out, ffn_w1, ffn_w3, ffn_w2, final_norm):
    B, S, D = x.shape
    C = ctx.shape[1]
    L = ada_w.shape[0]
    T = C + S
    M = B * T
    gps = T // GRP
    cgr = C // GRP
    lora = rw_w_up.shape[2]
    assert C % GRP == 0 and S % GRP == 0 and S % GRID_W == 0
    assert w_in.shape[2] == 1536 + 4 * lora + 256 + 1536 + 1536 + 3 * D
    tm = _tile(M, 1024)
    tm_in = next(t for t in (2176, 1024, 512, 256) if M % t == 0)

    w_in_p = _in_layout(w_in, lora).astype(BF16)
    mu_p, lmu_p = _shift_layout(rw_mu_prev, lora)
    mu_n, lmu_n = _shift_layout(rw_mu_next, lora)
    w_up = _pad_rows(rw_w_up, LORA_PAD)
    a_up = _pad_rows(rw_a_up, LORA_PAD)
    gq = _deinterleave(gq_q_norm)
    gk = _deinterleave(gq_k_norm)
    wa = w_br_a.astype(BF16)
    wb = w_br_b.astype(BF16)
    wc = w_br_c.astype(BF16)
    wo = w_out.astype(BF16)
    w1 = ffn_w1.astype(BF16)
    w3 = ffn_w3.astype(BF16)
    w2 = ffn_w2.astype(BF16)
    cosf, sinf = _axial_tables(C, S)
    na_bias = _na_bias_table(na_rpb, S // GRID_W)

    cc = jnp.concatenate([c, c_ctx[None, :], jnp.zeros((-(B + 1) % 8, D), F32)], axis=0)
    mod = _ada_mod(cc, ada_w, ada_b)
    modg = jnp.concatenate([jnp.broadcast_to(mod[:, B, None, None, :], (L, B, cgr, 6 * D)),
                            jnp.broadcast_to(mod[:, :B, None, :], (L, B, gps - cgr, 6 * D))], axis=2)
    modg = modg.reshape(L, B * gps, 1, 6 * D)

    xt = jnp.concatenate([ctx, x], axis=1).reshape(M, D)

    for l in range(L):
        mg = modg[l]
        h = _modnorm(xt, norm1[l], mg, 0, 1)
        z = _matmul(h, w_in_p[l], tm_in, 768, BF16, "in_proj")

        pr = {'mu_p': mu_p[l][None], 'mu_n': mu_n[l][None], 'lmu_p': lmu_p[l][None], 'lmu_n': lmu_n[l][None],
              'k_k': rw_k_k[l][None], 'k_a': rw_k_a[l][None], 'r_k': rw_r_k[l].reshape(1, RW_WIDTH),
              'w0': rw_w0[l][:, None, :], 'w_up': w_up[l], 'a0': rw_a0[l][:, None, :], 'a_up': a_up[l],
              'g_up': rw_g_up[l]}
        v_, g_, bonus_, ra_, at_, bt_, kt_, pc_ = _rw_prep(z, pr, B, gps, cgr)
        of_, ob_ = _rw_scan(ra_, at_, bt_, kt_, v_, pc_, B, C)
        ya = _rw_out(of_.reshape(M, RW_WIDTH), ob_.reshape(M, RW_WIDTH), g_, bonus_, rw_ln_w[l], rw_ln_b[l])

        q, k, v = _gqa_prep(z, cosf, sinf, gq[l], gk[l], gps)
        yb = _gqa_attn(q.reshape(B, T, GQ_WIDTH), k.reshape(B, T, GQ_KV_WIDTH), v.reshape(B, T, 2 * GQ_KV_WIDTH), C)
        yb = yb.reshape(M, GQ_WIDTH)

        yc = _na_attn(z.reshape(B, T, Z_W), na_bias[l], C).reshape(M, NA_WIDTH)

        m = _merge(ya, yb, yc, wa[l], wb[l], wc[l], z, tm)
        xt = _matmul_res(m, wo[l], xt, mg, 2, tm, 512, "out_proj")
        h = _modnorm(xt, norm2[l], mg, 3, 4)
        u = _ffn_up(h, w1[l], w3[l], tm_in, 512)
        xt = _matmul_res(u, w2[l], xt, mg, 5, tm, 512, "ffn_down")

    return _final_norm(xt.reshape(B, T, D), final_norm, C)
```

```python
import functools
import math

import numpy as np
import jax
import jax.numpy as jnp
from jax import lax
from jax.experimental import pallas as pl
from jax.experimental.pallas import tpu as pltpu

F32 = jnp.float32
BF16 = jnp.bfloat16

GRP = 256
GRID_W = 64
NORM_EPS = 1e-6

RW_HEAD = 64
RW_HEADS = 8
RW_WIDTH = RW_HEAD * RW_HEADS
LORA_PAD = 128
RW_GN_EPS = 64e-5
SCAN_CHUNK = 64

GQ_HEAD = 128
GQ_HEADS = 8
GQ_KV_HEADS = 2
GQ_GROUP = GQ_HEADS // GQ_KV_HEADS
GQ_WIDTH = GQ_HEAD * GQ_HEADS
GQ_KV_WIDTH = GQ_HEAD * GQ_KV_HEADS
ROPE_THETA = 10000.0

NA_HEAD = 64
NA_HEADS = 8
NA_WIDTH = NA_HEAD * NA_HEADS
NA_WIN_ROWS = 8
NA_WIN_COLS = 16
NEG_BIG = -1e30

VMEM_LIMIT = 56 * 1024 * 1024

ZG_OFF, ZG_W = 0, 6144
ZRKV_OFF = 6144
ZQ_OFF = ZRKV_OFF + 1536
ZN_OFF = ZQ_OFF + 1536
ZL_OFF = ZN_OFF + 1536
ZL_W = 4 * LORA_PAD + 256
Z_W = ZL_OFF + ZL_W


def _cparams(sem):
    return pltpu.CompilerParams(dimension_semantics=sem, vmem_limit_bytes=VMEM_LIMIT)


def _dot(a, b, dims=None, precision=None):
    if dims is None:
        dims = (((a.ndim - 1,), (0,)), ((), ()))
    return lax.dot_general(a, b, dims, precision=precision, preferred_element_type=F32)


def _dot_nt(a, b, precision=None):
    return _dot(a, b, (((1,), (1,)), ((), ())), precision)


def _split3(x):
    x1 = x.astype(BF16)
    r1 = x - x1.astype(F32)
    x2 = r1.astype(BF16)
    x3 = (r1 - x2.astype(F32)).astype(BF16)
    return x1, x2, x3


def _ada_kernel(c_ref, w_ref, b_ref, o_ref):
    c = c_ref[...]
    a = (c * jax.nn.sigmoid(c)).astype(BF16)
    o_ref[0] = _dot(a, w_ref[0].astype(BF16)) + b_ref[0]


def _ada_mod(cc, ada_w, ada_b):
    L, D, N = ada_w.shape
    R = cc.shape[0]
    tn = 1536
    return pl.pallas_call(
        _ada_kernel,
        grid=(L, N // tn),
        in_specs=[pl.BlockSpec((R, D), lambda l, j: (0, 0)),
                  pl.BlockSpec((1, D, tn), lambda l, j: (l, 0, j)),
                  pl.BlockSpec((1, 1, tn), lambda l, j: (l, 0, j))],
        out_specs=pl.BlockSpec((1, R, tn), lambda l, j: (l, 0, j)),
        out_shape=jax.ShapeDtypeStruct((L, R, N), F32),
        compiler_params=_cparams(("parallel", "parallel")),
        name="ada_mod",
    )(cc, ada_w, ada_b.reshape(L, 1, N))


def _modnorm_kernel(x_ref, nw_ref, sh_ref, sc_ref, h_ref, *, groups):
    for g in range(groups):
        rows = pl.ds(g * GRP, GRP)
        x = x_ref[rows, :]
        y = x * lax.rsqrt(jnp.mean(x * x, axis=-1, keepdims=True) + NORM_EPS) * nw_ref[...]
        h_ref[rows, :] = (y * (1.0 + sc_ref[g]) + sh_ref[g]).astype(BF16)


def _modnorm(x, nw, modg, sh_blk, sc_blk):
    M, D = x.shape
    tm = _tile(M, 512)
    groups = tm // GRP
    return pl.pallas_call(
        functools.partial(_modnorm_kernel, groups=groups),
        grid=(M // tm,),
        in_specs=[pl.BlockSpec((tm, D), lambda i: (i, 0)),
                  pl.BlockSpec((1, D), lambda i: (0, 0)),
                  pl.BlockSpec((groups, 1, D), lambda i: (i, 0, sh_blk)),
                  pl.BlockSpec((groups, 1, D), lambda i: (i, 0, sc_blk))],
        out_specs=pl.BlockSpec((tm, D), lambda i: (i, 0)),
        out_shape=jax.ShapeDtypeStruct((M, D), BF16),
        compiler_params=_cparams(("parallel",)),
        name="modnorm",
    )(x, nw.reshape(1, D), modg, modg)


def _mm_kernel(a_ref, b_ref, o_ref):
    o_ref[...] = _dot(a_ref[...], b_ref[...]).astype(o_ref.dtype)


def _matmul(a, b, tm, tn, out_dtype, name):
    M, K = a.shape
    N = b.shape[1]
    return pl.pallas_call(
        _mm_kernel,
        grid=(M // tm, N // tn),
        in_specs=[pl.BlockSpec((tm, K), lambda i, j: (i, 0)),
                  pl.BlockSpec((K, tn), lambda i, j: (0, j))],
        out_specs=pl.BlockSpec((tm, tn), lambda i, j: (i, j)),
        out_shape=jax.ShapeDtypeStruct((M, N), out_dtype),
        compiler_params=_cparams(("parallel", "parallel")),
        name=name,
    )(a, b)


def _mm_res_kernel(a_ref, b_ref, res_ref, gate_ref, o_ref, *, groups):
    acc = _dot(a_ref[...], b_ref[...])
    for g in range(groups):
        rows = slice(g * GRP, (g + 1) * GRP)
        o_ref[rows, :] = res_ref[rows, :] + gate_ref[g] * acc[rows, :]


def _matmul_res(a, b, res, modg, gate_blk, tm, tn, name):
    M, K = a.shape
    N = b.shape[1]
    groups = tm // GRP
    nb = N // tn
    return pl.pallas_call(
        functools.partial(_mm_res_kernel, groups=groups),
        grid=(M // tm, nb),
        in_specs=[pl.BlockSpec((tm, K), lambda i, j: (i, 0)),
                  pl.BlockSpec((K, tn), lambda i, j: (0, j)),
                  pl.BlockSpec((tm, tn), lambda i, j: (i, j)),
                  pl.BlockSpec((groups, 1, tn), lambda i, j: (i, 0, gate_blk * nb + j))],
        out_specs=pl.BlockSpec((tm, tn), lambda i, j: (i, j)),
        out_shape=jax.ShapeDtypeStruct((M, N), F32),
        compiler_params=_cparams(("parallel", "parallel")),
        name=name,
    )(a, b, res, modg)


def _ffn_up_kernel(h_ref, w1_ref, w3_ref, o_ref):
    h = h_ref[...]
    a = _dot(h, w1_ref[...])
    b = _dot(h, w3_ref[...])
    o_ref[...] = (a * jax.nn.sigmoid(a) * b).astype(o_ref.dtype)


def _ffn_up(h, w1, w3, tm, tn):
    M, K = h.shape
    N = w1.shape[1]
    return pl.pallas_call(
        _ffn_up_kernel,
        grid=(M // tm, N // tn),
        in_specs=[pl.BlockSpec((tm, K), lambda i, j: (i, 0)),
                  pl.BlockSpec((K, tn), lambda i, j: (0, j)),
                  pl.BlockSpec((K, tn), lambda i, j: (0, j))],
        out_specs=pl.BlockSpec((tm, tn), lambda i, j: (i, j)),
        out_shape=jax.ShapeDtypeStruct((M, N), BF16),
        compiler_params=_cparams(("parallel", "parallel")),
        name="ffn_up",
    )(h, w1, w3)


def _merge_kernel(ya_ref, yb_ref, yc_ref, wa_ref, wb_ref, wc_ref, ga_ref, gb_ref, gc_ref, o_ref):
    def sig(x):
        return 0.5 * jnp.tanh(0.5 * x) + 0.5

    m = sig(ga_ref[...].astype(F32)) * _dot(ya_ref[...], wa_ref[...])
    m = m + sig(gb_ref[...].astype(F32)) * _dot(yb_ref[...], wb_ref[...])
    m = m + sig(gc_ref[...].astype(F32)) * _dot(yc_ref[...], wc_ref[...])
    o_ref[...] = m.astype(o_ref.dtype)


def _merge(ya, yb, yc, wa, wb, wc, z, tm, tn):
    M = ya.shape[0]
    D = wa.shape[1]
    nb = D // tn
    g0 = ZG_OFF // tn
    return pl.pallas_call(
        _merge_kernel,
        grid=(M // tm, nb),
        in_specs=[pl.BlockSpec((tm, ya.shape[1]), lambda i, j: (i, 0)),
                  pl.BlockSpec((tm, yb.shape[1]), lambda i, j: (i, 0)),
                  pl.BlockSpec((tm, yc.shape[1]), lambda i, j: (i, 0)),
                  pl.BlockSpec((wa.shape[0], tn), lambda i, j: (0, j)),
                  pl.BlockSpec((wb.shape[0], tn), lambda i, j: (0, j)),
                  pl.BlockSpec((wc.shape[0], tn), lambda i, j: (0, j)),
                  pl.BlockSpec((tm, tn), lambda i, j: (i, g0 + j)),
                  pl.BlockSpec((tm, tn), lambda i, j: (i, g0 + nb + j)),
                  pl.BlockSpec((tm, tn), lambda i, j: (i, g0 + 2 * nb + j))],
        out_specs=pl.BlockSpec((tm, tn), lambda i, j: (i, j)),
        out_shape=jax.ShapeDtypeStruct((M, D), BF16),
        compiler_params=_cparams(("parallel", "parallel")),
        name="merge",
    )(ya, yb, yc, wa, wb, wc, z, z, z)


def _final_norm_kernel(x_ref, w_ref, o_ref):
    x = x_ref[0]
    o_ref[0] = x * lax.rsqrt(jnp.mean(x * x, axis=-1, keepdims=True) + NORM_EPS) * w_ref[...]


def _final_norm(xt, w, ctx_len):
    B, T, D = xt.shape
    S = T - ctx_len
    cg = ctx_len // GRP
    return pl.pallas_call(
        _final_norm_kernel,
        grid=(B, S // GRP),
        in_specs=[pl.BlockSpec((1, GRP, D), lambda b, i: (b, cg + i, 0)),
                  pl.BlockSpec((1, D), lambda b, i: (0, 0))],
        out_specs=pl.BlockSpec((1, GRP, D), lambda b, i: (b, i, 0)),
        out_shape=jax.ShapeDtypeStruct((B, S, D), F32),
        compiler_params=_cparams(("parallel", "parallel")),
        name="final_norm",
    )(xt, w.reshape(1, D))


def _gqa_prep_kernel(z_ref, cos_ref, sin_ref, gq_ref, gk_ref, q_ref, k_ref, v_ref):
    cosf = cos_ref[...]
    sinf = sin_ref[...]

    def norm_rope(x, gain):
        x = x.astype(F32)
        y = x * lax.rsqrt(jnp.mean(x * x, axis=-1, keepdims=True) + NORM_EPS) * gain
        return y * cosf + pltpu.roll(y, GQ_HEAD // 2, axis=1) * sinf

    scale = GQ_HEAD ** -0.5 * math.log2(math.e)
    for h in range(GQ_HEADS):
        cols = slice(h * GQ_HEAD, (h + 1) * GQ_HEAD)
        q_ref[:, cols] = (norm_rope(z_ref[:, cols], gq_ref[...]) * scale).astype(BF16)
    for h in range(GQ_KV_HEADS):
        cols = slice(h * GQ_HEAD, (h + 1) * GQ_HEAD)
        zc = slice(GQ_WIDTH + h * GQ_HEAD, GQ_WIDTH + (h + 1) * GQ_HEAD)
        k_ref[:, cols] = norm_rope(z_ref[:, zc], gk_ref[...]).astype(BF16)
        vz = slice(GQ_WIDTH + GQ_KV_WIDTH + h * GQ_HEAD, GQ_WIDTH + GQ_KV_WIDTH + (h + 1) * GQ_HEAD)
        v_ref[:, 2 * h * GQ_HEAD:(2 * h + 1) * GQ_HEAD] = z_ref[:, vz].astype(BF16)
        v_ref[:, (2 * h + 1) * GQ_HEAD:(2 * h + 2) * GQ_HEAD] = jnp.ones((GRP, GQ_HEAD), BF16)


def _gqa_prep(z, cosf, sinf, gq, gk, gps):
    M = z.shape[0]
    zw = GQ_WIDTH + 2 * GQ_KV_WIDTH
    return pl.pallas_call(
        _gqa_prep_kernel,
        grid=(M // GRP,),
        in_specs=[pl.BlockSpec((GRP, zw), lambda i: (i, ZQ_OFF // zw)),
                  pl.BlockSpec((GRP, GQ_HEAD), lambda i: (i % gps, 0)),
                  pl.BlockSpec((GRP, GQ_HEAD), lambda i: (i % gps, 0)),
                  pl.BlockSpec((1, GQ_HEAD), lambda i: (0, 0)),
                  pl.BlockSpec((1, GQ_HEAD), lambda i: (0, 0))],
        out_specs=[pl.BlockSpec((GRP, GQ_WIDTH), lambda i: (i, 0)),
                   pl.BlockSpec((GRP, GQ_KV_WIDTH), lambda i: (i, 0)),
                   pl.BlockSpec((GRP, 2 * GQ_KV_WIDTH), lambda i: (i, 0))],
        out_shape=[jax.ShapeDtypeStruct((M, GQ_WIDTH), BF16),
                   jax.ShapeDtypeStruct((M, GQ_KV_WIDTH), BF16),
                   jax.ShapeDtypeStruct((M, 2 * GQ_KV_WIDTH), BF16)],
        compiler_params=_cparams(("parallel",)),
        name="gqa_prep",
    )(z, cosf, sinf, gq.reshape(1, GQ_HEAD), gk.reshape(1, GQ_HEAD))


def _gqa_attn_kernel(q_ref, k_ref, v_ref, o_ref, *, ctx_tiles, ctx_len, n_keys):
    qi = pl.program_id(2)

    def attend(nk):
        k = k_ref[0, 0:nk, :]
        v = v_ref[0, 0:nk, :]
        def scores(h):
            return _dot_nt(q_ref[0, :, h * GQ_HEAD:(h + 1) * GQ_HEAD], k)

        def finish(h, p):
            ol = _dot(p, v)
            o = ol[:, :GQ_HEAD] / ol[:, GQ_HEAD:GQ_HEAD + 1]
            o_ref[0, :, h * GQ_HEAD:(h + 1) * GQ_HEAD] = o.astype(o_ref.dtype)

        s = scores(0)
        for h in range(GQ_GROUP):
            s_next = scores(h + 1) if h + 1 < GQ_GROUP else None
            p = jnp.exp2(s - jnp.max(s, axis=-1, keepdims=True)).astype(BF16)
            finish(h, p)
            s = s_next

    @pl.when(qi < ctx_tiles)
    def _():
        attend(ctx_len)

    @pl.when(qi >= ctx_tiles)
    def _():
        attend(n_keys)


def _gqa_attn(q, k, v, ctx_len, tq=256):
    B, T, _ = q.shape
    gw = GQ_GROUP * GQ_HEAD
    kern = functools.partial(_gqa_attn_kernel, ctx_tiles=ctx_len // tq, ctx_len=ctx_len, n_keys=T)
    return pl.pallas_call(
        kern,
        grid=(B, GQ_KV_HEADS, T // tq),
        in_specs=[pl.BlockSpec((1, tq, gw), lambda b, g, i: (b, i, g)),
                  pl.BlockSpec((1, T, GQ_HEAD), lambda b, g, i: (b, 0, g)),
                  pl.BlockSpec((1, T, 2 * GQ_HEAD), lambda b, g, i: (b, 0, g))],
        out_specs=pl.BlockSpec((1, tq, gw), lambda b, g, i: (b, i, g)),
        out_shape=jax.ShapeDtypeStruct((B, T, GQ_WIDTH), BF16),
        compiler_params=_cparams(("parallel", "parallel", "arbitrary")),
        name="gqa_attn",
    )(q, k, v)


def _na_kernel(q_ref, k_ref, v_ref, bias_ref, o_ref, qs_ref, ks_ref, vs_ref, *, ctx_len, rows):
    scale = NA_HEAD ** -0.5
    wr = min(NA_WIN_ROWS, rows)
    nwin = wr * GRID_W
    rpg = math.gcd(rows, 8)

    for hh in range(2):
        cols = slice(hh * NA_HEAD, (hh + 1) * NA_HEAD)
        qs_ref[hh] = (q_ref[0, :, cols] * scale).astype(BF16)
        ks_ref[hh] = k_ref[0, :, cols].astype(BF16)
        vs_ref[hh] = v_ref[0, :, cols].astype(BF16)

    for hh in range(2):
        cols = slice(hh * NA_HEAD, (hh + 1) * NA_HEAD)
        kc = ks_ref[hh, 0:ctx_len, :]
        vc = vs_ref[hh, 0:ctx_len, :]

        s = _dot_nt(qs_ref[hh, 0:ctx_len, :], kc)
        p = jnp.exp(s - jnp.max(s, axis=-1, keepdims=True))
        o = _dot(p.astype(BF16), vc) / jnp.sum(p, axis=-1, keepdims=True)
        o_ref[0, 0:ctx_len, cols] = o.astype(o_ref.dtype)

        def scores(g):
            i0 = g * rpg
            qoff = pl.multiple_of(ctx_len + i0 * GRID_W, rpg * GRID_W)
            qg = qs_ref[hh, pl.ds(qoff, rpg * GRID_W), :]
            s2 = _dot_nt(qg, kc)
            koffs, s1 = [], []
            for r in range(rpg):
                rs = jnp.clip(i0 + r - wr // 2, 0, rows - wr)
                koffs.append(pl.multiple_of(ctx_len + rs * GRID_W, GRID_W))
                kw = ks_ref[hh, pl.ds(koffs[r], nwin), :]
                s1.append(_dot_nt(qg[r * GRID_W:(r + 1) * GRID_W], kw) + bias_ref[hh, i0 + r - rs])
            return qoff, koffs, jnp.concatenate(s1, axis=0), s2

        def finish(qoff, koffs, s1, s2):
            m = jnp.maximum(jnp.max(s1, axis=-1, keepdims=True), jnp.max(s2, axis=-1, keepdims=True))
            p1 = jnp.exp(s1 - m)
            p2 = jnp.exp(s2 - m)
            l = jnp.sum(p1, axis=-1, keepdims=True) + jnp.sum(p2, axis=-1, keepdims=True)
            p1 = p1.astype(BF16)
            o1 = [_dot(p1[r * GRID_W:(r + 1) * GRID_W], vs_ref[hh, pl.ds(koffs[r], nwin), :]) for r in range(rpg)]
            o = (jnp.concatenate(o1, axis=0) + _dot(p2.astype(BF16), vc)) / l
            o_ref[0, pl.ds(qoff, rpg * GRID_W), cols] = o.astype(o_ref.dtype)

        ngrp = rows // rpg
        per_iter = math.gcd(ngrp, 4)

        def groups_body(it, carry):
            cur = scores(it * per_iter)
            for u in range(per_iter):
                nxt = scores(it * per_iter + u + 1) if u + 1 < per_iter else None
                finish(*cur)
                cur = nxt
            return carry

        lax.fori_loop(0, ngrp // per_iter, groups_body, 0)


def _na_bias_table(rpb, rows):
    wr = min(NA_WIN_ROWS, rows)
    cols = np.arange(GRID_W)
    cstart = np.clip(cols - NA_WIN_COLS // 2, 0, GRID_W - NA_WIN_COLS)
    c = np.arange(GRID_W)
    inside = (c[None, :] >= cstart[:, None]) & (c[None, :] < cstart[:, None] + NA_WIN_COLS)
    col_off = np.clip(c[None, :] - cols[:, None] + (NA_WIN_COLS - 1), 0, 2 * NA_WIN_COLS - 2)
    onehot = (np.arange(2 * NA_WIN_COLS - 1)[:, None, None] == col_off[None]).astype(np.float32)
    tc = jnp.einsum('...ro,ojc->...rjc', rpb, onehot, precision=lax.Precision.HIGHEST)
    tc = jnp.where(inside, tc, NEG_BIG)
    t = jnp.stack([tc[..., NA_WIN_ROWS - 1 - d:NA_WIN_ROWS - 1 - d + wr, :, :] for d in range(wr)], axis=-4)
    t = jnp.swapaxes(t, -3, -2)
    return t.reshape(rpb.shape[:-2] + (wr, GRID_W, wr * GRID_W))


def _na_attn(z, bias, ctx_len):
    B, T, _ = z.shape
    rows = (T - ctx_len) // GRID_W
    wr = bias.shape[1]
    pw = 2 * NA_HEAD
    c0 = ZN_OFF // pw
    per = NA_WIDTH // pw
    scr = pltpu.VMEM((2, T, NA_HEAD), BF16)
    return pl.pallas_call(
        functools.partial(_na_kernel, ctx_len=ctx_len, rows=rows),
        grid=(B, per),
        in_specs=[pl.BlockSpec((1, T, pw), lambda b, h: (b, 0, c0 + h)),
                  pl.BlockSpec((1, T, pw), lambda b, h: (b, 0, c0 + per + h)),
                  pl.BlockSpec((1, T, pw), lambda b, h: (b, 0, c0 + 2 * per + h)),
                  pl.BlockSpec((2, wr, GRID_W, wr * GRID_W), lambda b, h: (h, 0, 0, 0))],
        out_specs=pl.BlockSpec((1, T, pw), lambda b, h: (b, 0, h)),
        out_shape=jax.ShapeDtypeStruct((B, T, NA_WIDTH), BF16),
        scratch_shapes=[scr, scr, scr],
        compiler_params=_cparams(("parallel", "parallel")),
        name="na_attn",
    )(z, z, z, bias)


RW_PAIRS = RW_HEADS // 2
PAIR_W = 2 * RW_HEAD
PC_ROWS = 8
HALO = 16


def _head0_lanes(shape):
    return lax.broadcasted_iota(jnp.int32, shape, len(shape) - 1) < RW_HEAD


def _per_head_rows(x):
    m0 = _head0_lanes(x.shape)
    return jnp.concatenate([jnp.where(m0, x, 0.0), jnp.where(m0, 0.0, x)], axis=-2)


def _pair_diag(full):
    half = full.shape[-2] // 2
    top, bot = full[..., :half, :], full[..., half:, :]
    return jnp.where(_head0_lanes(top.shape), top, bot)


def _rw_prep_kernel(z_ref, zp_ref, zn_ref, l_ref, lp_ref, ln_ref,
                    mup_ref, mun_ref, lmup_ref, lmun_ref, kk_ref_, ka_ref, rk_ref, w0_ref, wup_ref, a0_ref,
                    aup_ref, gup_ref,
                    v_out, g_out, bonus_out, ra_out, at_out, bt_out, kt_out, pc_out, *, gps, ctx_groups):
    t = pl.program_id(1)
    first = jnp.logical_or(t == 0, t == ctx_groups)
    last = jnp.logical_or(t == ctx_groups - 1, t == gps - 1)
    pv = jnp.where(first, 0.0, 1.0)
    nv = jnp.where(last, 0.0, 1.0)
    row = lax.broadcasted_iota(jnp.int32, (GRP, 1), 0)
    C = SCAN_CHUNK

    def shift(x_ref, p_ref, n_ref, mp_ref, mn_ref):
        x = x_ref[...].astype(F32)
        xp = jnp.where(row == 0, p_ref[HALO - 1:HALO, :].astype(F32) * pv, pltpu.roll(x, 1, axis=0))
        xn = jnp.where(row == GRP - 1, n_ref[0:1, :].astype(F32) * nv, pltpu.roll(x, GRP - 1, axis=0))
        return x + mp_ref[...] * (xp - x) + mn_ref[...] * (xn - x)

    zs = shift(z_ref, zp_ref, zn_ref, mup_ref, mun_ref)
    ls = shift(l_ref, lp_ref, ln_ref, lmup_ref, lmun_ref)
    r = zs[:, 0:RW_WIDTH]
    k = zs[:, RW_WIDTH:2 * RW_WIDTH]
    v = zs[:, 2 * RW_WIDTH:3 * RW_WIDTH]

    ci = lax.broadcasted_iota(jnp.int32, (PAIR_W, PAIR_W), 0) // RW_HEAD
    cj = lax.broadcasted_iota(jnp.int32, (PAIR_W, PAIR_W), 1) // RW_HEAD
    head_ones = jnp.where(ci == cj, 1.0, 0.0).astype(BF16)

    def head_sum(x):
        parts = _split3(x)
        return jnp.concatenate([sum(_dot(part[:, p * PAIR_W:(p + 1) * PAIR_W], head_ones) for part in parts)
                                for p in range(RW_PAIRS)], axis=1)

    def dot_f32(x, w):
        x1, x2, _ = _split3(x)
        w1, w2, _ = _split3(w)
        return _dot(x1, w1) + (_dot(x2, w1) + _dot(x1, w2))

    kk = k * kk_ref_[...]
    kk = kk / jnp.maximum(jnp.sqrt(head_sum(kk * kk)), 1e-12)

    ti = lax.broadcasted_iota(jnp.int32, (GRP, GRP), 0)
    tj = lax.broadcasted_iota(jnp.int32, (GRP, GRP), 1)
    same = (ti // C) == (tj // C)

    v_out[...] = v.astype(BF16)
    gd = ls[:, 4 * LORA_PAD:]
    g_out[...] = _dot(jax.nn.sigmoid(gd).astype(BF16), gup_ref[...].astype(BF16))

    kd_sum = jnp.zeros_like(k)
    for d in range(2):
        wl = ls[:, d * LORA_PAD:(d + 1) * LORA_PAD]
        al = ls[:, (2 + d) * LORA_PAD:(3 + d) * LORA_PAD]
        w_raw = w0_ref[d] + dot_f32(jnp.tanh(wl), wup_ref[d])
        lw = -math.exp(-0.5) * jax.nn.sigmoid(w_raw)
        a = jax.nn.sigmoid(a0_ref[d] + _dot(al.astype(BF16), aup_ref[d].astype(BF16)))
        kd = k * (1.0 + (a - 1.0) * ka_ref[...])
        kd_sum = kd_sum + kd
        tri = jnp.where(jnp.logical_and(same, (ti >= tj) if d == 0 else (ti <= tj)), 1.0, 0.0).astype(BF16)
        cs = sum(_dot(tri, part) for part in _split3(lw))
        pinv = jnp.exp(-cs)
        ra_out[d] = (r * jnp.exp(cs)).astype(BF16)
        at_out[d] = (-kk * jnp.exp(cs - lw)).astype(BF16)
        bt_out[d] = (kk * a * pinv).astype(BF16)
        kt_out[d] = (kd * pinv).astype(BF16)
        for c in range(GRP // C):
            end = c * C + (C - 1 if d == 0 else 0)
            pc_out[d, c * PC_ROWS:(c + 1) * PC_ROWS, :] = jnp.broadcast_to(jnp.exp(cs[end:end + 1, :]),
                                                                          (PC_ROWS, RW_WIDTH))
    bonus_out[...] = head_sum(r * kd_sum * rk_ref[...]) * v


def _rw_prep(z, p, B, gps, ctx_groups):
    M = z.shape[0]
    nb8 = M // HALO
    rb = GRP // HALO
    zc = ZRKV_OFF // 1536
    lc = ZL_OFF // ZL_W
    pcr = GRP // SCAN_CHUNK * PC_ROWS

    def cur(c):
        return lambda b, t: (b * gps + t, c)

    def prev(c):
        return lambda b, t: (jnp.maximum((b * gps + t) * rb - 1, 0), c)

    def nxt(c):
        return lambda b, t: (jnp.minimum((b * gps + t + 1) * rb, nb8 - 1), c)

    def const(shape):
        nd = len(shape)
        return pl.BlockSpec(shape, lambda b, t: (0,) * nd)

    tok = pl.BlockSpec((GRP, RW_WIDTH), lambda b, t: (b * gps + t, 0))
    tok2 = pl.BlockSpec((2, GRP, RW_WIDTH), lambda b, t: (0, b * gps + t, 0))
    s1 = jax.ShapeDtypeStruct((M, RW_WIDTH), F32)
    s2 = jax.ShapeDtypeStruct((2, M, RW_WIDTH), BF16)
    return pl.pallas_call(
        functools.partial(_rw_prep_kernel, gps=gps, ctx_groups=ctx_groups),
        grid=(B, gps),
        in_specs=[pl.BlockSpec((GRP, 1536), cur(zc)),
                  pl.BlockSpec((HALO, 1536), prev(zc)),
                  pl.BlockSpec((HALO, 1536), nxt(zc)),
                  pl.BlockSpec((GRP, ZL_W), cur(lc)),
                  pl.BlockSpec((HALO, ZL_W), prev(lc)),
                  pl.BlockSpec((HALO, ZL_W), nxt(lc)),
                  const((1, 1536)), const((1, 1536)), const((1, ZL_W)), const((1, ZL_W)),
                  const((1, RW_WIDTH)), const((1, RW_WIDTH)), const((1, RW_WIDTH)),
                  const((2, 1, RW_WIDTH)), const((2, LORA_PAD, RW_WIDTH)),
                  const((2, 1, RW_WIDTH)), const((2, LORA_PAD, RW_WIDTH)),
                  const((256, RW_WIDTH))],
        out_specs=[tok, tok, tok, tok2, tok2, tok2, tok2,
                   pl.BlockSpec((2, pcr, RW_WIDTH), lambda b, t: (0, b * gps + t, 0))],
        out_shape=[jax.ShapeDtypeStruct((M, RW_WIDTH), BF16), s1, s1, s2, s2, s2, s2,
                   jax.ShapeDtypeStruct((2, M // GRP * pcr, RW_WIDTH), F32)],
        compiler_params=_cparams(("parallel", "parallel")),
        name="rw_prep",
    )(z, z, z, z, z, z, p['mu_p'], p['mu_n'], p['lmu_p'], p['lmu_n'], p['k_k'], p['k_a'], p['r_k'],
      p['w0'], p['w_up'], p['a0'], p['a_up'], p['g_up'])


def _bmm(a, b, nt=False):
    a = a.astype(BF16)
    b = b.astype(BF16)
    out = []
    for h in range(a.shape[0]):
        out.append(_dot_nt(a[h], b[h]) if nt else _dot(a[h], b[h]))
    return jnp.stack(out, axis=0)


def _bmm_tn(a, b):
    a = a.astype(BF16)
    b = b.astype(BF16)
    out = []
    for h in range(a.shape[0]):
        out.append(_dot(a[h], b[h], (((0,), (0,)), ((), ()))))
    return jnp.stack(out, axis=0)


def _rw_intra_kernel(ra_ref, at_ref, bt_ref, kt_ref, v_ref, pc_ref, rh_ref, o0_ref, gm_ref, sd_ref, *, nck):
    d = pl.program_id(1)
    C, P = SCAN_CHUNK, RW_PAIRS

    def pairs(x):
        x = x.reshape(nck, x.shape[0] // nck, RW_WIDTH)
        return jnp.concatenate([x[:, :, p * PAIR_W:(p + 1) * PAIR_W] for p in range(P)], axis=0)

    def unpairs(y):
        rows = y.shape[1] * nck
        return jnp.concatenate([y[p * nck:(p + 1) * nck].reshape(rows, PAIR_W) for p in range(P)], axis=1)

    ti = lax.broadcasted_iota(jnp.int32, (C, PAIR_W), 0)
    tj = lax.broadcasted_iota(jnp.int32, (C, PAIR_W), 1) % C
    before = (ti - tj) * jnp.where(d == 0, 1, -1)
    strict = (before > 0)[None]
    incl = (before >= 0)[None]
    eye = (ti == tj)[None]

    ra = pairs(ra_ref[0])
    at = pairs(at_ref[0])
    bt = pairs(bt_ref[0])
    kt = pairs(kt_ref[0])
    v = pairs(v_ref[...])
    pc = pairs(pc_ref[0])[:, 0:1, :]

    lhs = jnp.concatenate([at, ra], axis=1)
    rhs = jnp.concatenate([_per_head_rows(bt), _per_head_rows(kt)], axis=1)
    gmat = _bmm(lhs, rhs, nt=True)
    a_ab = jnp.where(strict, gmat[:, :C, :PAIR_W], 0.0)
    a_ak = jnp.where(strict, gmat[:, :C, PAIR_W:], 0.0)
    a_rb = jnp.where(incl, gmat[:, C:, :PAIR_W], 0.0)
    a_rk = jnp.where(incl, gmat[:, C:, PAIR_W:], 0.0)

    tm = jnp.where(eye, 1.0, 0.0) + a_ab
    pw = a_ab
    for _ in range(int(math.log2(C)) - 1):
        pw = _bmm(pw, _per_head_rows(pw))
        tm = tm + _bmm(tm, _per_head_rows(pw))

    vrows = _per_head_rows(v)
    x0 = _bmm(a_ak, vrows)
    au = _bmm(tm, jnp.concatenate([_per_head_rows(at), _per_head_rows(x0)], axis=2))
    ah, u0 = au[:, :, :PAIR_W], au[:, :, PAIR_W:]
    upper = jnp.concatenate([_per_head_rows(ah), _per_head_rows(u0)], axis=2)
    lower = jnp.concatenate([jnp.zeros_like(vrows), vrows], axis=2)
    ro = _bmm(jnp.concatenate([a_rb, a_rk], axis=2),
              jnp.concatenate([upper, lower], axis=1))
    bh = bt * pc
    kh = kt * pc
    rh_ref[0] = unpairs(ra + ro[:, :, :PAIR_W]).astype(rh_ref.dtype)
    o0_ref[0] = unpairs(ro[:, :, PAIR_W:])
    gm_ref[0] = unpairs(_pair_diag(_bmm_tn(ah, bh))).astype(gm_ref.dtype)
    sd_ref[0] = unpairs(_pair_diag(_bmm_tn(jnp.concatenate([u0, v], axis=1), jnp.concatenate([bh, kh], axis=1))))


def _rw_seq_kernel(rhf_ref, o0f_ref, gmf_ref, sdf_ref, pcf_ref, rhb_ref, o0b_ref, gmb_ref, sdb_ref, pcb_ref,
                   of_ref, ob_ref, s_ref, *, nck):
    C = SCAN_CHUNK

    @pl.when(pl.program_id(0) == 0)
    def _():
        s_ref[...] = jnp.zeros(s_ref.shape, F32)

    dirs = ((rhf_ref, o0f_ref, gmf_ref, sdf_ref, pcf_ref, of_ref, range(nck)),
            (rhb_ref, o0b_ref, gmb_ref, sdb_ref, pcb_ref, ob_ref, range(nck - 1, -1, -1)))
    chains = [(d, b, p) for d in range(2) for b in range(s_ref.shape[1]) for p in range(RW_PAIRS)]
    state = {ch: s_ref[ch] for ch in chains}
    for step in range(nck):
        for ch in chains:
            d, b, p = ch
            rh_ref, o0_ref, gm_ref, sd_ref, pc_ref, o_ref, order = dirs[d]
            c = order[step]
            rows = slice(c * C, (c + 1) * C)
            cols = slice(p * PAIR_W, (p + 1) * PAIR_W)
            s = state[ch]
            sb = s.astype(BF16)
            o_ref[b, rows, cols] = _dot_nt(rh_ref[0, b, rows, cols], _per_head_rows(sb)) + o0_ref[0, b, rows, cols]
            state[ch] = (s * pc_ref[0, b, c * PC_ROWS:c * PC_ROWS + 1, cols]
                         + _dot(sb, _per_head_rows(gm_ref[0, b, rows, cols])) + sd_ref[0, b, rows, cols])
    for ch in chains:
        s_ref[ch] = state[ch]


def _rw_scan(ra, at, bt, kt, v, pc, B, ctx_len, nck=4):
    M = v.shape[0]
    T = M // B
    W = RW_WIDTH
    ct = nck * SCAN_CHUNK
    nblk = T // ct
    ncb = ctx_len // ct
    pcr = nck * PC_ROWS
    assert ctx_len % ct == 0 and T % ct == 0

    per_dir = pl.BlockSpec((1, ct, W), lambda b, d, j: (d, b * nblk + j, 0))
    outs = [jax.ShapeDtypeStruct((2, M, W), BF16), jax.ShapeDtypeStruct((2, M, W), F32),
            jax.ShapeDtypeStruct((2, M, W), BF16), jax.ShapeDtypeStruct((2, M, W), F32)]
    rh, o0, gm, sd = pl.pallas_call(
        functools.partial(_rw_intra_kernel, nck=nck),
        grid=(B, 2, nblk),
        in_specs=[per_dir, per_dir, per_dir, per_dir,
                  pl.BlockSpec((ct, W), lambda b, d, j: (b * nblk + j, 0)),
                  pl.BlockSpec((1, pcr, W), lambda b, d, j: (d, b * nblk + j, 0))],
        out_specs=[per_dir, per_dir, per_dir, per_dir],
        out_shape=outs,
        compiler_params=_cparams(("parallel", "parallel", "parallel")),
        name="rw_intra",
    )(ra, at, bt, kt, v, pc)

    def bwd_blk(j):
        return jnp.where(j < ncb, ncb - 1 - j, nblk - 1 - (j - ncb))

    def view(x, rows):
        return x.reshape(2, B, rows, W)

    f_big = pl.BlockSpec((1, B, ct, W), lambda j: (0, 0, j, 0))
    b_big = pl.BlockSpec((1, B, ct, W), lambda j: (1, 0, bwd_blk(j), 0))
    f_pc = pl.BlockSpec((1, B, pcr, W), lambda j: (0, 0, j, 0))
    b_pc = pl.BlockSpec((1, B, pcr, W), lambda j: (1, 0, bwd_blk(j), 0))
    out = jax.ShapeDtypeStruct((B, T, W), F32)
    big = [view(x, T) for x in (rh, o0, gm, sd)]
    pcv = view(pc, T // SCAN_CHUNK * PC_ROWS)
    return pl.pallas_call(
        functools.partial(_rw_seq_kernel, nck=nck),
        grid=(nblk,),
        in_specs=[f_big, f_big, f_big, f_big, f_pc, b_big, b_big, b_big, b_big, b_pc],
        out_specs=[pl.BlockSpec((B, ct, W), lambda j: (0, j, 0)),
                   pl.BlockSpec((B, ct, W), lambda j: (0, bwd_blk(j), 0))],
        out_shape=[out, out],
        scratch_shapes=[pltpu.VMEM((2, B, RW_PAIRS, RW_HEAD, PAIR_W), F32)],
        compiler_params=_cparams(("arbitrary",)),
        name="rw_seq",
    )(*big, pcv, *big, pcv)


def _rw_out_kernel(of_ref, ob_ref, g_ref, bonus_ref, lnw_ref, lnb_ref, y_ref):
    ci = lax.broadcasted_iota(jnp.int32, (PAIR_W, PAIR_W), 0) // RW_HEAD
    cj = lax.broadcasted_iota(jnp.int32, (PAIR_W, PAIR_W), 1) // RW_HEAD
    head_mean = jnp.where(ci == cj, 1.0 / RW_HEAD, 0.0).astype(BF16)

    def mean(x):
        return sum(_dot(part, head_mean) for part in _split3(x))

    for p in range(RW_PAIRS):
        cols = slice(p * PAIR_W, (p + 1) * PAIR_W)
        wkv = of_ref[:, cols] + ob_ref[:, cols]
        cen = wkv - mean(wkv)
        y = cen * lax.rsqrt(mean(cen * cen) + RW_GN_EPS) * lnw_ref[:, cols] + lnb_ref[:, cols] + bonus_ref[:, cols]
        y_ref[:, cols] = (y * g_ref[:, cols]).astype(y_ref.dtype)


def _rw_out(o_f, o_b, g, bonus, lnw, lnb):
    M, W = g.shape
    tm = _tile(M, 512)
    tok = pl.BlockSpec((tm, W), lambda i: (i, 0))
    par = pl.BlockSpec((1, W), lambda i: (0, 0))
    return pl.pallas_call(
        _rw_out_kernel,
        grid=(M // tm,),
        in_specs=[tok, tok, tok, tok, par, par],
        out_specs=tok,
        out_shape=jax.ShapeDtypeStruct((M, W), BF16),
        compiler_params=_cparams(("parallel",)),
        name="rw_out",
    )(o_f, o_b, g, bonus, lnw.reshape(1, W), lnb.reshape(1, W))


def _pad_rows(w, n):
    return jnp.pad(w, [(0, 0)] * (w.ndim - 2) + [(0, n - w.shape[-2]), (0, 0)])


def _pad_cols(w, n):
    return jnp.pad(w, [(0, 0)] * (w.ndim - 1) + [(0, n - w.shape[-1])])


def _deinterleave(w):
    n = w.shape[-1] // GQ_HEAD
    w = w.reshape(w.shape[:-1] + (n, GQ_HEAD // 2, 2))
    return jnp.swapaxes(w, -1, -2).reshape(w.shape[:-3] + (n * GQ_HEAD,))


def _in_layout(w, lora):
    rkv = w[..., 0:1536]
    lo = [w[..., 1536 + i * lora: 1536 + (i + 1) * lora] for i in range(4)]
    gd = w[..., 1536 + 4 * lora: 1536 + 4 * lora + 256]
    o = 1536 + 4 * lora + 256
    zq = w[..., o:o + 1536]
    zn = w[..., o + 1536:o + 3072]
    zg = w[..., o + 3072:]
    zq = jnp.concatenate([_deinterleave(zq[..., :GQ_WIDTH + GQ_KV_WIDTH]), zq[..., GQ_WIDTH + GQ_KV_WIDTH:]], axis=-1)
    return jnp.concatenate([zg, rkv, zq, zn] + [_pad_cols(x, LORA_PAD) for x in lo] + [gd], axis=-1)


def _shift_layout(mu, lora):
    lo = [_pad_cols(mu[..., 1536 + i * lora: 1536 + (i + 1) * lora], LORA_PAD) for i in range(4)]
    return mu[..., 0:1536], jnp.concatenate(lo + [mu[..., 1536 + 4 * lora:]], axis=-1)


def _tile(n, pref):
    t = pref
    while n % t:
        t //= 2
    return t


def _axial_tables(ctx_len, seq):
    n_freq = GQ_HEAD // 4
    inv = ROPE_THETA ** (-jnp.arange(n_freq, dtype=F32) / n_freq)
    t = jnp.arange(seq, dtype=jnp.int32)
    row = (t // GRID_W).astype(F32)
    col = (t % GRID_W).astype(F32)
    ang = jnp.concatenate([row[:, None] * inv, col[:, None] * inv], axis=-1)
    cos, sin = jnp.cos(ang), jnp.sin(ang)
    cosf = jnp.concatenate([jnp.ones((ctx_len, GQ_HEAD), F32), jnp.concatenate([cos, cos], axis=-1)], axis=0)
    sinf = jnp.concatenate([jnp.zeros((ctx_len, GQ_HEAD), F32), jnp.concatenate([-sin, sin], axis=-1)], axis=0)
    return cosf, sinf


def kernel(x, c, ctx, c_ctx, ada_w, ada_b, norm1, norm2, w_in, rw_mu_prev, rw_mu_next, rw_w0, rw_w_up, rw_a0, rw_a_up, rw_g_up, rw_k_k, rw_k_a, rw_r_k, rw_ln_w, rw_ln_b, gq_q_norm, gq_k_norm, na_rpb, w_br_a, w_br_b, w_br_c, w_out, ffn_w1, ffn_w3, ffn_w2, final_norm):
    B, S, D = x.shape
    C = ctx.shape[1]
    L = ada_w.shape[0]
    T = C + S
    M = B * T
    gps = T // GRP
    cgr = C // GRP
    lora = rw_w_up.shape[2]
    assert C % GRP == 0 and S % GRP == 0 and S % GRID_W == 0
    assert w_in.shape[2] == 1536 + 4 * lora + 256 + 1536 + 1536 + 3 * D
    tm = _tile(M, 1024)
    tm_in = next(t for t in (2176, 1024, 512, 256) if M % t == 0)

    w_in_p = _in_layout(w_in, lora).astype(BF16)
    mu_p, lmu_p = _shift_layout(rw_mu_prev, lora)
    mu_n, lmu_n = _shift_layout(rw_mu_next, lora)
    w_up = _pad_rows(rw_w_up, LORA_PAD)
    a_up = _pad_rows(rw_a_up, LORA_PAD)
    gq = _deinterleave(gq_q_norm)
    gk = _deinterleave(gq_k_norm)
    wa = w_br_a.astype(BF16)
    wb = w_br_b.astype(BF16)
    wc = w_br_c.astype(BF16)
    wo = w_out.astype(BF16)
    w1 = ffn_w1.astype(BF16)
    w3 = ffn_w3.astype(BF16)
    w2 = ffn_w2.astype(BF16)
    cosf, sinf = _axial_tables(C, S)
    na_bias = _na_bias_table(na_rpb, S // GRID_W)

    cc = jnp.concatenate([c, c_ctx[None, :], jnp.zeros((-(B + 1) % 8, D), F32)], axis=0)
    mod = _ada_mod(cc, ada_w, ada_b)
    modg = jnp.concatenate([jnp.broadcast_to(mod[:, B, None, None, :], (L, B, cgr, 6 * D)),
                            jnp.broadcast_to(mod[:, :B, None, :], (L, B, gps - cgr, 6 * D))], axis=2)
    modg = modg.reshape(L, B * gps, 1, 6 * D)

    xt = jnp.concatenate([ctx, x], axis=1).reshape(M, D)

    for l in range(L):
        mg = modg[l]
        h = _modnorm(xt, norm1[l], mg, 0, 1)
        z = _matmul(h, w_in_p[l], tm_in, 768, BF16, "in_proj")

        pr = {'mu_p': mu_p[l][None], 'mu_n': mu_n[l][None], 'lmu_p': lmu_p[l][None], 'lmu_n': lmu_n[l][None],
              'k_k': rw_k_k[l][None], 'k_a': rw_k_a[l][None], 'r_k': rw_r_k[l].reshape(1, RW_WIDTH),
              'w0': rw_w0[l][:, None, :], 'w_up': w_up[l], 'a0': rw_a0[l][:, None, :], 'a_up': a_up[l],
              'g_up': rw_g_up[l]}
        v_, g_, bonus_, ra_, at_, bt_, kt_, pc_ = _rw_prep(z, pr, B, gps, cgr)
        of_, ob_ = _rw_scan(ra_, at_, bt_, kt_, v_, pc_, B, C)
        ya = _rw_out(of_.reshape(M, RW_WIDTH), ob_.reshape(M, RW_WIDTH), g_, bonus_, rw_ln_w[l], rw_ln_b[l])

        q, k, v = _gqa_prep(z, cosf, sinf, gq[l], gk[l], gps)
        yb = _gqa_attn(q.reshape(B, T, GQ_WIDTH), k.reshape(B, T, GQ_KV_WIDTH), v.reshape(B, T, 2 * GQ_KV_WIDTH), C)
        yb = yb.reshape(M, GQ_WIDTH)

        yc = _na_attn(z.reshape(B, T, Z_W), na_bias[l], C).reshape(M, NA_WIDTH)

        m = _merge(ya, yb, yc, wa[l], wb[l], wc[l], z, tm, 1024)
        xt = _matmul_res(m, wo[l], xt, mg, 2, tm, 1024, "out_proj")
        h = _modnorm(xt, norm2[l], mg, 3, 4)
        u = _ffn_up(h, w1[l], w3[l], tm, 512)
        xt = _matmul_res(u, w2[l], xt, mg, 5, tm, 512, "ffn_down")

    return _final_norm(xt.reshape(B, T, D), final_norm, C)
```

```python
import functools
import math

import numpy as np
import jax
import jax.numpy as jnp
from jax import lax
from jax.experimental import pallas as pl
from jax.experimental.pallas import tpu as pltpu

F32 = jnp.float32
BF16 = jnp.bfloat16

GRP = 256
GRID_W = 64
NORM_EPS = 1e-6

RW_HEAD = 64
RW_HEADS = 8
RW_WIDTH = RW_HEAD * RW_HEADS
LORA_IN = 256
LORA_GD = 384
RW_GN_EPS = 64e-5
SCAN_CHUNK = 64

GQ_HEAD = 128
GQ_HEADS = 8
GQ_KV_HEADS = 2
GQ_GROUP = GQ_HEADS // GQ_KV_HEADS
GQ_WIDTH = GQ_HEAD * GQ_HEADS
GQ_KV_WIDTH = GQ_HEAD * GQ_KV_HEADS
ROPE_THETA = 10000.0

NA_HEAD = 64
NA_HEADS = 8
NA_WIDTH = NA_HEAD * NA_HEADS
NA_WIN_ROWS = 8
NA_WIN_COLS = 16
NEG_BIG = -1e30

VMEM_LIMIT = 56 * 1024 * 1024

ZG_OFF, ZG_W = 0, 6144
ZRKV_OFF = 6144
ZQ_OFF = ZRKV_OFF + 1536
ZN_OFF = ZQ_OFF + 1536
ZL_OFF = ZN_OFF + 1536
ZL_W = 768
Z_W = ZL_OFF + ZL_W


def _cparams(sem):
    return pltpu.CompilerParams(dimension_semantics=sem, vmem_limit_bytes=VMEM_LIMIT)


def _dot(a, b, dims=None, precision=None):
    if dims is None:
        dims = (((a.ndim - 1,), (0,)), ((), ()))
    return lax.dot_general(a, b, dims, precision=precision, preferred_element_type=F32)


def _dot_nt(a, b, precision=None):
    return _dot(a, b, (((1,), (1,)), ((), ())), precision)


def _split3(x):
    x1 = x.astype(BF16)
    r1 = x - x1.astype(F32)
    x2 = r1.astype(BF16)
    x3 = (r1 - x2.astype(F32)).astype(BF16)
    return x1, x2, x3


def _ada_kernel(c_ref, w_ref, b_ref, o_ref):
    c = c_ref[...]
    a = (c * jax.nn.sigmoid(c)).astype(BF16)
    o_ref[0] = _dot(a, w_ref[0].astype(BF16)) + b_ref[0]


def _ada_mod(cc, ada_w, ada_b):
    L, D, N = ada_w.shape
    R = cc.shape[0]
    tn = 1536
    return pl.pallas_call(
        _ada_kernel,
        grid=(L, N // tn),
        in_specs=[pl.BlockSpec((R, D), lambda l, j: (0, 0)),
                  pl.BlockSpec((1, D, tn), lambda l, j: (l, 0, j)),
                  pl.BlockSpec((1, 1, tn), lambda l, j: (l, 0, j))],
        out_specs=pl.BlockSpec((1, R, tn), lambda l, j: (l, 0, j)),
        out_shape=jax.ShapeDtypeStruct((L, R, N), F32),
        compiler_params=_cparams(("parallel", "parallel")),
        name="ada_mod",
    )(cc, ada_w, ada_b.reshape(L, 1, N))


def _modnorm_kernel(x_ref, nw_ref, sh_ref, sc_ref, h_ref, *, groups):
    for g in range(groups):
        rows = pl.ds(g * GRP, GRP)
        x = x_ref[rows, :]
        y = x * lax.rsqrt(jnp.mean(x * x, axis=-1, keepdims=True) + NORM_EPS) * nw_ref[...]
        h_ref[rows, :] = (y * (1.0 + sc_ref[g]) + sh_ref[g]).astype(BF16)


def _modnorm(x, nw, modg, sh_blk, sc_blk):
    M, D = x.shape
    tm = _tile(M, 512)
    groups = tm // GRP
    return pl.pallas_call(
        functools.partial(_modnorm_kernel, groups=groups),
        grid=(M // tm,),
        in_specs=[pl.BlockSpec((tm, D), lambda i: (i, 0)),
                  pl.BlockSpec((1, D), lambda i: (0, 0)),
                  pl.BlockSpec((groups, 1, D), lambda i: (i, 0, sh_blk)),
                  pl.BlockSpec((groups, 1, D), lambda i: (i, 0, sc_blk))],
        out_specs=pl.BlockSpec((tm, D), lambda i: (i, 0)),
        out_shape=jax.ShapeDtypeStruct((M, D), BF16),
        compiler_params=_cparams(("parallel",)),
        name="modnorm",
    )(x, nw.reshape(1, D), modg, modg)


def _mm_kernel(a_ref, b_ref, o_ref):
    o_ref[...] = _dot(a_ref[...], b_ref[...]).astype(o_ref.dtype)


def _matmul(a, b, tm, tn, out_dtype, name):
    M, K = a.shape
    N = b.shape[1]
    return pl.pallas_call(
        _mm_kernel,
        grid=(M // tm, N // tn),
        in_specs=[pl.BlockSpec((tm, K), lambda i, j: (i, 0)),
                  pl.BlockSpec((K, tn), lambda i, j: (0, j))],
        out_specs=pl.BlockSpec((tm, tn), lambda i, j: (i, j)),
        out_shape=jax.ShapeDtypeStruct((M, N), out_dtype),
        compiler_params=_cparams(("parallel", "parallel")),
        name=name,
    )(a, b)


def _mm_res_kernel(a_ref, b_ref, res_ref, gate_ref, o_ref, *, groups):
    acc = _dot(a_ref[...], b_ref[...])
    for g in range(groups):
        rows = slice(g * GRP, (g + 1) * GRP)
        o_ref[rows, :] = res_ref[rows, :] + gate_ref[g] * acc[rows, :]


def _matmul_res(a, b, res, modg, gate_blk, tm, tn, name):
    M, K = a.shape
    N = b.shape[1]
    groups = tm // GRP
    nb = N // tn
    return pl.pallas_call(
        functools.partial(_mm_res_kernel, groups=groups),
        grid=(M // tm, nb),
        in_specs=[pl.BlockSpec((tm, K), lambda i, j: (i, 0)),
                  pl.BlockSpec((K, tn), lambda i, j: (0, j)),
                  pl.BlockSpec((tm, tn), lambda i, j: (i, j)),
                  pl.BlockSpec((groups, 1, tn), lambda i, j: (i, 0, gate_blk * nb + j))],
        out_specs=pl.BlockSpec((tm, tn), lambda i, j: (i, j)),
        out_shape=jax.ShapeDtypeStruct((M, N), F32),
        compiler_params=_cparams(("parallel", "parallel")),
        name=name,
    )(a, b, res, modg)


def _ffn_up_kernel(h_ref, w1_ref, w3_ref, o_ref):
    h = h_ref[...]
    a = _dot(h, w1_ref[...])
    b = _dot(h, w3_ref[...])
    o_ref[...] = (a * jax.nn.sigmoid(a) * b).astype(o_ref.dtype)


def _ffn_up(h, w1, w3, tm, tn):
    M, K = h.shape
    N = w1.shape[1]
    return pl.pallas_call(
        _ffn_up_kernel,
        grid=(M // tm, N // tn),
        in_specs=[pl.BlockSpec((tm, K), lambda i, j: (i, 0)),
                  pl.BlockSpec((K, tn), lambda i, j: (0, j)),
                  pl.BlockSpec((K, tn), lambda i, j: (0, j))],
        out_specs=pl.BlockSpec((tm, tn), lambda i, j: (i, j)),
        out_shape=jax.ShapeDtypeStruct((M, N), BF16),
        compiler_params=_cparams(("parallel", "parallel")),
        name="ffn_up",
    )(h, w1, w3)


def _merge_kernel(ya_ref, yb_ref, yc_ref, wa_ref, wb_ref, wc_ref, ga_ref, gb_ref, gc_ref, o_ref):
    def sig(x):
        return 0.5 * jnp.tanh(0.5 * x) + 0.5

    m = sig(ga_ref[...].astype(F32)) * _dot(ya_ref[...], wa_ref[...])
    m = m + sig(gb_ref[...].astype(F32)) * _dot(yb_ref[...], wb_ref[...])
    m = m + sig(gc_ref[...].astype(F32)) * _dot(yc_ref[...], wc_ref[...])
    o_ref[...] = m.astype(o_ref.dtype)


def _merge(ya, yb, yc, wa, wb, wc, z, tm, tn):
    M = ya.shape[0]
    D = wa.shape[1]
    nb = D // tn
    g0 = ZG_OFF // tn
    return pl.pallas_call(
        _merge_kernel,
        grid=(M // tm, nb),
        in_specs=[pl.BlockSpec((tm, ya.shape[1]), lambda i, j: (i, 0)),
                  pl.BlockSpec((tm, yb.shape[1]), lambda i, j: (i, 0)),
                  pl.BlockSpec((tm, yc.shape[1]), lambda i, j: (i, 0)),
                  pl.BlockSpec((wa.shape[0], tn), lambda i, j: (0, j)),
                  pl.BlockSpec((wb.shape[0], tn), lambda i, j: (0, j)),
                  pl.BlockSpec((wc.shape[0], tn), lambda i, j: (0, j)),
                  pl.BlockSpec((tm, tn), lambda i, j: (i, g0 + j)),
                  pl.BlockSpec((tm, tn), lambda i, j: (i, g0 + nb + j)),
                  pl.BlockSpec((tm, tn), lambda i, j: (i, g0 + 2 * nb + j))],
        out_specs=pl.BlockSpec((tm, tn), lambda i, j: (i, j)),
        out_shape=jax.ShapeDtypeStruct((M, D), BF16),
        compiler_params=_cparams(("parallel", "parallel")),
        name="merge",
    )(ya, yb, yc, wa, wb, wc, z, z, z)


def _final_norm_kernel(x_ref, w_ref, o_ref):
    x = x_ref[0]
    o_ref[0] = x * lax.rsqrt(jnp.mean(x * x, axis=-1, keepdims=True) + NORM_EPS) * w_ref[...]


def _final_norm(xt, w, ctx_len):
    B, T, D = xt.shape
    S = T - ctx_len
    cg = ctx_len // GRP
    return pl.pallas_call(
        _final_norm_kernel,
        grid=(B, S // GRP),
        in_specs=[pl.BlockSpec((1, GRP, D), lambda b, i: (b, cg + i, 0)),
                  pl.BlockSpec((1, D), lambda b, i: (0, 0))],
        out_specs=pl.BlockSpec((1, GRP, D), lambda b, i: (b, i, 0)),
        out_shape=jax.ShapeDtypeStruct((B, S, D), F32),
        compiler_params=_cparams(("parallel", "parallel")),
        name="final_norm",
    )(xt, w.reshape(1, D))


def _gqa_prep_kernel(z_ref, cos_ref, sin_ref, gq_ref, gk_ref, q_ref, k_ref, v_ref):
    cosf = cos_ref[...]
    sinf = sin_ref[...]

    even = lax.broadcasted_iota(jnp.int32, (1, GQ_HEAD), 1) % 2 == 0

    def norm_rope(x, gain):
        x = x.astype(F32)
        y = x * lax.rsqrt(jnp.mean(x * x, axis=-1, keepdims=True) + NORM_EPS) * gain
        partner = jnp.where(even, pltpu.roll(y, GQ_HEAD - 1, axis=1), pltpu.roll(y, 1, axis=1))
        return y * cosf + partner * sinf

    scale = GQ_HEAD ** -0.5 * math.log2(math.e)
    for h in range(GQ_HEADS):
        cols = slice(h * GQ_HEAD, (h + 1) * GQ_HEAD)
        q_ref[:, cols] = (norm_rope(z_ref[:, cols], gq_ref[...]) * scale).astype(BF16)
    for h in range(GQ_KV_HEADS):
        cols = slice(h * GQ_HEAD, (h + 1) * GQ_HEAD)
        zc = slice(GQ_WIDTH + h * GQ_HEAD, GQ_WIDTH + (h + 1) * GQ_HEAD)
        k_ref[:, cols] = norm_rope(z_ref[:, zc], gk_ref[...]).astype(BF16)
        vz = slice(GQ_WIDTH + GQ_KV_WIDTH + h * GQ_HEAD, GQ_WIDTH + GQ_KV_WIDTH + (h + 1) * GQ_HEAD)
        v_ref[:, 2 * h * GQ_HEAD:(2 * h + 1) * GQ_HEAD] = z_ref[:, vz].astype(BF16)
        v_ref[:, (2 * h + 1) * GQ_HEAD:(2 * h + 2) * GQ_HEAD] = jnp.ones((GRP, GQ_HEAD), BF16)


def _gqa_prep(z, cosf, sinf, gq, gk, gps):
    M = z.shape[0]
    zw = GQ_WIDTH + 2 * GQ_KV_WIDTH
    return pl.pallas_call(
        _gqa_prep_kernel,
        grid=(M // GRP,),
        in_specs=[pl.BlockSpec((GRP, zw), lambda i: (i, ZQ_OFF // zw)),
                  pl.BlockSpec((GRP, GQ_HEAD), lambda i: (i % gps, 0)),
                  pl.BlockSpec((GRP, GQ_HEAD), lambda i: (i % gps, 0)),
                  pl.BlockSpec((1, GQ_HEAD), lambda i: (0, 0)),
                  pl.BlockSpec((1, GQ_HEAD), lambda i: (0, 0))],
        out_specs=[pl.BlockSpec((GRP, GQ_WIDTH), lambda i: (i, 0)),
                   pl.BlockSpec((GRP, GQ_KV_WIDTH), lambda i: (i, 0)),
                   pl.BlockSpec((GRP, 2 * GQ_KV_WIDTH), lambda i: (i, 0))],
        out_shape=[jax.ShapeDtypeStruct((M, GQ_WIDTH), BF16),
                   jax.ShapeDtypeStruct((M, GQ_KV_WIDTH), BF16),
                   jax.ShapeDtypeStruct((M, 2 * GQ_KV_WIDTH), BF16)],
        compiler_params=_cparams(("parallel",)),
        name="gqa_prep",
    )(z, cosf, sinf, gq.reshape(1, GQ_HEAD), gk.reshape(1, GQ_HEAD))


def _gqa_attn_kernel(q_ref, k_ref, v_ref, o_ref, *, ctx_tiles, ctx_len, n_keys):
    qi = pl.program_id(2)

    def attend(nk):
        k = k_ref[0, 0:nk, :]
        v = v_ref[0, 0:nk, :]
        def scores(h):
            return _dot_nt(q_ref[0, :, h * GQ_HEAD:(h + 1) * GQ_HEAD], k)

        def finish(h, p):
            ol = _dot(p, v)
            o = ol[:, :GQ_HEAD] / ol[:, GQ_HEAD:GQ_HEAD + 1]
            o_ref[0, :, h * GQ_HEAD:(h + 1) * GQ_HEAD] = o.astype(o_ref.dtype)

        s = scores(0)
        for h in range(GQ_GROUP):
            s_next = scores(h + 1) if h + 1 < GQ_GROUP else None
            p = jnp.exp2(s - jnp.max(s, axis=-1, keepdims=True)).astype(BF16)
            finish(h, p)
            s = s_next

    @pl.when(qi < ctx_tiles)
    def _():
        attend(ctx_len)

    @pl.when(qi >= ctx_tiles)
    def _():
        attend(n_keys)


def _gqa_attn(q, k, v, ctx_len, tq=256):
    B, T, _ = q.shape
    gw = GQ_GROUP * GQ_HEAD
    kern = functools.partial(_gqa_attn_kernel, ctx_tiles=ctx_len // tq, ctx_len=ctx_len, n_keys=T)
    return pl.pallas_call(
        kern,
        grid=(B, GQ_KV_HEADS, T // tq),
        in_specs=[pl.BlockSpec((1, tq, gw), lambda b, g, i: (b, i, g)),
                  pl.BlockSpec((1, T, GQ_HEAD), lambda b, g, i: (b, 0, g)),
                  pl.BlockSpec((1, T, 2 * GQ_HEAD), lambda b, g, i: (b, 0, g))],
        out_specs=pl.BlockSpec((1, tq, gw), lambda b, g, i: (b, i, g)),
        out_shape=jax.ShapeDtypeStruct((B, T, GQ_WIDTH), BF16),
        compiler_params=_cparams(("parallel", "parallel", "arbitrary")),
        name="gqa_attn",
    )(q, k, v)


def _na_kernel(q_ref, k_ref, v_ref, bias_ref, o_ref, qs_ref, ks_ref, vs_ref, *, ctx_len, rows):
    scale = NA_HEAD ** -0.5
    wr = min(NA_WIN_ROWS, rows)
    nwin = wr * GRID_W
    rpg = math.gcd(rows, 8)

    for hh in range(2):
        cols = slice(hh * NA_HEAD, (hh + 1) * NA_HEAD)
        qs_ref[hh] = (q_ref[0, :, cols] * scale).astype(BF16)
        ks_ref[hh] = k_ref[0, :, cols].astype(BF16)
        vs_ref[hh] = v_ref[0, :, cols].astype(BF16)

    for hh in range(2):
        cols = slice(hh * NA_HEAD, (hh + 1) * NA_HEAD)
        kc = ks_ref[hh, 0:ctx_len, :]
        vc = vs_ref[hh, 0:ctx_len, :]

        s = _dot_nt(qs_ref[hh, 0:ctx_len, :], kc)
        p = jnp.exp(s - jnp.max(s, axis=-1, keepdims=True))
        o = _dot(p.astype(BF16), vc) / jnp.sum(p, axis=-1, keepdims=True)
        o_ref[0, 0:ctx_len, cols] = o.astype(o_ref.dtype)

        def scores(g):
            i0 = g * rpg
            qoff = pl.multiple_of(ctx_len + i0 * GRID_W, rpg * GRID_W)
            qg = qs_ref[hh, pl.ds(qoff, rpg * GRID_W), :]
            s2 = _dot_nt(qg, kc)
            koffs, s1 = [], []
            for r in range(rpg):
                rs = jnp.clip(i0 + r - wr // 2, 0, rows - wr)
                koffs.append(pl.multiple_of(ctx_len + rs * GRID_W, GRID_W))
                kw = ks_ref[hh, pl.ds(koffs[r], nwin), :]
                s1.append(_dot_nt(qg[r * GRID_W:(r + 1) * GRID_W], kw) + bias_ref[hh, i0 + r - rs])
            return qoff, koffs, jnp.concatenate(s1, axis=0), s2

        def finish(qoff, koffs, s1, s2):
            m = jnp.maximum(jnp.max(s1, axis=-1, keepdims=True), jnp.max(s2, axis=-1, keepdims=True))
            p1 = jnp.exp(s1 - m)
            p2 = jnp.exp(s2 - m)
            l = jnp.sum(p1, axis=-1, keepdims=True) + jnp.sum(p2, axis=-1, keepdims=True)
            p1 = p1.astype(BF16)
            o1 = [_dot(p1[r * GRID_W:(r + 1) * GRID_W], vs_ref[hh, pl.ds(koffs[r], nwin), :]) for r in range(rpg)]
            o = (jnp.concatenate(o1, axis=0) + _dot(p2.astype(BF16), vc)) / l
            o_ref[0, pl.ds(qoff, rpg * GRID_W), cols] = o.astype(o_ref.dtype)

        ngrp = rows // rpg
        per_iter = math.gcd(ngrp, 4)

        def groups_body(it, carry):
            cur = scores(it * per_iter)
            for u in range(per_iter):
                nxt = scores(it * per_iter + u + 1) if u + 1 < per_iter else None
                finish(*cur)
                cur = nxt
            return carry

        lax.fori_loop(0, ngrp // per_iter, groups_body, 0)


def _na_bias_table(rpb, rows):
    wr = min(NA_WIN_ROWS, rows)
    cols = np.arange(GRID_W)
    cstart = np.clip(cols - NA_WIN_COLS // 2, 0, GRID_W - NA_WIN_COLS)
    c = np.arange(GRID_W)
    inside = (c[None, :] >= cstart[:, None]) & (c[None, :] < cstart[:, None] + NA_WIN_COLS)
    col_off = np.clip(c[None, :] - cols[:, None] + (NA_WIN_COLS - 1), 0, 2 * NA_WIN_COLS - 2)
    onehot = (np.arange(2 * NA_WIN_COLS - 1)[:, None, None] == col_off[None]).astype(np.float32)
    tc = jnp.einsum('...ro,ojc->...rjc', rpb, onehot, precision=lax.Precision.HIGHEST)
    tc = jnp.where(inside, tc, NEG_BIG)
    t = jnp.stack([tc[..., NA_WIN_ROWS - 1 - d:NA_WIN_ROWS - 1 - d + wr, :, :] for d in range(wr)], axis=-4)
    t = jnp.swapaxes(t, -3, -2)
    return t.reshape(rpb.shape[:-2] + (wr, GRID_W, wr * GRID_W))


def _na_attn(z, bias, ctx_len):
    B, T, _ = z.shape
    rows = (T - ctx_len) // GRID_W
    wr = bias.shape[1]
    pw = 2 * NA_HEAD
    c0 = ZN_OFF // pw
    per = NA_WIDTH // pw
    scr = pltpu.VMEM((2, T, NA_HEAD), BF16)
    return pl.pallas_call(
        functools.partial(_na_kernel, ctx_len=ctx_len, rows=rows),
        grid=(B, per),
        in_specs=[pl.BlockSpec((1, T, pw), lambda b, h: (b, 0, c0 + h)),
                  pl.BlockSpec((1, T, pw), lambda b, h: (b, 0, c0 + per + h)),
                  pl.BlockSpec((1, T, pw), lambda b, h: (b, 0, c0 + 2 * per + h)),
                  pl.BlockSpec((2, wr, GRID_W, wr * GRID_W), lambda b, h: (h, 0, 0, 0))],
        out_specs=pl.BlockSpec((1, T, pw), lambda b, h: (b, 0, h)),
        out_shape=jax.ShapeDtypeStruct((B, T, NA_WIDTH), BF16),
        scratch_shapes=[scr, scr, scr],
        compiler_params=_cparams(("parallel", "parallel")),
        name="na_attn",
    )(z, z, z, bias)


RW_PAIRS = RW_HEADS // 2
PAIR_W = 2 * RW_HEAD
PC_ROWS = 8
HALO = 16


def _head0_lanes(shape):
    return lax.broadcasted_iota(jnp.int32, shape, len(shape) - 1) < RW_HEAD


def _per_head_rows(x):
    m0 = _head0_lanes(x.shape)
    return jnp.concatenate([jnp.where(m0, x, 0.0), jnp.where(m0, 0.0, x)], axis=-2)


def _pair_diag(full):
    half = full.shape[-2] // 2
    top, bot = full[..., :half, :], full[..., half:, :]
    return jnp.where(_head0_lanes(top.shape), top, bot)


def _rw_prep_kernel(z_ref, zp_ref, zn_ref, l_ref, lp_ref, ln_ref,
                    mup_ref, mun_ref, lmup_ref, lmun_ref, kk_ref_, ka_ref, rk_ref, w0_ref, wup_ref, a0_ref,
                    aup_ref, gup_ref,
                    v_out, g_out, bonus_out, ra_out, at_out, bt_out, kt_out, pc_out, *, gps, ctx_groups, lora):
    t = pl.program_id(1)
    first = jnp.logical_or(t == 0, t == ctx_groups)
    last = jnp.logical_or(t == ctx_groups - 1, t == gps - 1)
    pv = jnp.where(first, 0.0, 1.0)
    nv = jnp.where(last, 0.0, 1.0)
    row = lax.broadcasted_iota(jnp.int32, (GRP, 1), 0)
    C = SCAN_CHUNK

    def shift(x_ref, p_ref, n_ref, mp_ref, mn_ref):
        x = x_ref[...].astype(F32)
        xp = jnp.where(row == 0, p_ref[HALO - 1:HALO, :].astype(F32) * pv, pltpu.roll(x, 1, axis=0))
        xn = jnp.where(row == GRP - 1, n_ref[0:1, :].astype(F32) * nv, pltpu.roll(x, GRP - 1, axis=0))
        return x + mp_ref[...] * (xp - x) + mn_ref[...] * (xn - x)

    zs = shift(z_ref, zp_ref, zn_ref, mup_ref, mun_ref)
    ls = shift(l_ref, lp_ref, ln_ref, lmup_ref, lmun_ref)
    r = zs[:, 0:RW_WIDTH]
    k = zs[:, RW_WIDTH:2 * RW_WIDTH]
    v = zs[:, 2 * RW_WIDTH:3 * RW_WIDTH]

    ci = lax.broadcasted_iota(jnp.int32, (PAIR_W, PAIR_W), 0) // RW_HEAD
    cj = lax.broadcasted_iota(jnp.int32, (PAIR_W, PAIR_W), 1) // RW_HEAD
    head_ones = jnp.where(ci == cj, 1.0, 0.0).astype(BF16)

    def head_sum(x):
        parts = _split3(x)
        return jnp.concatenate([sum(_dot(part[:, p * PAIR_W:(p + 1) * PAIR_W], head_ones) for part in parts)
                                for p in range(RW_PAIRS)], axis=1)

    def dot_f32(x, w):
        x1, x2, _ = _split3(x)
        w1, w2, _ = _split3(w)
        return _dot(x1, w1) + (_dot(x2, w1) + _dot(x1, w2))

    kk = k * kk_ref_[...]
    kk = kk / jnp.maximum(jnp.sqrt(head_sum(kk * kk)), 1e-12)

    ti = lax.broadcasted_iota(jnp.int32, (GRP, GRP), 0)
    tj = lax.broadcasted_iota(jnp.int32, (GRP, GRP), 1)
    same = (ti // C) == (tj // C)

    v_out[...] = v.astype(BF16)
    gd = ls[:, LORA_GD:LORA_GD + gup_ref.shape[0]]
    g_out[...] = _dot(jax.nn.sigmoid(gd).astype(BF16), gup_ref[...].astype(BF16))

    lane = lax.broadcasted_iota(jnp.int32, (1, LORA_IN), 1)
    w_lin = dot_f32(jnp.where(lane < 2 * lora, jnp.tanh(ls[:, :LORA_IN]), 0.0), wup_ref[...])
    a_lin = _dot(ls[:, LORA_IN // 2:LORA_IN // 2 + LORA_IN].astype(BF16), aup_ref[...].astype(BF16))

    kd_sum = jnp.zeros_like(k)
    for d in range(2):
        dcols = slice(d * RW_WIDTH, (d + 1) * RW_WIDTH)
        w_raw = w0_ref[d] + w_lin[:, dcols]
        lw = -math.exp(-0.5) * jax.nn.sigmoid(w_raw)
        a = jax.nn.sigmoid(a0_ref[d] + a_lin[:, dcols])
        kd = k * (1.0 + (a - 1.0) * ka_ref[...])
        kd_sum = kd_sum + kd
        tri = jnp.where(jnp.logical_and(same, (ti >= tj) if d == 0 else (ti <= tj)), 1.0, 0.0).astype(BF16)
        cs = sum(_dot(tri, part) for part in _split3(lw))
        pinv = jnp.exp(-cs)
        ra_out[d] = (r * jnp.exp(cs)).astype(BF16)
        at_out[d] = (-kk * jnp.exp(cs - lw)).astype(BF16)
        bt_out[d] = (kk * a * pinv).astype(BF16)
        kt_out[d] = (kd * pinv).astype(BF16)
        for c in range(GRP // C):
            end = c * C + (C - 1 if d == 0 else 0)
            pc_out[d, c * PC_ROWS:(c + 1) * PC_ROWS, :] = jnp.broadcast_to(jnp.exp(cs[end:end + 1, :]),
                                                                          (PC_ROWS, RW_WIDTH))
    bonus_out[...] = head_sum(r * kd_sum * rk_ref[...]) * v


def _rw_prep(z, p, B, gps, ctx_groups, lora):
    M = z.shape[0]
    nb8 = M // HALO
    rb = GRP // HALO
    zc = ZRKV_OFF // 1536
    lc = ZL_OFF // ZL_W
    pcr = GRP // SCAN_CHUNK * PC_ROWS

    def cur(c):
        return lambda b, t: (b * gps + t, c)

    def prev(c):
        return lambda b, t: (jnp.maximum((b * gps + t) * rb - 1, 0), c)

    def nxt(c):
        return lambda b, t: (jnp.minimum((b * gps + t + 1) * rb, nb8 - 1), c)

    def const(shape):
        nd = len(shape)
        return pl.BlockSpec(shape, lambda b, t: (0,) * nd)

    tok = pl.BlockSpec((GRP, RW_WIDTH), lambda b, t: (b * gps + t, 0))
    tok2 = pl.BlockSpec((2, GRP, RW_WIDTH), lambda b, t: (0, b * gps + t, 0))
    s1 = jax.ShapeDtypeStruct((M, RW_WIDTH), F32)
    s2 = jax.ShapeDtypeStruct((2, M, RW_WIDTH), BF16)
    return pl.pallas_call(
        functools.partial(_rw_prep_kernel, gps=gps, ctx_groups=ctx_groups, lora=lora),
        grid=(B, gps),
        in_specs=[pl.BlockSpec((GRP, 1536), cur(zc)),
                  pl.BlockSpec((HALO, 1536), prev(zc)),
                  pl.BlockSpec((HALO, 1536), nxt(zc)),
                  pl.BlockSpec((GRP, ZL_W), cur(lc)),
                  pl.BlockSpec((HALO, ZL_W), prev(lc)),
                  pl.BlockSpec((HALO, ZL_W), nxt(lc)),
                  const((1, 1536)), const((1, 1536)), const((1, ZL_W)), const((1, ZL_W)),
                  const((1, RW_WIDTH)), const((1, RW_WIDTH)), const((1, RW_WIDTH)),
                  const((2, 1, RW_WIDTH)), const((LORA_IN, 2 * RW_WIDTH)),
                  const((2, 1, RW_WIDTH)), const((LORA_IN, 2 * RW_WIDTH)),
                  const(p['g_up'].shape)],
        out_specs=[tok, tok, tok, tok2, tok2, tok2, tok2,
                   pl.BlockSpec((2, pcr, RW_WIDTH), lambda b, t: (0, b * gps + t, 0))],
        out_shape=[jax.ShapeDtypeStruct((M, RW_WIDTH), BF16), s1, s1, s2, s2, s2, s2,
                   jax.ShapeDtypeStruct((2, M // GRP * pcr, RW_WIDTH), F32)],
        compiler_params=_cparams(("parallel", "parallel")),
        name="rw_prep",
    )(z, z, z, z, z, z, p['mu_p'], p['mu_n'], p['lmu_p'], p['lmu_n'], p['k_k'], p['k_a'], p['r_k'],
      p['w0'], p['w_up'], p['a0'], p['a_up'], p['g_up'])


def _bmm(a, b, nt=False):
    a = a.astype(BF16)
    b = b.astype(BF16)
    out = []
    for h in range(a.shape[0]):
        out.append(_dot_nt(a[h], b[h]) if nt else _dot(a[h], b[h]))
    return jnp.stack(out, axis=0)


def _bmm_tn(a, b):
    a = a.astype(BF16)
    b = b.astype(BF16)
    out = []
    for h in range(a.shape[0]):
        out.append(_dot(a[h], b[h], (((0,), (0,)), ((), ()))))
    return jnp.stack(out, axis=0)


def _rw_intra_kernel(ra_ref, at_ref, bt_ref, kt_ref, v_ref, pc_ref, rh_ref, o0_ref, gm_ref, sd_ref, *, nck):
    d = pl.program_id(1)
    C, P = SCAN_CHUNK, RW_PAIRS

    def pairs(x):
        x = x.reshape(nck, x.shape[0] // nck, RW_WIDTH)
        return jnp.concatenate([x[:, :, p * PAIR_W:(p + 1) * PAIR_W] for p in range(P)], axis=0)

    def unpairs(y):
        rows = y.shape[1] * nck
        return jnp.concatenate([y[p * nck:(p + 1) * nck].reshape(rows, PAIR_W) for p in range(P)], axis=1)

    ti = lax.broadcasted_iota(jnp.int32, (C, PAIR_W), 0)
    tj = lax.broadcasted_iota(jnp.int32, (C, PAIR_W), 1) % C
    before = (ti - tj) * jnp.where(d == 0, 1, -1)
    strict = (before > 0)[None]
    incl = (before >= 0)[None]
    eye = (ti == tj)[None]

    ra = pairs(ra_ref[0])
    at = pairs(at_ref[0])
    bt = pairs(bt_ref[0])
    kt = pairs(kt_ref[0])
    v = pairs(v_ref[...])
    pc = pairs(pc_ref[0])[:, 0:1, :]

    lhs = jnp.concatenate([at, ra], axis=1)
    rhs = jnp.concatenate([_per_head_rows(bt), _per_head_rows(kt)], axis=1)
    gmat = _bmm(lhs, rhs, nt=True)
    a_ab = jnp.where(strict, gmat[:, :C, :PAIR_W], 0.0)
    a_ak = jnp.where(strict, gmat[:, :C, PAIR_W:], 0.0)
    a_rb = jnp.where(incl, gmat[:, C:, :PAIR_W], 0.0)
    a_rk = jnp.where(incl, gmat[:, C:, PAIR_W:], 0.0)

    tm = jnp.where(eye, 1.0, 0.0) + a_ab
    pw = a_ab
    for _ in range(int(math.log2(C)) - 1):
        pw = _bmm(pw, _per_head_rows(pw))
        tm = tm + _bmm(tm, _per_head_rows(pw))

    vrows = _per_head_rows(v)
    x0 = _bmm(a_ak, vrows)
    au = _bmm(tm, jnp.concatenate([_per_head_rows(at), _per_head_rows(x0)], axis=2))
    ah, u0 = au[:, :, :PAIR_W], au[:, :, PAIR_W:]
    upper = jnp.concatenate([_per_head_rows(ah), _per_head_rows(u0)], axis=2)
    lower = jnp.concatenate([jnp.zeros_like(vrows), vrows], axis=2)
    ro = _bmm(jnp.concatenate([a_rb, a_rk], axis=2),
              jnp.concatenate([upper, lower], axis=1))
    bh = bt * pc
    kh = kt * pc
    rh_ref[0] = unpairs(ra + ro[:, :, :PAIR_W]).astype(rh_ref.dtype)
    o0_ref[0] = unpairs(ro[:, :, PAIR_W:])
    gm_ref[0] = unpairs(_pair_diag(_bmm_tn(ah, bh))).astype(gm_ref.dtype)
    sd_ref[0] = unpairs(_pair_diag(_bmm_tn(jnp.concatenate([u0, v], axis=1), jnp.concatenate([bh, kh], axis=1))))


def _rw_seq_kernel(rhf_ref, o0f_ref, gmf_ref, sdf_ref, pcf_ref, rhb_ref, o0b_ref, gmb_ref, sdb_ref, pcb_ref,
                   of_ref, ob_ref, s_ref, *, nck):
    C = SCAN_CHUNK

    @pl.when(pl.program_id(0) == 0)
    def _():
        s_ref[...] = jnp.zeros(s_ref.shape, F32)

    dirs = ((rhf_ref, o0f_ref, gmf_ref, sdf_ref, pcf_ref, of_ref, range(nck)),
            (rhb_ref, o0b_ref, gmb_ref, sdb_ref, pcb_ref, ob_ref, range(nck - 1, -1, -1)))
    chains = [(d, b, p) for d in range(2) for b in range(s_ref.shape[1]) for p in range(RW_PAIRS)]
    state = {ch: s_ref[ch] for ch in chains}
    for step in range(nck):
        for ch in chains:
            d, b, p = ch
            rh_ref, o0_ref, gm_ref, sd_ref, pc_ref, o_ref, order = dirs[d]
            c = order[step]
            rows = slice(c * C, (c + 1) * C)
            cols = slice(p * PAIR_W, (p + 1) * PAIR_W)
            s = state[ch]
            sb = s.astype(BF16)
            o_ref[b, rows, cols] = _dot_nt(rh_ref[0, b, rows, cols], _per_head_rows(sb)) + o0_ref[0, b, rows, cols]
            state[ch] = (s * pc_ref[0, b, c * PC_ROWS:c * PC_ROWS + 1, cols]
                         + _dot(sb, _per_head_rows(gm_ref[0, b, rows, cols])) + sd_ref[0, b, rows, cols])
    for ch in chains:
        s_ref[ch] = state[ch]


def _rw_scan(ra, at, bt, kt, v, pc, B, ctx_len, nck=4):
    M = v.shape[0]
    T = M // B
    W = RW_WIDTH
    ct = nck * SCAN_CHUNK
    nblk = T // ct
    ncb = ctx_len // ct
    pcr = nck * PC_ROWS
    assert ctx_len % ct == 0 and T % ct == 0

    per_dir = pl.BlockSpec((1, ct, W), lambda b, d, j: (d, b * nblk + j, 0))
    outs = [jax.ShapeDtypeStruct((2, M, W), BF16), jax.ShapeDtypeStruct((2, M, W), F32),
            jax.ShapeDtypeStruct((2, M, W), BF16), jax.ShapeDtypeStruct((2, M, W), F32)]
    rh, o0, gm, sd = pl.pallas_call(
        functools.partial(_rw_intra_kernel, nck=nck),
        grid=(B, 2, nblk),
        in_specs=[per_dir, per_dir, per_dir, per_dir,
                  pl.BlockSpec((ct, W), lambda b, d, j: (b * nblk + j, 0)),
                  pl.BlockSpec((1, pcr, W), lambda b, d, j: (d, b * nblk + j, 0))],
        out_specs=[per_dir, per_dir, per_dir, per_dir],
        out_shape=outs,
        compiler_params=_cparams(("parallel", "parallel", "parallel")),
        name="rw_intra",
    )(ra, at, bt, kt, v, pc)

    def bwd_blk(j):
        return jnp.where(j < ncb, ncb - 1 - j, nblk - 1 - (j - ncb))

    def view(x, rows):
        return x.reshape(2, B, rows, W)

    f_big = pl.BlockSpec((1, B, ct, W), lambda j: (0, 0, j, 0))
    b_big = pl.BlockSpec((1, B, ct, W), lambda j: (1, 0, bwd_blk(j), 0))
    f_pc = pl.BlockSpec((1, B, pcr, W), lambda j: (0, 0, j, 0))
    b_pc = pl.BlockSpec((1, B, pcr, W), lambda j: (1, 0, bwd_blk(j), 0))
    out = jax.ShapeDtypeStruct((B, T, W), F32)
    big = [view(x, T) for x in (rh, o0, gm, sd)]
    pcv = view(pc, T // SCAN_CHUNK * PC_ROWS)
    return pl.pallas_call(
        functools.partial(_rw_seq_kernel, nck=nck),
        grid=(nblk,),
        in_specs=[f_big, f_big, f_big, f_big, f_pc, b_big, b_big, b_big, b_big, b_pc],
        out_specs=[pl.BlockSpec((B, ct, W), lambda j: (0, j, 0)),
                   pl.BlockSpec((B, ct, W), lambda j: (0, bwd_blk(j), 0))],
        out_shape=[out, out],
        scratch_shapes=[pltpu.VMEM((2, B, RW_PAIRS, RW_HEAD, PAIR_W), F32)],
        compiler_params=_cparams(("arbitrary",)),
        name="rw_seq",
    )(*big, pcv, *big, pcv)


def _rw_out_kernel(of_ref, ob_ref, g_ref, bonus_ref, lnw_ref, lnb_ref, y_ref):
    ci = lax.broadcasted_iota(jnp.int32, (PAIR_W, PAIR_W), 0) // RW_HEAD
    cj = lax.broadcasted_iota(jnp.int32, (PAIR_W, PAIR_W), 1) // RW_HEAD
    head_mean = jnp.where(ci == cj, 1.0 / RW_HEAD, 0.0).astype(BF16)

    def mean(x):
        return sum(_dot(part, head_mean) for part in _split3(x))

    for p in range(RW_PAIRS):
        cols = slice(p * PAIR_W, (p + 1) * PAIR_W)
        wkv = of_ref[:, cols] + ob_ref[:, cols]
        cen = wkv - mean(wkv)
        y = cen * lax.rsqrt(mean(cen * cen) + RW_GN_EPS) * lnw_ref[:, cols] + lnb_ref[:, cols] + bonus_ref[:, cols]
        y_ref[:, cols] = (y * g_ref[:, cols]).astype(y_ref.dtype)


def _rw_out(o_f, o_b, g, bonus, lnw, lnb):
    M, W = g.shape
    tm = _tile(M, 512)
    tok = pl.BlockSpec((tm, W), lambda i: (i, 0))
    par = pl.BlockSpec((1, W), lambda i: (0, 0))
    return pl.pallas_call(
        _rw_out_kernel,
        grid=(M // tm,),
        in_specs=[tok, tok, tok, tok, par, par],
        out_specs=tok,
        out_shape=jax.ShapeDtypeStruct((M, W), BF16),
        compiler_params=_cparams(("parallel",)),
        name="rw_out",
    )(o_f, o_b, g, bonus, lnw.reshape(1, W), lnb.reshape(1, W))


def _pad_cols(w, n):
    return jnp.pad(w, [(0, 0)] * (w.ndim - 1) + [(0, n - w.shape[-1])])


def _in_layout(w, lw):
    o = 1536 + lw
    return jnp.concatenate([w[..., o + 3072:], w[..., 0:1536], w[..., o:o + 3072],
                            _pad_cols(w[..., 1536:o], ZL_W)], axis=-1)


def _shift_layout(mu, lw):
    return mu[..., 0:1536], _pad_cols(mu[..., 1536:1536 + lw], ZL_W)


def _lora_up(up, row0, rows):
    lora, w = up.shape[2], up.shape[3]
    parts = [jnp.pad(up[:, d], ((0, 0), (row0 + d * lora, rows - row0 - (d + 1) * lora), (d * w, (1 - d) * w)))
             for d in range(2)]
    return parts[0] + parts[1]


def _tile(n, pref):
    t = pref
    while n % t:
        t //= 2
    return t


def _axial_tables(ctx_len, seq):
    n_freq = GQ_HEAD // 4
    inv = ROPE_THETA ** (-jnp.arange(n_freq, dtype=F32) / n_freq)
    t = jnp.arange(seq, dtype=jnp.int32)
    row = (t // GRID_W).astype(F32)
    col = (t % GRID_W).astype(F32)
    ang = jnp.concatenate([row[:, None] * inv, col[:, None] * inv], axis=-1)
    cos, sin = jnp.cos(ang), jnp.sin(ang)
    cosf = jnp.concatenate([jnp.ones((ctx_len, GQ_HEAD), F32), jnp.repeat(cos, 2, axis=-1)], axis=0)
    sinf = jnp.concatenate([jnp.zeros((ctx_len, GQ_HEAD), F32),
                            jnp.stack([-sin, sin], axis=-1).reshape(seq, GQ_HEAD)], axis=0)
    return cosf, sinf


def kernel(x, c, ctx, c_ctx, ada_w, ada_b, norm1, norm2, w_in, rw_mu_prev, rw_mu_next, rw_w0, rw_w_up, rw_a0, rw_a_up, rw_g_up, rw_k_k, rw_k_a, rw_r_k, rw_ln_w, rw_ln_b, gq_q_norm, gq_k_norm, na_rpb, w_br_a, w_br_b, w_br_c, w_out, ffn_w1, ffn_w3, ffn_w2, final_norm):
    B, S, D = x.shape
    C = ctx.shape[1]
    L = ada_w.shape[0]
    T = C + S
    M = B * T
    gps = T // GRP
    cgr = C // GRP
    lora = rw_w_up.shape[2]
    assert C % GRP == 0 and S % GRP == 0 and S % GRID_W == 0
    lw = 4 * lora + rw_g_up.shape[1]
    assert w_in.shape[2] == 1536 + lw + 1536 + 1536 + 3 * D
    assert lw <= ZL_W and 2 * lora <= LORA_IN and 2 * lora >= LORA_IN // 2 and 4 * lora == LORA_GD
    tm = _tile(M, 1024)
    tm_in = next(t for t in (2176, 1024, 512, 256) if M % t == 0)

    w_in_p = _in_layout(w_in, lw).astype(BF16)
    mu_p, lmu_p = _shift_layout(rw_mu_prev, lw)
    mu_n, lmu_n = _shift_layout(rw_mu_next, lw)
    w_up = _lora_up(rw_w_up, 0, LORA_IN)
    a_up = _lora_up(rw_a_up, 2 * lora - LORA_IN // 2, LORA_IN)
    gq, gk = gq_q_norm, gq_k_norm
    wa = w_br_a.astype(BF16)
    wb = w_br_b.astype(BF16)
    wc = w_br_c.astype(BF16)
    wo = w_out.astype(BF16)
    w1 = ffn_w1.astype(BF16)
    w3 = ffn_w3.astype(BF16)
    w2 = ffn_w2.astype(BF16)
    cosf, sinf = _axial_tables(C, S)
    na_bias = _na_bias_table(na_rpb, S // GRID_W)

    cc = jnp.concatenate([c, c_ctx[None, :], jnp.zeros((-(B + 1) % 8, D), F32)], axis=0)
    mod = _ada_mod(cc, ada_w, ada_b)
    modg = jnp.concatenate([jnp.broadcast_to(mod[:, B, None, None, :], (L, B, cgr, 6 * D)),
                            jnp.broadcast_to(mod[:, :B, None, :], (L, B, gps - cgr, 6 * D))], axis=2)
    modg = modg.reshape(L, B * gps, 1, 6 * D)

    xt = jnp.concatenate([ctx, x], axis=1).reshape(M, D)

    for l in range(L):
        mg = modg[l]
        h = _modnorm(xt, norm1[l], mg, 0, 1)
        z = _matmul(h, w_in_p[l], tm_in, 768, BF16, "in_proj")

        pr = {'mu_p': mu_p[l][None], 'mu_n': mu_n[l][None], 'lmu_p': lmu_p[l][None], 'lmu_n': lmu_n[l][None],
              'k_k': rw_k_k[l][None], 'k_a': rw_k_a[l][None], 'r_k': rw_r_k[l].reshape(1, RW_WIDTH),
              'w0': rw_w0[l][:, None, :], 'w_up': w_up[l], 'a0': rw_a0[l][:, None, :], 'a_up': a_up[l],
              'g_up': rw_g_up[l]}
        v_, g_, bonus_, ra_, at_, bt_, kt_, pc_ = _rw_prep(z, pr, B, gps, cgr, lora)
        of_, ob_ = _rw_scan(ra_, at_, bt_, kt_, v_, pc_, B, C)
        ya = _rw_out(of_.reshape(M, RW_WIDTH), ob_.reshape(M, RW_WIDTH), g_, bonus_, rw_ln_w[l], rw_ln_b[l])

        q, k, v = _gqa_prep(z, cosf, sinf, gq[l], gk[l], gps)
        yb = _gqa_attn(q.reshape(B, T, GQ_WIDTH), k.reshape(B, T, GQ_KV_WIDTH), v.reshape(B, T, 2 * GQ_KV_WIDTH), C)
        yb = yb.reshape(M, GQ_WIDTH)

        yc = _na_attn(z.reshape(B, T, Z_W), na_bias[l], C).reshape(M, NA_WIDTH)

        m = _merge(ya, yb, yc, wa[l], wb[l], wc[l], z, tm, 1024)
        xt = _matmul_res(m, wo[l], xt, mg, 2, tm, 1024, "out_proj")
        h = _modnorm(xt, norm2[l], mg, 3, 4)
        u = _ffn_up(h, w1[l], w3[l], tm, 512)
        xt = _matmul_res(u, w2[l], xt, mg, 5, tm, 512, "ffn_down")

    return _final_norm(xt.reshape(B, T, D), final_norm, C)
```

```python
import functools
import math

import numpy as np
import jax
import jax.numpy as jnp
from jax import lax
from jax.experimental import pallas as pl
from jax.experimental.pallas import tpu as pltpu

F32 = jnp.float32
BF16 = jnp.bfloat16

GRP = 256
GRID_W = 64
NORM_EPS = 1e-6

RW_HEAD = 64
RW_HEADS = 8
RW_WIDTH = RW_HEAD * RW_HEADS
LORA_IN = 256
LORA_GD = 384
RW_GN_EPS = 64e-5
SCAN_CHUNK = 64

GQ_HEAD = 128
GQ_HEADS = 8
GQ_KV_HEADS = 2
GQ_GROUP = GQ_HEADS // GQ_KV_HEADS
GQ_WIDTH = GQ_HEAD * GQ_HEADS
GQ_KV_WIDTH = GQ_HEAD * GQ_KV_HEADS
ROPE_THETA = 10000.0

NA_HEAD = 64
NA_HEADS = 8
NA_WIDTH = NA_HEAD * NA_HEADS
NA_WIN_ROWS = 8
NA_WIN_COLS = 16
NEG_BIG = -1e30

VMEM_LIMIT = 56 * 1024 * 1024

ZG_OFF, ZG_W = 0, 6144
ZRKV_OFF = 6144
ZQ_OFF = ZRKV_OFF + 1536
ZN_OFF = ZQ_OFF + 1536
ZL_OFF = ZN_OFF + 1536
ZL_W = 768
Z_W = ZL_OFF + ZL_W


def _cparams(sem):
    return pltpu.CompilerParams(dimension_semantics=sem, vmem_limit_bytes=VMEM_LIMIT)


def _dot(a, b, dims=None, precision=None):
    if dims is None:
        dims = (((a.ndim - 1,), (0,)), ((), ()))
    return lax.dot_general(a, b, dims, precision=precision, preferred_element_type=F32)


def _dot_nt(a, b, precision=None):
    return _dot(a, b, (((1,), (1,)), ((), ())), precision)


def _split3(x):
    x1 = x.astype(BF16)
    r1 = x - x1.astype(F32)
    x2 = r1.astype(BF16)
    x3 = (r1 - x2.astype(F32)).astype(BF16)
    return x1, x2, x3


def _ada_kernel(c_ref, w_ref, b_ref, o_ref):
    c = c_ref[...]
    a = (c * jax.nn.sigmoid(c)).astype(BF16)
    o_ref[0] = _dot(a, w_ref[0].astype(BF16)) + b_ref[0]


def _ada_mod(cc, ada_w, ada_b):
    L, D, N = ada_w.shape
    R = cc.shape[0]
    tn = 1536
    return pl.pallas_call(
        _ada_kernel,
        grid=(L, N // tn),
        in_specs=[pl.BlockSpec((R, D), lambda l, j: (0, 0)),
                  pl.BlockSpec((1, D, tn), lambda l, j: (l, 0, j)),
                  pl.BlockSpec((1, 1, tn), lambda l, j: (l, 0, j))],
        out_specs=pl.BlockSpec((1, R, tn), lambda l, j: (l, 0, j)),
        out_shape=jax.ShapeDtypeStruct((L, R, N), F32),
        compiler_params=_cparams(("parallel", "parallel")),
        name="ada_mod",
    )(cc, ada_w, ada_b.reshape(L, 1, N))


def _modnorm_kernel(x_ref, nw_ref, sh_ref, sc_ref, h_ref, *, groups):
    for g in range(groups):
        rows = pl.ds(g * GRP, GRP)
        x = x_ref[rows, :]
        y = x * lax.rsqrt(jnp.mean(x * x, axis=-1, keepdims=True) + NORM_EPS) * nw_ref[...]
        h_ref[rows, :] = (y * (1.0 + sc_ref[g]) + sh_ref[g]).astype(BF16)


def _modnorm(x, nw, modg, sh_blk, sc_blk):
    M, D = x.shape
    tm = _tile(M, 512)
    groups = tm // GRP
    return pl.pallas_call(
        functools.partial(_modnorm_kernel, groups=groups),
        grid=(M // tm,),
        in_specs=[pl.BlockSpec((tm, D), lambda i: (i, 0)),
                  pl.BlockSpec((1, D), lambda i: (0, 0)),
                  pl.BlockSpec((groups, 1, D), lambda i: (i, 0, sh_blk)),
                  pl.BlockSpec((groups, 1, D), lambda i: (i, 0, sc_blk))],
        out_specs=pl.BlockSpec((tm, D), lambda i: (i, 0)),
        out_shape=jax.ShapeDtypeStruct((M, D), BF16),
        compiler_params=_cparams(("parallel",)),
        name="modnorm",
    )(x, nw.reshape(1, D), modg, modg)


def _mm_kernel(a_ref, b_ref, o_ref):
    o_ref[...] = _dot(a_ref[...], b_ref[...]).astype(o_ref.dtype)


def _layer_spec(w, l, tn):
    return pl.BlockSpec((None, w.shape[1], tn), lambda i, j: (l, 0, j))


def _matmul(a, b, l, tm, tn, out_dtype, name):
    M, K = a.shape
    N = b.shape[2]
    return pl.pallas_call(
        _mm_kernel,
        grid=(M // tm, N // tn),
        in_specs=[pl.BlockSpec((tm, K), lambda i, j: (i, 0)),
                  _layer_spec(b, l, tn)],
        out_specs=pl.BlockSpec((tm, tn), lambda i, j: (i, j)),
        out_shape=jax.ShapeDtypeStruct((M, N), out_dtype),
        compiler_params=_cparams(("parallel", "parallel")),
        name=name,
    )(a, b)


def _mm_res_kernel(a_ref, b_ref, res_ref, gate_ref, o_ref, *, groups):
    acc = _dot(a_ref[...], b_ref[...])
    for g in range(groups):
        rows = slice(g * GRP, (g + 1) * GRP)
        o_ref[rows, :] = res_ref[rows, :] + gate_ref[g] * acc[rows, :]


def _matmul_res(a, b, l, res, modg, gate_blk, tm, tn, name):
    M, K = a.shape
    N = b.shape[2]
    groups = tm // GRP
    nb = N // tn
    return pl.pallas_call(
        functools.partial(_mm_res_kernel, groups=groups),
        grid=(M // tm, nb),
        in_specs=[pl.BlockSpec((tm, K), lambda i, j: (i, 0)),
                  _layer_spec(b, l, tn),
                  pl.BlockSpec((tm, tn), lambda i, j: (i, j)),
                  pl.BlockSpec((groups, 1, tn), lambda i, j: (i, 0, gate_blk * nb + j))],
        out_specs=pl.BlockSpec((tm, tn), lambda i, j: (i, j)),
        out_shape=jax.ShapeDtypeStruct((M, N), F32),
        compiler_params=_cparams(("parallel", "parallel")),
        name=name,
    )(a, b, res, modg)


def _ffn_up_kernel(h_ref, w1_ref, w3_ref, o_ref):
    h = h_ref[...]
    a = _dot(h, w1_ref[...])
    b = _dot(h, w3_ref[...])
    o_ref[...] = (a * jax.nn.sigmoid(a) * b).astype(o_ref.dtype)


def _ffn_up(h, w1, w3, l, tm, tn):
    M, K = h.shape
    N = w1.shape[2]
    return pl.pallas_call(
        _ffn_up_kernel,
        grid=(M // tm, N // tn),
        in_specs=[pl.BlockSpec((tm, K), lambda i, j: (i, 0)),
                  _layer_spec(w1, l, tn),
                  _layer_spec(w3, l, tn)],
        out_specs=pl.BlockSpec((tm, tn), lambda i, j: (i, j)),
        out_shape=jax.ShapeDtypeStruct((M, N), BF16),
        compiler_params=_cparams(("parallel", "parallel")),
        name="ffn_up",
    )(h, w1, w3)


def _merge_kernel(ya_ref, yb_ref, yc_ref, wa_ref, wb_ref, wc_ref, ga_ref, gb_ref, gc_ref, o_ref):
    def sig(x):
        return 0.5 * jnp.tanh(0.5 * x) + 0.5

    m = sig(ga_ref[...].astype(F32)) * _dot(ya_ref[...], wa_ref[...])
    m = m + sig(gb_ref[...].astype(F32)) * _dot(yb_ref[...], wb_ref[...])
    m = m + sig(gc_ref[...].astype(F32)) * _dot(yc_ref[...], wc_ref[...])
    o_ref[...] = m.astype(o_ref.dtype)


def _merge(ya, yb, yc, wa, wb, wc, l, z, tm, tn):
    M = ya.shape[0]
    D = wa.shape[2]
    nb = D // tn
    g0 = ZG_OFF // tn
    return pl.pallas_call(
        _merge_kernel,
        grid=(M // tm, nb),
        in_specs=[pl.BlockSpec((tm, ya.shape[1]), lambda i, j: (i, 0)),
                  pl.BlockSpec((tm, yb.shape[1]), lambda i, j: (i, 0)),
                  pl.BlockSpec((tm, yc.shape[1]), lambda i, j: (i, 0)),
                  _layer_spec(wa, l, tn), _layer_spec(wb, l, tn), _layer_spec(wc, l, tn),
                  pl.BlockSpec((tm, tn), lambda i, j: (i, g0 + j)),
                  pl.BlockSpec((tm, tn), lambda i, j: (i, g0 + nb + j)),
                  pl.BlockSpec((tm, tn), lambda i, j: (i, g0 + 2 * nb + j))],
        out_specs=pl.BlockSpec((tm, tn), lambda i, j: (i, j)),
        out_shape=jax.ShapeDtypeStruct((M, D), BF16),
        compiler_params=_cparams(("parallel", "parallel")),
        name="merge",
    )(ya, yb, yc, wa, wb, wc, z, z, z)


def _final_norm_kernel(x_ref, w_ref, o_ref):
    x = x_ref[0]
    o_ref[0] = x * lax.rsqrt(jnp.mean(x * x, axis=-1, keepdims=True) + NORM_EPS) * w_ref[...]


def _final_norm(xt, w, ctx_len):
    B, T, D = xt.shape
    S = T - ctx_len
    cg = ctx_len // GRP
    return pl.pallas_call(
        _final_norm_kernel,
        grid=(B, S // GRP),
        in_specs=[pl.BlockSpec((1, GRP, D), lambda b, i: (b, cg + i, 0)),
                  pl.BlockSpec((1, D), lambda b, i: (0, 0))],
        out_specs=pl.BlockSpec((1, GRP, D), lambda b, i: (b, i, 0)),
        out_shape=jax.ShapeDtypeStruct((B, S, D), F32),
        compiler_params=_cparams(("parallel", "parallel")),
        name="final_norm",
    )(xt, w.reshape(1, D))


def _gqa_prep_kernel(z_ref, cos_ref, sin_ref, gq_ref, gk_ref, q_ref, k_ref, v_ref):
    cosf = cos_ref[...]
    sinf = sin_ref[...]

    even = lax.broadcasted_iota(jnp.int32, (1, GQ_HEAD), 1) % 2 == 0

    def norm_rope(x, gain):
        x = x.astype(F32)
        y = x * lax.rsqrt(jnp.mean(x * x, axis=-1, keepdims=True) + NORM_EPS) * gain
        partner = jnp.where(even, pltpu.roll(y, GQ_HEAD - 1, axis=1), pltpu.roll(y, 1, axis=1))
        return y * cosf + partner * sinf

    scale = GQ_HEAD ** -0.5 * math.log2(math.e)
    for h in range(GQ_HEADS):
        cols = slice(h * GQ_HEAD, (h + 1) * GQ_HEAD)
        q_ref[:, cols] = (norm_rope(z_ref[:, cols], gq_ref[...]) * scale).astype(BF16)
    for h in range(GQ_KV_HEADS):
        cols = slice(h * GQ_HEAD, (h + 1) * GQ_HEAD)
        zc = slice(GQ_WIDTH + h * GQ_HEAD, GQ_WIDTH + (h + 1) * GQ_HEAD)
        k_ref[:, cols] = norm_rope(z_ref[:, zc], gk_ref[...]).astype(BF16)
        vz = slice(GQ_WIDTH + GQ_KV_WIDTH + h * GQ_HEAD, GQ_WIDTH + GQ_KV_WIDTH + (h + 1) * GQ_HEAD)
        v_ref[:, 2 * h * GQ_HEAD:(2 * h + 1) * GQ_HEAD] = z_ref[:, vz].astype(BF16)
        v_ref[:, (2 * h + 1) * GQ_HEAD:(2 * h + 2) * GQ_HEAD] = jnp.ones((GRP, GQ_HEAD), BF16)


def _gqa_prep(z, cosf, sinf, gq, gk, gps):
    M = z.shape[0]
    zw = GQ_WIDTH + 2 * GQ_KV_WIDTH
    return pl.pallas_call(
        _gqa_prep_kernel,
        grid=(M // GRP,),
        in_specs=[pl.BlockSpec((GRP, zw), lambda i: (i, ZQ_OFF // zw)),
                  pl.BlockSpec((GRP, GQ_HEAD), lambda i: (i % gps, 0)),
                  pl.BlockSpec((GRP, GQ_HEAD), lambda i: (i % gps, 0)),
                  pl.BlockSpec((1, GQ_HEAD), lambda i: (0, 0)),
                  pl.BlockSpec((1, GQ_HEAD), lambda i: (0, 0))],
        out_specs=[pl.BlockSpec((GRP, GQ_WIDTH), lambda i: (i, 0)),
                   pl.BlockSpec((GRP, GQ_KV_WIDTH), lambda i: (i, 0)),
                   pl.BlockSpec((GRP, 2 * GQ_KV_WIDTH), lambda i: (i, 0))],
        out_shape=[jax.ShapeDtypeStruct((M, GQ_WIDTH), BF16),
                   jax.ShapeDtypeStruct((M, GQ_KV_WIDTH), BF16),
                   jax.ShapeDtypeStruct((M, 2 * GQ_KV_WIDTH), BF16)],
        compiler_params=_cparams(("parallel",)),
        name="gqa_prep",
    )(z, cosf, sinf, gq.reshape(1, GQ_HEAD), gk.reshape(1, GQ_HEAD))


def _gqa_attn_kernel(q_ref, k_ref, v_ref, o_ref, *, ctx_tiles, ctx_len, n_keys):
    qi = pl.program_id(2)

    def attend(nk):
        k = k_ref[0, 0:nk, :]
        v = v_ref[0, 0:nk, :]
        def scores(h):
            return _dot_nt(q_ref[0, :, h * GQ_HEAD:(h + 1) * GQ_HEAD], k)

        def finish(h, p):
            ol = _dot(p, v)
            o = ol[:, :GQ_HEAD] / ol[:, GQ_HEAD:GQ_HEAD + 1]
            o_ref[0, :, h * GQ_HEAD:(h + 1) * GQ_HEAD] = o.astype(o_ref.dtype)

        s = scores(0)
        for h in range(GQ_GROUP):
            s_next = scores(h + 1) if h + 1 < GQ_GROUP else None
            p = jnp.exp2(s - jnp.max(s, axis=-1, keepdims=True)).astype(BF16)
            finish(h, p)
            s = s_next

    @pl.when(qi < ctx_tiles)
    def _():
        attend(ctx_len)

    @pl.when(qi >= ctx_tiles)
    def _():
        attend(n_keys)


def _gqa_attn(q, k, v, ctx_len, tq=256):
    B, T, _ = q.shape
    gw = GQ_GROUP * GQ_HEAD
    kern = functools.partial(_gqa_attn_kernel, ctx_tiles=ctx_len // tq, ctx_len=ctx_len, n_keys=T)
    return pl.pallas_call(
        kern,
        grid=(B, GQ_KV_HEADS, T // tq),
        in_specs=[pl.BlockSpec((1, tq, gw), lambda b, g, i: (b, i, g)),
                  pl.BlockSpec((1, T, GQ_HEAD), lambda b, g, i: (b, 0, g)),
                  pl.BlockSpec((1, T, 2 * GQ_HEAD), lambda b, g, i: (b, 0, g))],
        out_specs=pl.BlockSpec((1, tq, gw), lambda b, g, i: (b, i, g)),
        out_shape=jax.ShapeDtypeStruct((B, T, GQ_WIDTH), BF16),
        compiler_params=_cparams(("parallel", "parallel", "arbitrary")),
        name="gqa_attn",
    )(q, k, v)


def _na_kernel(q_ref, k_ref, v_ref, bias_ref, o_ref, qs_ref, ks_ref, vs_ref, *, ctx_len, rows):
    scale = NA_HEAD ** -0.5
    wr = min(NA_WIN_ROWS, rows)
    nwin = wr * GRID_W
    rpg = math.gcd(rows, 8)

    for hh in range(2):
        cols = slice(hh * NA_HEAD, (hh + 1) * NA_HEAD)
        qs_ref[hh] = (q_ref[0, :, cols] * scale).astype(BF16)
        ks_ref[hh] = k_ref[0, :, cols].astype(BF16)
        vs_ref[hh] = v_ref[0, :, cols].astype(BF16)

    for hh in range(2):
        cols = slice(hh * NA_HEAD, (hh + 1) * NA_HEAD)
        kc = ks_ref[hh, 0:ctx_len, :]
        vc = vs_ref[hh, 0:ctx_len, :]

        s = _dot_nt(qs_ref[hh, 0:ctx_len, :], kc)
        p = jnp.exp(s - jnp.max(s, axis=-1, keepdims=True))
        o = _dot(p.astype(BF16), vc) / jnp.sum(p, axis=-1, keepdims=True)
        o_ref[0, 0:ctx_len, cols] = o.astype(o_ref.dtype)

        def scores(g):
            i0 = g * rpg
            qoff = pl.multiple_of(ctx_len + i0 * GRID_W, rpg * GRID_W)
            qg = qs_ref[hh, pl.ds(qoff, rpg * GRID_W), :]
            s2 = _dot_nt(qg, kc)
            koffs, s1 = [], []
            for r in range(rpg):
                rs = jnp.clip(i0 + r - wr // 2, 0, rows - wr)
                koffs.append(pl.multiple_of(ctx_len + rs * GRID_W, GRID_W))
                kw = ks_ref[hh, pl.ds(koffs[r], nwin), :]
                s1.append(_dot_nt(qg[r * GRID_W:(r + 1) * GRID_W], kw) + bias_ref[hh, i0 + r - rs])
            return qoff, koffs, jnp.concatenate(s1, axis=0), s2

        def finish(qoff, koffs, s1, s2):
            m = jnp.maximum(jnp.max(s1, axis=-1, keepdims=True), jnp.max(s2, axis=-1, keepdims=True))
            p1 = jnp.exp(s1 - m)
            p2 = jnp.exp(s2 - m)
            l = jnp.sum(p1, axis=-1, keepdims=True) + jnp.sum(p2, axis=-1, keepdims=True)
            p1 = p1.astype(BF16)
            o1 = [_dot(p1[r * GRID_W:(r + 1) * GRID_W], vs_ref[hh, pl.ds(koffs[r], nwin), :]) for r in range(rpg)]
            o = (jnp.concatenate(o1, axis=0) + _dot(p2.astype(BF16), vc)) / l
            o_ref[0, pl.ds(qoff, rpg * GRID_W), cols] = o.astype(o_ref.dtype)

        ngrp = rows // rpg
        per_iter = math.gcd(ngrp, 4)

        def groups_body(it, carry):
            cur = scores(it * per_iter)
            for u in range(per_iter):
                nxt = scores(it * per_iter + u + 1) if u + 1 < per_iter else None
                finish(*cur)
                cur = nxt
            return carry

        lax.fori_loop(0, ngrp // per_iter, groups_body, 0)


def _na_bias_table(rpb, rows):
    wr = min(NA_WIN_ROWS, rows)
    cols = np.arange(GRID_W)
    cstart = np.clip(cols - NA_WIN_COLS // 2, 0, GRID_W - NA_WIN_COLS)
    c = np.arange(GRID_W)
    inside = (c[None, :] >= cstart[:, None]) & (c[None, :] < cstart[:, None] + NA_WIN_COLS)
    col_off = np.clip(c[None, :] - cols[:, None] + (NA_WIN_COLS - 1), 0, 2 * NA_WIN_COLS - 2)
    onehot = (np.arange(2 * NA_WIN_COLS - 1)[:, None, None] == col_off[None]).astype(np.float32)
    tc = jnp.einsum('...ro,ojc->...rjc', rpb, onehot, precision=lax.Precision.HIGHEST)
    tc = jnp.where(inside, tc, NEG_BIG)
    t = jnp.stack([tc[..., NA_WIN_ROWS - 1 - d:NA_WIN_ROWS - 1 - d + wr, :, :] for d in range(wr)], axis=-4)
    t = jnp.swapaxes(t, -3, -2)
    return t.reshape(rpb.shape[:-2] + (wr, GRID_W, wr * GRID_W))


def _na_attn(z, bias, l, ctx_len):
    B, T, _ = z.shape
    rows = (T - ctx_len) // GRID_W
    wr = bias.shape[2]
    pw = 2 * NA_HEAD
    c0 = ZN_OFF // pw
    per = NA_WIDTH // pw
    scr = pltpu.VMEM((2, T, NA_HEAD), BF16)
    return pl.pallas_call(
        functools.partial(_na_kernel, ctx_len=ctx_len, rows=rows),
        grid=(B, per),
        in_specs=[pl.BlockSpec((1, T, pw), lambda b, h: (b, 0, c0 + h)),
                  pl.BlockSpec((1, T, pw), lambda b, h: (b, 0, c0 + per + h)),
                  pl.BlockSpec((1, T, pw), lambda b, h: (b, 0, c0 + 2 * per + h)),
                  pl.BlockSpec((None, 2, wr, GRID_W, wr * GRID_W), lambda b, h: (l, h, 0, 0, 0))],
        out_specs=pl.BlockSpec((1, T, pw), lambda b, h: (b, 0, h)),
        out_shape=jax.ShapeDtypeStruct((B, T, NA_WIDTH), BF16),
        scratch_shapes=[scr, scr, scr],
        compiler_params=_cparams(("parallel", "parallel")),
        name="na_attn",
    )(z, z, z, bias)


RW_PAIRS = RW_HEADS // 2
PAIR_W = 2 * RW_HEAD
PC_ROWS = 8
HALO = 16


def _head0_lanes(shape):
    return lax.broadcasted_iota(jnp.int32, shape, len(shape) - 1) < RW_HEAD


def _per_head_rows(x):
    m0 = _head0_lanes(x.shape)
    return jnp.concatenate([jnp.where(m0, x, 0.0), jnp.where(m0, 0.0, x)], axis=-2)


def _pair_diag(full):
    half = full.shape[-2] // 2
    top, bot = full[..., :half, :], full[..., half:, :]
    return jnp.where(_head0_lanes(top.shape), top, bot)


def _rw_prep_kernel(z_ref, zp_ref, zn_ref, l_ref, lp_ref, ln_ref,
                    mup_ref, mun_ref, lmup_ref, lmun_ref, kk_ref_, ka_ref, rk_ref, w0_ref, wup_ref, a0_ref,
                    aup_ref, gup_ref,
                    v_out, g_out, bonus_out, ra_out, at_out, bt_out, kt_out, pc_out, *, gps, ctx_groups, lora):
    t = pl.program_id(1)
    first = jnp.logical_or(t == 0, t == ctx_groups)
    last = jnp.logical_or(t == ctx_groups - 1, t == gps - 1)
    pv = jnp.where(first, 0.0, 1.0)
    nv = jnp.where(last, 0.0, 1.0)
    row = lax.broadcasted_iota(jnp.int32, (GRP, 1), 0)
    C = SCAN_CHUNK

    def shift(x_ref, p_ref, n_ref, mp_ref, mn_ref):
        x = x_ref[...].astype(F32)
        xp = jnp.where(row == 0, p_ref[HALO - 1:HALO, :].astype(F32) * pv, pltpu.roll(x, 1, axis=0))
        xn = jnp.where(row == GRP - 1, n_ref[0:1, :].astype(F32) * nv, pltpu.roll(x, GRP - 1, axis=0))
        return x + mp_ref[...] * (xp - x) + mn_ref[...] * (xn - x)

    zs = shift(z_ref, zp_ref, zn_ref, mup_ref, mun_ref)
    ls = shift(l_ref, lp_ref, ln_ref, lmup_ref, lmun_ref)
    r = zs[:, 0:RW_WIDTH]
    k = zs[:, RW_WIDTH:2 * RW_WIDTH]
    v = zs[:, 2 * RW_WIDTH:3 * RW_WIDTH]

    ci = lax.broadcasted_iota(jnp.int32, (PAIR_W, PAIR_W), 0) // RW_HEAD
    cj = lax.broadcasted_iota(jnp.int32, (PAIR_W, PAIR_W), 1) // RW_HEAD
    head_ones = jnp.where(ci == cj, 1.0, 0.0).astype(BF16)

    def head_sum(x):
        parts = _split3(x)
        return jnp.concatenate([sum(_dot(part[:, p * PAIR_W:(p + 1) * PAIR_W], head_ones) for part in parts)
                                for p in range(RW_PAIRS)], axis=1)

    def dot_f32(x, w):
        x1, x2, _ = _split3(x)
        w1, w2, _ = _split3(w)
        return _dot(x1, w1) + (_dot(x2, w1) + _dot(x1, w2))

    kk = k * kk_ref_[...]
    kk = kk / jnp.maximum(jnp.sqrt(head_sum(kk * kk)), 1e-12)

    ti = lax.broadcasted_iota(jnp.int32, (GRP, GRP), 0)
    tj = lax.broadcasted_iota(jnp.int32, (GRP, GRP), 1)
    same = (ti // C) == (tj // C)

    v_out[...] = v.astype(BF16)
    gd = ls[:, LORA_GD:LORA_GD + gup_ref.shape[0]]
    g_out[...] = _dot(jax.nn.sigmoid(gd).astype(BF16), gup_ref[...].astype(BF16))

    lane = lax.broadcasted_iota(jnp.int32, (1, LORA_IN), 1)
    w_lin = dot_f32(jnp.where(lane < 2 * lora, jnp.tanh(ls[:, :LORA_IN]), 0.0), wup_ref[...])
    a_lin = _dot(ls[:, LORA_IN // 2:LORA_IN // 2 + LORA_IN].astype(BF16), aup_ref[...].astype(BF16))

    kd_sum = jnp.zeros_like(k)
    for d in range(2):
        dcols = slice(d * RW_WIDTH, (d + 1) * RW_WIDTH)
        w_raw = w0_ref[d] + w_lin[:, dcols]
        lw = -math.exp(-0.5) * jax.nn.sigmoid(w_raw)
        a = jax.nn.sigmoid(a0_ref[d] + a_lin[:, dcols])
        kd = k * (1.0 + (a - 1.0) * ka_ref[...])
        kd_sum = kd_sum + kd
        tri = jnp.where(jnp.logical_and(same, (ti >= tj) if d == 0 else (ti <= tj)), 1.0, 0.0).astype(BF16)
        cs = sum(_dot(tri, part) for part in _split3(lw))
        pinv = jnp.exp(-cs)
        ra_out[d] = (r * jnp.exp(cs)).astype(BF16)
        at_out[d] = (-kk * jnp.exp(cs - lw)).astype(BF16)
        bt_out[d] = (kk * a * pinv).astype(BF16)
        kt_out[d] = (kd * pinv).astype(BF16)
        for c in range(GRP // C):
            end = c * C + (C - 1 if d == 0 else 0)
            pc_out[d, c * PC_ROWS:(c + 1) * PC_ROWS, :] = jnp.broadcast_to(jnp.exp(cs[end:end + 1, :]),
                                                                          (PC_ROWS, RW_WIDTH))
    bonus_out[...] = head_sum(r * kd_sum * rk_ref[...]) * v


def _rw_prep(z, p, B, gps, ctx_groups, lora):
    M = z.shape[0]
    nb8 = M // HALO
    rb = GRP // HALO
    zc = ZRKV_OFF // 1536
    lc = ZL_OFF // ZL_W
    pcr = GRP // SCAN_CHUNK * PC_ROWS

    def cur(c):
        return lambda b, t: (b * gps + t, c)

    def prev(c):
        return lambda b, t: (jnp.maximum((b * gps + t) * rb - 1, 0), c)

    def nxt(c):
        return lambda b, t: (jnp.minimum((b * gps + t + 1) * rb, nb8 - 1), c)

    def const(shape):
        nd = len(shape)
        return pl.BlockSpec(shape, lambda b, t: (0,) * nd)

    tok = pl.BlockSpec((GRP, RW_WIDTH), lambda b, t: (b * gps + t, 0))
    tok2 = pl.BlockSpec((2, GRP, RW_WIDTH), lambda b, t: (0, b * gps + t, 0))
    s1 = jax.ShapeDtypeStruct((M, RW_WIDTH), F32)
    s2 = jax.ShapeDtypeStruct((2, M, RW_WIDTH), BF16)
    return pl.pallas_call(
        functools.partial(_rw_prep_kernel, gps=gps, ctx_groups=ctx_groups, lora=lora),
        grid=(B, gps),
        in_specs=[pl.BlockSpec((GRP, 1536), cur(zc)),
                  pl.BlockSpec((HALO, 1536), prev(zc)),
                  pl.BlockSpec((HALO, 1536), nxt(zc)),
                  pl.BlockSpec((GRP, ZL_W), cur(lc)),
                  pl.BlockSpec((HALO, ZL_W), prev(lc)),
                  pl.BlockSpec((HALO, ZL_W), nxt(lc)),
                  const((1, 1536)), const((1, 1536)), const((1, ZL_W)), const((1, ZL_W)),
                  const((1, RW_WIDTH)), const((1, RW_WIDTH)), const((1, RW_WIDTH)),
                  const((2, 1, RW_WIDTH)), const((LORA_IN, 2 * RW_WIDTH)),
                  const((2, 1, RW_WIDTH)), const((LORA_IN, 2 * RW_WIDTH)),
                  const(p['g_up'].shape)],
        out_specs=[tok, tok, tok, tok2, tok2, tok2, tok2,
                   pl.BlockSpec((2, pcr, RW_WIDTH), lambda b, t: (0, b * gps + t, 0))],
        out_shape=[jax.ShapeDtypeStruct((M, RW_WIDTH), BF16), s1, s1, s2, s2, s2, s2,
                   jax.ShapeDtypeStruct((2, M // GRP * pcr, RW_WIDTH), F32)],
        compiler_params=_cparams(("parallel", "parallel")),
        name="rw_prep",
    )(z, z, z, z, z, z, p['mu_p'], p['mu_n'], p['lmu_p'], p['lmu_n'], p['k_k'], p['k_a'], p['r_k'],
      p['w0'], p['w_up'], p['a0'], p['a_up'], p['g_up'])


def _bmm(a, b, nt=False):
    a = a.astype(BF16)
    b = b.astype(BF16)
    out = []
    for h in range(a.shape[0]):
        out.append(_dot_nt(a[h], b[h]) if nt else _dot(a[h], b[h]))
    return jnp.stack(out, axis=0)


def _bmm_tn(a, b):
    a = a.astype(BF16)
    b = b.astype(BF16)
    out = []
    for h in range(a.shape[0]):
        out.append(_dot(a[h], b[h], (((0,), (0,)), ((), ()))))
    return jnp.stack(out, axis=0)


def _rw_intra_kernel(ra_ref, at_ref, bt_ref, kt_ref, v_ref, pc_ref, rh_ref, o0_ref, gm_ref, sd_ref, *, nck):
    d = pl.program_id(1)
    C, P = SCAN_CHUNK, RW_PAIRS

    def pairs(x):
        x = x.reshape(nck, x.shape[0] // nck, RW_WIDTH)
        return jnp.concatenate([x[:, :, p * PAIR_W:(p + 1) * PAIR_W] for p in range(P)], axis=0)

    def unpairs(y):
        rows = y.shape[1] * nck
        return jnp.concatenate([y[p * nck:(p + 1) * nck].reshape(rows, PAIR_W) for p in range(P)], axis=1)

    ti = lax.broadcasted_iota(jnp.int32, (C, PAIR_W), 0)
    tj = lax.broadcasted_iota(jnp.int32, (C, PAIR_W), 1) % C
    before = (ti - tj) * jnp.where(d == 0, 1, -1)
    strict = (before > 0)[None]
    incl = (before >= 0)[None]
    eye = (ti == tj)[None]

    ra = pairs(ra_ref[0])
    at = pairs(at_ref[0])
    bt = pairs(bt_ref[0])
    kt = pairs(kt_ref[0])
    v = pairs(v_ref[...])
    pc = pairs(pc_ref[0])[:, 0:1, :]

    lhs = jnp.concatenate([at, ra], axis=1)
    rhs = jnp.concatenate([_per_head_rows(bt), _per_head_rows(kt)], axis=1)
    gmat = _bmm(lhs, rhs, nt=True)
    a_ab = jnp.where(strict, gmat[:, :C, :PAIR_W], 0.0)
    a_ak = jnp.where(strict, gmat[:, :C, PAIR_W:], 0.0)
    a_rb = jnp.where(incl, gmat[:, C:, :PAIR_W], 0.0)
    a_rk = jnp.where(incl, gmat[:, C:, PAIR_W:], 0.0)

    tm = jnp.where(eye, 1.0, 0.0) + a_ab
    pw = a_ab
    for _ in range(int(math.log2(C)) - 1):
        pw = _bmm(pw, _per_head_rows(pw))
        tm = tm + _bmm(tm, _per_head_rows(pw))

    vrows = _per_head_rows(v)
    x0 = _bmm(a_ak, vrows)
    au = _bmm(tm, jnp.concatenate([_per_head_rows(at), _per_head_rows(x0)], axis=2))
    ah, u0 = au[:, :, :PAIR_W], au[:, :, PAIR_W:]
    upper = jnp.concatenate([_per_head_rows(ah), _per_head_rows(u0)], axis=2)
    lower = jnp.concatenate([jnp.zeros_like(vrows), vrows], axis=2)
    ro = _bmm(jnp.concatenate([a_rb, a_rk], axis=2),
              jnp.concatenate([upper, lower], axis=1))
    bh = bt * pc
    kh = kt * pc
    rh_ref[0] = unpairs(ra + ro[:, :, :PAIR_W]).astype(rh_ref.dtype)
    o0_ref[0] = unpairs(ro[:, :, PAIR_W:])
    gm_ref[0] = unpairs(_pair_diag(_bmm_tn(ah, bh))).astype(gm_ref.dtype)
    sd_ref[0] = unpairs(_pair_diag(_bmm_tn(jnp.concatenate([u0, v], axis=1), jnp.concatenate([bh, kh], axis=1))))


def _rw_seq_kernel(rhf_ref, o0f_ref, gmf_ref, sdf_ref, pcf_ref, rhb_ref, o0b_ref, gmb_ref, sdb_ref, pcb_ref,
                   of_ref, ob_ref, s_ref, *, nck):
    C = SCAN_CHUNK

    @pl.when(pl.program_id(0) == 0)
    def _():
        s_ref[...] = jnp.zeros(s_ref.shape, F32)

    dirs = ((rhf_ref, o0f_ref, gmf_ref, sdf_ref, pcf_ref, of_ref, range(nck)),
            (rhb_ref, o0b_ref, gmb_ref, sdb_ref, pcb_ref, ob_ref, range(nck - 1, -1, -1)))
    chains = [(d, b, p) for d in range(2) for b in range(s_ref.shape[1]) for p in range(RW_PAIRS)]
    state = {ch: s_ref[ch] for ch in chains}
    for step in range(nck):
        for ch in chains:
            d, b, p = ch
            rh_ref, o0_ref, gm_ref, sd_ref, pc_ref, o_ref, order = dirs[d]
            c = order[step]
            rows = slice(c * C, (c + 1) * C)
            cols = slice(p * PAIR_W, (p + 1) * PAIR_W)
            s = state[ch]
            sb = s.astype(BF16)
            o_ref[b, rows, cols] = _dot_nt(rh_ref[0, b, rows, cols], _per_head_rows(sb)) + o0_ref[0, b, rows, cols]
            state[ch] = (s * pc_ref[0, b, c * PC_ROWS:c * PC_ROWS + 1, cols]
                         + _dot(sb, _per_head_rows(gm_ref[0, b, rows, cols])) + sd_ref[0, b, rows, cols])
    for ch in chains:
        s_ref[ch] = state[ch]


def _rw_scan(ra, at, bt, kt, v, pc, B, ctx_len, nck=4):
    M = v.shape[0]
    T = M // B
    W = RW_WIDTH
    ct = nck * SCAN_CHUNK
    nblk = T // ct
    ncb = ctx_len // ct
    pcr = nck * PC_ROWS
    assert ctx_len % ct == 0 and T % ct == 0

    per_dir = pl.BlockSpec((1, ct, W), lambda b, d, j: (d, b * nblk + j, 0))
    outs = [jax.ShapeDtypeStruct((2, M, W), BF16), jax.ShapeDtypeStruct((2, M, W), F32),
            jax.ShapeDtypeStruct((2, M, W), BF16), jax.ShapeDtypeStruct((2, M, W), F32)]
    rh, o0, gm, sd = pl.pallas_call(
        functools.partial(_rw_intra_kernel, nck=nck),
        grid=(B, 2, nblk),
        in_specs=[per_dir, per_dir, per_dir, per_dir,
                  pl.BlockSpec((ct, W), lambda b, d, j: (b * nblk + j, 0)),
                  pl.BlockSpec((1, pcr, W), lambda b, d, j: (d, b * nblk + j, 0))],
        out_specs=[per_dir, per_dir, per_dir, per_dir],
        out_shape=outs,
        compiler_params=_cparams(("parallel", "parallel", "parallel")),
        name="rw_intra",
    )(ra, at, bt, kt, v, pc)

    def bwd_blk(j):
        return jnp.where(j < ncb, ncb - 1 - j, nblk - 1 - (j - ncb))

    def view(x, rows):
        return x.reshape(2, B, rows, W)

    f_big = pl.BlockSpec((1, B, ct, W), lambda j: (0, 0, j, 0))
    b_big = pl.BlockSpec((1, B, ct, W), lambda j: (1, 0, bwd_blk(j), 0))
    f_pc = pl.BlockSpec((1, B, pcr, W), lambda j: (0, 0, j, 0))
    b_pc = pl.BlockSpec((1, B, pcr, W), lambda j: (1, 0, bwd_blk(j), 0))
    out = jax.ShapeDtypeStruct((B, T, W), F32)
    big = [view(x, T) for x in (rh, o0, gm, sd)]
    pcv = view(pc, T // SCAN_CHUNK * PC_ROWS)
    return pl.pallas_call(
        functools.partial(_rw_seq_kernel, nck=nck),
        grid=(nblk,),
        in_specs=[f_big, f_big, f_big, f_big, f_pc, b_big, b_big, b_big, b_big, b_pc],
        out_specs=[pl.BlockSpec((B, ct, W), lambda j: (0, j, 0)),
                   pl.BlockSpec((B, ct, W), lambda j: (0, bwd_blk(j), 0))],
        out_shape=[out, out],
        scratch_shapes=[pltpu.VMEM((2, B, RW_PAIRS, RW_HEAD, PAIR_W), F32)],
        compiler_params=_cparams(("arbitrary",)),
        name="rw_seq",
    )(*big, pcv, *big, pcv)


def _rw_out_kernel(of_ref, ob_ref, g_ref, bonus_ref, lnw_ref, lnb_ref, y_ref):
    ci = lax.broadcasted_iota(jnp.int32, (PAIR_W, PAIR_W), 0) // RW_HEAD
    cj = lax.broadcasted_iota(jnp.int32, (PAIR_W, PAIR_W), 1) // RW_HEAD
    head_mean = jnp.where(ci == cj, 1.0 / RW_HEAD, 0.0).astype(BF16)

    def mean(x):
        return sum(_dot(part, head_mean) for part in _split3(x))

    for p in range(RW_PAIRS):
        cols = slice(p * PAIR_W, (p + 1) * PAIR_W)
        wkv = of_ref[:, cols] + ob_ref[:, cols]
        cen = wkv - mean(wkv)
        y = cen * lax.rsqrt(mean(cen * cen) + RW_GN_EPS) * lnw_ref[:, cols] + lnb_ref[:, cols] + bonus_ref[:, cols]
        y_ref[:, cols] = (y * g_ref[:, cols]).astype(y_ref.dtype)


def _rw_out(o_f, o_b, g, bonus, lnw, lnb):
    M, W = g.shape
    tm = _tile(M, 512)
    tok = pl.BlockSpec((tm, W), lambda i: (i, 0))
    par = pl.BlockSpec((1, W), lambda i: (0, 0))
    return pl.pallas_call(
        _rw_out_kernel,
        grid=(M // tm,),
        in_specs=[tok, tok, tok, tok, par, par],
        out_specs=tok,
        out_shape=jax.ShapeDtypeStruct((M, W), BF16),
        compiler_params=_cparams(("parallel",)),
        name="rw_out",
    )(o_f, o_b, g, bonus, lnw.reshape(1, W), lnb.reshape(1, W))


def _pad_cols(w, n):
    return jnp.pad(w, [(0, 0)] * (w.ndim - 1) + [(0, n - w.shape[-1])])


def _in_layout(w, lw):
    o = 1536 + lw
    parts = [w[..., o + 3072:], w[..., 0:1536], w[..., o:o + 3072], _pad_cols(w[..., 1536:o], ZL_W)]
    return jnp.concatenate([x.astype(BF16) for x in parts], axis=-1)


def _shift_layout(mu, lw):
    return mu[..., 0:1536], _pad_cols(mu[..., 1536:1536 + lw], ZL_W)


def _lora_up(up, row0, rows):
    lora, w = up.shape[2], up.shape[3]
    parts = [jnp.pad(up[:, d], ((0, 0), (row0 + d * lora, rows - row0 - (d + 1) * lora), (d * w, (1 - d) * w)))
             for d in range(2)]
    return parts[0] + parts[1]


def _tile(n, pref):
    t = pref
    while n % t:
        t //= 2
    return t


def _axial_tables(ctx_len, seq):
    n_freq = GQ_HEAD // 4
    inv = ROPE_THETA ** (-jnp.arange(n_freq, dtype=F32) / n_freq)
    t = jnp.arange(seq, dtype=jnp.int32)
    row = (t // GRID_W).astype(F32)
    col = (t % GRID_W).astype(F32)
    ang = jnp.concatenate([row[:, None] * inv, col[:, None] * inv], axis=-1)
    cos, sin = jnp.cos(ang), jnp.sin(ang)
    cosf = jnp.concatenate([jnp.ones((ctx_len, GQ_HEAD), F32), jnp.repeat(cos, 2, axis=-1)], axis=0)
    sinf = jnp.concatenate([jnp.zeros((ctx_len, GQ_HEAD), F32),
                            jnp.stack([-sin, sin], axis=-1).reshape(seq, GQ_HEAD)], axis=0)
    return cosf, sinf


def kernel(x, c, ctx, c_ctx, ada_w, ada_b, norm1, norm2, w_in, rw_mu_prev, rw_mu_next, rw_w0, rw_w_up, rw_a0, rw_a_up, rw_g_up, rw_k_k, rw_k_a, rw_r_k, rw_ln_w, rw_ln_b, gq_q_norm, gq_k_norm, na_rpb, w_br_a, w_br_b, w_br_c, w_out, ffn_w1, ffn_w3, ffn_w2, final_norm):
    B, S, D = x.shape
    C = ctx.shape[1]
    L = ada_w.shape[0]
    T = C + S
    M = B * T
    gps = T // GRP
    cgr = C // GRP
    lora = rw_w_up.shape[2]
    assert C % GRP == 0 and S % GRP == 0 and S % GRID_W == 0
    lw = 4 * lora + rw_g_up.shape[1]
    assert w_in.shape[2] == 1536 + lw + 1536 + 1536 + 3 * D
    assert lw <= ZL_W and 2 * lora <= LORA_IN and 2 * lora >= LORA_IN // 2 and 4 * lora == LORA_GD
    tm = _tile(M, 1024)
    tm_in = next(t for t in (2176, 1024, 512, 256) if M % t == 0)

    w_in_p = _in_layout(w_in, lw)
    mu_p, lmu_p = _shift_layout(rw_mu_prev, lw)
    mu_n, lmu_n = _shift_layout(rw_mu_next, lw)
    w_up = _lora_up(rw_w_up, 0, LORA_IN)
    a_up = _lora_up(rw_a_up, 2 * lora - LORA_IN // 2, LORA_IN)
    gq, gk = gq_q_norm, gq_k_norm
    wa = w_br_a.astype(BF16)
    wb = w_br_b.astype(BF16)
    wc = w_br_c.astype(BF16)
    wo = w_out.astype(BF16)
    w1 = ffn_w1.astype(BF16)
    w3 = ffn_w3.astype(BF16)
    w2 = ffn_w2.astype(BF16)
    cosf, sinf = _axial_tables(C, S)
    na_bias = _na_bias_table(na_rpb, S // GRID_W)

    cc = jnp.concatenate([c, c_ctx[None, :], jnp.zeros((-(B + 1) % 8, D), F32)], axis=0)
    mod = _ada_mod(cc, ada_w, ada_b)
    modg = jnp.concatenate([jnp.broadcast_to(mod[:, B, None, None, :], (L, B, cgr, 6 * D)),
                            jnp.broadcast_to(mod[:, :B, None, :], (L, B, gps - cgr, 6 * D))], axis=2)
    modg = modg.reshape(L, B * gps, 1, 6 * D)

    xt = jnp.concatenate([ctx, x], axis=1).reshape(M, D)

    for l in range(L):
        mg = modg[l]
        h = _modnorm(xt, norm1[l], mg, 0, 1)
        z = _matmul(h, w_in_p, l, tm_in, 768, BF16, "in_proj")

        pr = {'mu_p': mu_p[l][None], 'mu_n': mu_n[l][None], 'lmu_p': lmu_p[l][None], 'lmu_n': lmu_n[l][None],
              'k_k': rw_k_k[l][None], 'k_a': rw_k_a[l][None], 'r_k': rw_r_k[l].reshape(1, RW_WIDTH),
              'w0': rw_w0[l][:, None, :], 'w_up': w_up[l], 'a0': rw_a0[l][:, None, :], 'a_up': a_up[l],
              'g_up': rw_g_up[l]}
        v_, g_, bonus_, ra_, at_, bt_, kt_, pc_ = _rw_prep(z, pr, B, gps, cgr, lora)
        of_, ob_ = _rw_scan(ra_, at_, bt_, kt_, v_, pc_, B, C)
        ya = _rw_out(of_.reshape(M, RW_WIDTH), ob_.reshape(M, RW_WIDTH), g_, bonus_, rw_ln_w[l], rw_ln_b[l])

        q, k, v = _gqa_prep(z, cosf, sinf, gq[l], gk[l], gps)
        yb = _gqa_attn(q.reshape(B, T, GQ_WIDTH), k.reshape(B, T, GQ_KV_WIDTH), v.reshape(B, T, 2 * GQ_KV_WIDTH), C)
        yb = yb.reshape(M, GQ_WIDTH)

        yc = _na_attn(z.reshape(B, T, Z_W), na_bias, l, C).reshape(M, NA_WIDTH)

        m = _merge(ya, yb, yc, wa, wb, wc, l, z, tm, 1024)
        xt = _matmul_res(m, wo, l, xt, mg, 2, tm, 1024, "out_proj")
        h = _modnorm(xt, norm2[l], mg, 3, 4)
        u = _ffn_up(h, w1, w3, l, tm, 512)
        xt = _matmul_res(u, w2, l, xt, mg, 5, tm, 512, "ffn_down")

    return _final_norm(xt.reshape(B, T, D), final_norm, C)
```

```python
import functools
import math

import numpy as np
import jax
import jax.numpy as jnp
from jax import lax
from jax.experimental import pallas as pl
from jax.experimental.pallas import tpu as pltpu

F32 = jnp.float32
BF16 = jnp.bfloat16

GRP = 256
GRID_W = 64
NORM_EPS = 1e-6

RW_HEAD = 64
RW_HEADS = 8
RW_WIDTH = RW_HEAD * RW_HEADS
LORA_IN = 256
LORA_GD = 384
RW_GN_EPS = 64e-5
SCAN_CHUNK = 64

GQ_HEAD = 128
GQ_HEADS = 8
GQ_KV_HEADS = 2
GQ_GROUP = GQ_HEADS // GQ_KV_HEADS
GQ_WIDTH = GQ_HEAD * GQ_HEADS
GQ_KV_WIDTH = GQ_HEAD * GQ_KV_HEADS
ROPE_THETA = 10000.0

NA_HEAD = 64
NA_HEADS = 8
NA_WIDTH = NA_HEAD * NA_HEADS
NA_WIN_ROWS = 8
NA_WIN_COLS = 16
NEG_BIG = -1e30

VMEM_LIMIT = 56 * 1024 * 1024

ZG_OFF, ZG_W = 0, 6144
ZRKV_OFF = 6144
ZQ_OFF = ZRKV_OFF + 1536
ZN_OFF = ZQ_OFF + 1536
ZL_OFF = ZN_OFF + 1536
ZL_W = 768
Z_W = ZL_OFF + ZL_W


def _cparams(sem):
    return pltpu.CompilerParams(dimension_semantics=sem, vmem_limit_bytes=VMEM_LIMIT)


def _dot(a, b, dims=None, precision=None):
    if dims is None:
        dims = (((a.ndim - 1,), (0,)), ((), ()))
    return lax.dot_general(a, b, dims, precision=precision, preferred_element_type=F32)


def _dot_nt(a, b, precision=None):
    return _dot(a, b, (((1,), (1,)), ((), ())), precision)


def _split3(x):
    x1 = x.astype(BF16)
    r1 = x - x1.astype(F32)
    x2 = r1.astype(BF16)
    x3 = (r1 - x2.astype(F32)).astype(BF16)
    return x1, x2, x3


def _ada_kernel(c_ref, w_ref, b_ref, o_ref):
    c = c_ref[...]
    a = (c * jax.nn.sigmoid(c)).astype(BF16)
    o_ref[0] = _dot(a, w_ref[0].astype(BF16)) + b_ref[0]


def _ada_mod(cc, ada_w, ada_b):
    L, D, N = ada_w.shape
    R = cc.shape[0]
    tn = 1536
    return pl.pallas_call(
        _ada_kernel,
        grid=(L, N // tn),
        in_specs=[pl.BlockSpec((R, D), lambda l, j: (0, 0)),
                  pl.BlockSpec((1, D, tn), lambda l, j: (l, 0, j)),
                  pl.BlockSpec((1, 1, tn), lambda l, j: (l, 0, j))],
        out_specs=pl.BlockSpec((1, R, tn), lambda l, j: (l, 0, j)),
        out_shape=jax.ShapeDtypeStruct((L, R, N), F32),
        compiler_params=_cparams(("parallel", "parallel")),
        name="ada_mod",
    )(cc, ada_w, ada_b.reshape(L, 1, N))


def _modnorm_kernel(x_ref, nw_ref, sh_ref, sc_ref, h_ref, *, groups):
    for g in range(groups):
        rows = pl.ds(g * GRP, GRP)
        x = x_ref[rows, :]
        y = x * lax.rsqrt(jnp.mean(x * x, axis=-1, keepdims=True) + NORM_EPS) * nw_ref[...]
        h_ref[rows, :] = (y * (1.0 + sc_ref[g]) + sh_ref[g]).astype(BF16)


def _modnorm(x, nw, modg, sh_blk, sc_blk):
    M, D = x.shape
    tm = _tile(M, 512)
    groups = tm // GRP
    return pl.pallas_call(
        functools.partial(_modnorm_kernel, groups=groups),
        grid=(M // tm,),
        in_specs=[pl.BlockSpec((tm, D), lambda i: (i, 0)),
                  pl.BlockSpec((1, D), lambda i: (0, 0)),
                  pl.BlockSpec((groups, 1, D), lambda i: (i, 0, sh_blk)),
                  pl.BlockSpec((groups, 1, D), lambda i: (i, 0, sc_blk))],
        out_specs=pl.BlockSpec((tm, D), lambda i: (i, 0)),
        out_shape=jax.ShapeDtypeStruct((M, D), BF16),
        compiler_params=_cparams(("parallel",)),
        name="modnorm",
    )(x, nw.reshape(1, D), modg, modg)


def _mm_kernel(a_ref, b_ref, o_ref):
    o_ref[...] = _dot(a_ref[...], b_ref[...]).astype(o_ref.dtype)


def _layer_spec(w, l, tn):
    return pl.BlockSpec((None, w.shape[1], tn), lambda i, j: (l, 0, j))


def _matmul(a, b, l, tm, tn, out_dtype, name):
    M, K = a.shape
    N = b.shape[2]
    return pl.pallas_call(
        _mm_kernel,
        grid=(M // tm, N // tn),
        in_specs=[pl.BlockSpec((tm, K), lambda i, j: (i, 0)),
                  _layer_spec(b, l, tn)],
        out_specs=pl.BlockSpec((tm, tn), lambda i, j: (i, j)),
        out_shape=jax.ShapeDtypeStruct((M, N), out_dtype),
        compiler_params=_cparams(("parallel", "parallel")),
        name=name,
    )(a, b)


def _mm_res_kernel(a_ref, b_ref, res_ref, gate_ref, o_ref, *, groups):
    acc = _dot(a_ref[...], b_ref[...])
    for g in range(groups):
        rows = slice(g * GRP, (g + 1) * GRP)
        o_ref[rows, :] = res_ref[rows, :] + gate_ref[g] * acc[rows, :]


def _matmul_res(a, b, l, res, modg, gate_blk, tm, tn, name):
    M, K = a.shape
    N = b.shape[2]
    groups = tm // GRP
    nb = N // tn
    return pl.pallas_call(
        functools.partial(_mm_res_kernel, groups=groups),
        grid=(M // tm, nb),
        in_specs=[pl.BlockSpec((tm, K), lambda i, j: (i, 0)),
                  _layer_spec(b, l, tn),
                  pl.BlockSpec((tm, tn), lambda i, j: (i, j)),
                  pl.BlockSpec((groups, 1, tn), lambda i, j: (i, 0, gate_blk * nb + j))],
        out_specs=pl.BlockSpec((tm, tn), lambda i, j: (i, j)),
        out_shape=jax.ShapeDtypeStruct((M, N), F32),
        compiler_params=_cparams(("parallel", "parallel")),
        name=name,
    )(a, b, res, modg)


def _ffn_up_kernel(h_ref, w1_ref, w3_ref, o_ref):
    h = h_ref[...]
    a = _dot(h, w1_ref[...])
    b = _dot(h, w3_ref[...])
    o_ref[...] = (a * jax.nn.sigmoid(a) * b).astype(o_ref.dtype)


def _ffn_up(h, w1, w3, l, tm, tn):
    M, K = h.shape
    N = w1.shape[2]
    return pl.pallas_call(
        _ffn_up_kernel,
        grid=(M // tm, N // tn),
        in_specs=[pl.BlockSpec((tm, K), lambda i, j: (i, 0)),
                  _layer_spec(w1, l, tn),
                  _layer_spec(w3, l, tn)],
        out_specs=pl.BlockSpec((tm, tn), lambda i, j: (i, j)),
        out_shape=jax.ShapeDtypeStruct((M, N), BF16),
        compiler_params=_cparams(("parallel", "parallel")),
        name="ffn_up",
    )(h, w1, w3)


def _merge_kernel(ya_ref, yb_ref, yc_ref, wa_ref, wb_ref, wc_ref, ga_ref, gb_ref, gc_ref, o_ref):
    def sig(x):
        return 0.5 * jnp.tanh(0.5 * x) + 0.5

    m = sig(ga_ref[...].astype(F32)) * _dot(ya_ref[...], wa_ref[...])
    m = m + sig(gb_ref[...].astype(F32)) * _dot(yb_ref[...], wb_ref[...])
    m = m + sig(gc_ref[...].astype(F32)) * _dot(yc_ref[...], wc_ref[...])
    o_ref[...] = m.astype(o_ref.dtype)


def _merge(ya, yb, yc, wa, wb, wc, l, z, tm, tn):
    M = ya.shape[0]
    D = wa.shape[2]
    nb = D // tn
    g0 = ZG_OFF // tn
    return pl.pallas_call(
        _merge_kernel,
        grid=(M // tm, nb),
        in_specs=[pl.BlockSpec((tm, ya.shape[1]), lambda i, j: (i, 0)),
                  pl.BlockSpec((tm, yb.shape[1]), lambda i, j: (i, 0)),
                  pl.BlockSpec((tm, yc.shape[1]), lambda i, j: (i, 0)),
                  _layer_spec(wa, l, tn), _layer_spec(wb, l, tn), _layer_spec(wc, l, tn),
                  pl.BlockSpec((tm, tn), lambda i, j: (i, g0 + j)),
                  pl.BlockSpec((tm, tn), lambda i, j: (i, g0 + nb + j)),
                  pl.BlockSpec((tm, tn), lambda i, j: (i, g0 + 2 * nb + j))],
        out_specs=pl.BlockSpec((tm, tn), lambda i, j: (i, j)),
        out_shape=jax.ShapeDtypeStruct((M, D), BF16),
        compiler_params=_cparams(("parallel", "parallel")),
        name="merge",
    )(ya, yb, yc, wa, wb, wc, z, z, z)


def _final_norm_kernel(x_ref, w_ref, o_ref):
    x = x_ref[0]
    o_ref[0] = x * lax.rsqrt(jnp.mean(x * x, axis=-1, keepdims=True) + NORM_EPS) * w_ref[...]


def _final_norm(xt, w, ctx_len):
    B, T, D = xt.shape
    S = T - ctx_len
    cg = ctx_len // GRP
    return pl.pallas_call(
        _final_norm_kernel,
        grid=(B, S // GRP),
        in_specs=[pl.BlockSpec((1, GRP, D), lambda b, i: (b, cg + i, 0)),
                  pl.BlockSpec((1, D), lambda b, i: (0, 0))],
        out_specs=pl.BlockSpec((1, GRP, D), lambda b, i: (b, i, 0)),
        out_shape=jax.ShapeDtypeStruct((B, S, D), F32),
        compiler_params=_cparams(("parallel", "parallel")),
        name="final_norm",
    )(xt, w.reshape(1, D))


def _gqa_prep_kernel(z_ref, cos_ref, sin_ref, gq_ref, gk_ref, q_ref, k_ref, v_ref):
    cosf = cos_ref[...]
    sinf = sin_ref[...]

    even = lax.broadcasted_iota(jnp.int32, (1, GQ_HEAD), 1) % 2 == 0

    def norm_rope(x, gain):
        x = x.astype(F32)
        y = x * lax.rsqrt(jnp.mean(x * x, axis=-1, keepdims=True) + NORM_EPS) * gain
        partner = jnp.where(even, pltpu.roll(y, GQ_HEAD - 1, axis=1), pltpu.roll(y, 1, axis=1))
        return y * cosf + partner * sinf

    scale = GQ_HEAD ** -0.5 * math.log2(math.e)
    for h in range(GQ_HEADS):
        cols = slice(h * GQ_HEAD, (h + 1) * GQ_HEAD)
        q_ref[:, cols] = (norm_rope(z_ref[:, cols], gq_ref[...]) * scale).astype(BF16)
    for h in range(GQ_KV_HEADS):
        cols = slice(h * GQ_HEAD, (h + 1) * GQ_HEAD)
        zc = slice(GQ_WIDTH + h * GQ_HEAD, GQ_WIDTH + (h + 1) * GQ_HEAD)
        k_ref[:, cols] = norm_rope(z_ref[:, zc], gk_ref[...]).astype(BF16)
        vz = slice(GQ_WIDTH + GQ_KV_WIDTH + h * GQ_HEAD, GQ_WIDTH + GQ_KV_WIDTH + (h + 1) * GQ_HEAD)
        v_ref[:, 2 * h * GQ_HEAD:(2 * h + 1) * GQ_HEAD] = z_ref[:, vz].astype(BF16)
        v_ref[:, (2 * h + 1) * GQ_HEAD:(2 * h + 2) * GQ_HEAD] = jnp.ones((GRP, GQ_HEAD), BF16)


def _gqa_prep(z, cosf, sinf, gq, gk, gps):
    M = z.shape[0]
    zw = GQ_WIDTH + 2 * GQ_KV_WIDTH
    return pl.pallas_call(
        _gqa_prep_kernel,
        grid=(M // GRP,),
        in_specs=[pl.BlockSpec((GRP, zw), lambda i: (i, ZQ_OFF // zw)),
                  pl.BlockSpec((GRP, GQ_HEAD), lambda i: (i % gps, 0)),
                  pl.BlockSpec((GRP, GQ_HEAD), lambda i: (i % gps, 0)),
                  pl.BlockSpec((1, GQ_HEAD), lambda i: (0, 0)),
                  pl.BlockSpec((1, GQ_HEAD), lambda i: (0, 0))],
        out_specs=[pl.BlockSpec((GRP, GQ_WIDTH), lambda i: (i, 0)),
                   pl.BlockSpec((GRP, GQ_KV_WIDTH), lambda i: (i, 0)),
                   pl.BlockSpec((GRP, 2 * GQ_KV_WIDTH), lambda i: (i, 0))],
        out_shape=[jax.ShapeDtypeStruct((M, GQ_WIDTH), BF16),
                   jax.ShapeDtypeStruct((M, GQ_KV_WIDTH), BF16),
                   jax.ShapeDtypeStruct((M, 2 * GQ_KV_WIDTH), BF16)],
        compiler_params=_cparams(("parallel",)),
        name="gqa_prep",
    )(z, cosf, sinf, gq.reshape(1, GQ_HEAD), gk.reshape(1, GQ_HEAD))


def _gqa_attn_kernel(q_ref, k_ref, v_ref, o_ref, *, ctx_tiles, ctx_len, n_keys):
    qi = pl.program_id(2)

    def attend(nk):
        k = k_ref[0, 0:nk, :]
        v = v_ref[0, 0:nk, :]

        def scores(h):
            return _dot_nt(q_ref[0, :, h * GQ_HEAD:(h + 1) * GQ_HEAD], k)

        def finish(h, p):
            ol = _dot(p, v)
            o = ol[:, :GQ_HEAD] / ol[:, GQ_HEAD:GQ_HEAD + 1]
            o_ref[0, :, h * GQ_HEAD:(h + 1) * GQ_HEAD] = o.astype(o_ref.dtype)

        s = scores(0)
        for h in range(GQ_GROUP):
            s_next = scores(h + 1) if h + 1 < GQ_GROUP else None
            p = jnp.exp2(s - jnp.max(s, axis=-1, keepdims=True)).astype(BF16)
            finish(h, p)
            s = s_next

    @pl.when(qi < ctx_tiles)
    def _():
        attend(ctx_len)

    @pl.when(qi >= ctx_tiles)
    def _():
        attend(n_keys)


def _gqa_attn(q, k, v, ctx_len, tq=256):
    B, T, _ = q.shape
    gw = GQ_GROUP * GQ_HEAD
    kern = functools.partial(_gqa_attn_kernel, ctx_tiles=ctx_len // tq, ctx_len=ctx_len, n_keys=T)
    return pl.pallas_call(
        kern,
        grid=(B, GQ_KV_HEADS, T // tq),
        in_specs=[pl.BlockSpec((1, tq, gw), lambda b, g, i: (b, i, g)),
                  pl.BlockSpec((1, T, GQ_HEAD), lambda b, g, i: (b, 0, g)),
                  pl.BlockSpec((1, T, 2 * GQ_HEAD), lambda b, g, i: (b, 0, g))],
        out_specs=pl.BlockSpec((1, tq, gw), lambda b, g, i: (b, i, g)),
        out_shape=jax.ShapeDtypeStruct((B, T, GQ_WIDTH), BF16),
        compiler_params=_cparams(("parallel", "parallel", "arbitrary")),
        name="gqa_attn",
    )(q, k, v)


def _na_kernel(q_ref, k_ref, v_ref, bias_ref, o_ref, qs_ref, ks_ref, vs_ref, *, ctx_len, rows):
    scale = NA_HEAD ** -0.5
    wr = min(NA_WIN_ROWS, rows)
    nwin = wr * GRID_W
    rpg = math.gcd(rows, 8)

    for hh in range(2):
        cols = slice(hh * NA_HEAD, (hh + 1) * NA_HEAD)
        qs_ref[hh] = (q_ref[0, :, cols] * scale).astype(BF16)
        ks_ref[hh] = k_ref[0, :, cols].astype(BF16)
        vs_ref[hh] = v_ref[0, :, cols].astype(BF16)

    for hh in range(2):
        cols = slice(hh * NA_HEAD, (hh + 1) * NA_HEAD)
        kc = ks_ref[hh, 0:ctx_len, :]
        vc = vs_ref[hh, 0:ctx_len, :]

        s = _dot_nt(qs_ref[hh, 0:ctx_len, :], kc)
        p = jnp.exp(s - jnp.max(s, axis=-1, keepdims=True))
        o = _dot(p.astype(BF16), vc) / jnp.sum(p, axis=-1, keepdims=True)
        o_ref[0, 0:ctx_len, cols] = o.astype(o_ref.dtype)

        def scores(g):
            i0 = g * rpg
            qoff = pl.multiple_of(ctx_len + i0 * GRID_W, rpg * GRID_W)
            qg = qs_ref[hh, pl.ds(qoff, rpg * GRID_W), :]
            s2 = _dot_nt(qg, kc)
            koffs, s1 = [], []
            for r in range(rpg):
                rs = jnp.clip(i0 + r - wr // 2, 0, rows - wr)
                koffs.append(pl.multiple_of(ctx_len + rs * GRID_W, GRID_W))
                kw = ks_ref[hh, pl.ds(koffs[r], nwin), :]
                s1.append(_dot_nt(qg[r * GRID_W:(r + 1) * GRID_W], kw) + bias_ref[hh, i0 + r - rs])
            return qoff, koffs, jnp.concatenate(s1, axis=0), s2

        def finish(qoff, koffs, s1, s2):
            m = jnp.maximum(jnp.max(s1, axis=-1, keepdims=True), jnp.max(s2, axis=-1, keepdims=True))
            p1 = jnp.exp(s1 - m)
            p2 = jnp.exp(s2 - m)
            l = jnp.sum(p1, axis=-1, keepdims=True) + jnp.sum(p2, axis=-1, keepdims=True)
            p1 = p1.astype(BF16)
            o1 = [_dot(p1[r * GRID_W:(r + 1) * GRID_W], vs_ref[hh, pl.ds(koffs[r], nwin), :]) for r in range(rpg)]
            o = (jnp.concatenate(o1, axis=0) + _dot(p2.astype(BF16), vc)) / l
            o_ref[0, pl.ds(qoff, rpg * GRID_W), cols] = o.astype(o_ref.dtype)

        ngrp = rows // rpg
        per_iter = math.gcd(ngrp, 4)

        def groups_body(it, carry):
            cur = scores(it * per_iter)
            for u in range(per_iter):
                nxt = scores(it * per_iter + u + 1) if u + 1 < per_iter else None
                finish(*cur)
                cur = nxt
            return carry

        lax.fori_loop(0, ngrp // per_iter, groups_body, 0)


def _na_bias_table(rpb, rows):
    wr = min(NA_WIN_ROWS, rows)
    cols = np.arange(GRID_W)
    cstart = np.clip(cols - NA_WIN_COLS // 2, 0, GRID_W - NA_WIN_COLS)
    c = np.arange(GRID_W)
    inside = (c[None, :] >= cstart[:, None]) & (c[None, :] < cstart[:, None] + NA_WIN_COLS)
    col_off = np.clip(c[None, :] - cols[:, None] + (NA_WIN_COLS - 1), 0, 2 * NA_WIN_COLS - 2)
    onehot = (np.arange(2 * NA_WIN_COLS - 1)[:, None, None] == col_off[None]).astype(np.float32)
    tc = jnp.einsum('...ro,ojc->...rjc', rpb, onehot, precision=lax.Precision.HIGHEST)
    tc = jnp.where(inside, tc, NEG_BIG)
    t = jnp.stack([tc[..., NA_WIN_ROWS - 1 - d:NA_WIN_ROWS - 1 - d + wr, :, :] for d in range(wr)], axis=-4)
    t = jnp.swapaxes(t, -3, -2)
    return t.reshape(rpb.shape[:-2] + (wr, GRID_W, wr * GRID_W))


def _na_attn(z, bias, l, ctx_len):
    B, T, _ = z.shape
    rows = (T - ctx_len) // GRID_W
    wr = bias.shape[2]
    pw = 2 * NA_HEAD
    c0 = ZN_OFF // pw
    per = NA_WIDTH // pw
    scr = pltpu.VMEM((2, T, NA_HEAD), BF16)
    return pl.pallas_call(
        functools.partial(_na_kernel, ctx_len=ctx_len, rows=rows),
        grid=(B, per),
        in_specs=[pl.BlockSpec((1, T, pw), lambda b, h: (b, 0, c0 + h)),
                  pl.BlockSpec((1, T, pw), lambda b, h: (b, 0, c0 + per + h)),
                  pl.BlockSpec((1, T, pw), lambda b, h: (b, 0, c0 + 2 * per + h)),
                  pl.BlockSpec((None, 2, wr, GRID_W, wr * GRID_W), lambda b, h: (l, h, 0, 0, 0))],
        out_specs=pl.BlockSpec((1, T, pw), lambda b, h: (b, 0, h)),
        out_shape=jax.ShapeDtypeStruct((B, T, NA_WIDTH), BF16),
        scratch_shapes=[scr, scr, scr],
        compiler_params=_cparams(("parallel", "parallel")),
        name="na_attn",
    )(z, z, z, bias)


RW_PAIRS = RW_HEADS // 2
PAIR_W = 2 * RW_HEAD
PC_ROWS = 8
HALO = 16


def _head0_lanes(shape):
    return lax.broadcasted_iota(jnp.int32, shape, len(shape) - 1) < RW_HEAD


def _per_head_rows(x):
    m0 = _head0_lanes(x.shape)
    return jnp.concatenate([jnp.where(m0, x, 0.0), jnp.where(m0, 0.0, x)], axis=-2)


def _pair_diag(full):
    half = full.shape[-2] // 2
    top, bot = full[..., :half, :], full[..., half:, :]
    return jnp.where(_head0_lanes(top.shape), top, bot)


def _rw_prep_kernel(z_ref, zp_ref, zn_ref, l_ref, lp_ref, ln_ref,
                    mup_ref, mun_ref, lmup_ref, lmun_ref, kk_ref_, ka_ref, rk_ref, w0_ref, wup_ref, a0_ref,
                    aup_ref, gup_ref,
                    v_out, g_out, bonus_out, ra_out, at_out, bt_out, kt_out, pc_out, *, gps, ctx_groups, lora):
    t = pl.program_id(1)
    first = jnp.logical_or(t == 0, t == ctx_groups)
    last = jnp.logical_or(t == ctx_groups - 1, t == gps - 1)
    pv = jnp.where(first, 0.0, 1.0)
    nv = jnp.where(last, 0.0, 1.0)
    row = lax.broadcasted_iota(jnp.int32, (GRP, 1), 0)
    C = SCAN_CHUNK

    def shift(x_ref, p_ref, n_ref, mp_ref, mn_ref):
        x = x_ref[...].astype(F32)
        xp = jnp.where(row == 0, p_ref[HALO - 1:HALO, :].astype(F32) * pv, pltpu.roll(x, 1, axis=0))
        xn = jnp.where(row == GRP - 1, n_ref[0:1, :].astype(F32) * nv, pltpu.roll(x, GRP - 1, axis=0))
        return x + mp_ref[...] * (xp - x) + mn_ref[...] * (xn - x)

    zs = shift(z_ref, zp_ref, zn_ref, mup_ref, mun_ref)
    ls = shift(l_ref, lp_ref, ln_ref, lmup_ref, lmun_ref)
    r = zs[:, 0:RW_WIDTH]
    k = zs[:, RW_WIDTH:2 * RW_WIDTH]
    v = zs[:, 2 * RW_WIDTH:3 * RW_WIDTH]

    ci = lax.broadcasted_iota(jnp.int32, (PAIR_W, PAIR_W), 0) // RW_HEAD
    cj = lax.broadcasted_iota(jnp.int32, (PAIR_W, PAIR_W), 1) // RW_HEAD
    head_ones = jnp.where(ci == cj, 1.0, 0.0).astype(BF16)

    def head_sum(x):
        parts = _split3(x)
        return jnp.concatenate([sum(_dot(part[:, p * PAIR_W:(p + 1) * PAIR_W], head_ones) for part in parts)
                                for p in range(RW_PAIRS)], axis=1)

    def dot_f32(x, w):
        x1, x2, _ = _split3(x)
        w1, w2, _ = _split3(w)
        return _dot(x1, w1) + (_dot(x2, w1) + _dot(x1, w2))

    kk = k * kk_ref_[...]
    kk = kk / jnp.maximum(jnp.sqrt(head_sum(kk * kk)), 1e-12)

    ti = lax.broadcasted_iota(jnp.int32, (GRP, GRP), 0)
    tj = lax.broadcasted_iota(jnp.int32, (GRP, GRP), 1)
    same = (ti // C) == (tj // C)

    v_out[...] = v.astype(BF16)
    gd = ls[:, LORA_GD:LORA_GD + gup_ref.shape[0]]
    g_out[...] = _dot(jax.nn.sigmoid(gd).astype(BF16), gup_ref[...].astype(BF16))

    lane = lax.broadcasted_iota(jnp.int32, (1, LORA_IN), 1)
    w_lin = dot_f32(jnp.where(lane < 2 * lora, jnp.tanh(ls[:, :LORA_IN]), 0.0), wup_ref[...])
    a_lin = _dot(ls[:, LORA_IN // 2:LORA_IN // 2 + LORA_IN].astype(BF16), aup_ref[...].astype(BF16))

    kd_sum = jnp.zeros_like(k)
    for d in range(2):
        dcols = slice(d * RW_WIDTH, (d + 1) * RW_WIDTH)
        w_raw = w0_ref[d] + w_lin[:, dcols]
        lw = -math.exp(-0.5) * jax.nn.sigmoid(w_raw)
        a = jax.nn.sigmoid(a0_ref[d] + a_lin[:, dcols])
        kd = k * (1.0 + (a - 1.0) * ka_ref[...])
        kd_sum = kd_sum + kd
        tri = jnp.where(jnp.logical_and(same, (ti >= tj) if d == 0 else (ti <= tj)), 1.0, 0.0).astype(BF16)
        cs = sum(_dot(tri, part) for part in _split3(lw))
        pinv = jnp.exp(-cs)
        ra_out[d] = (r * jnp.exp(cs)).astype(BF16)
        at_out[d] = (-kk * jnp.exp(cs - lw)).astype(BF16)
        bt_out[d] = (kk * a * pinv).astype(BF16)
        kt_out[d] = (kd * pinv).astype(BF16)
        for c in range(GRP // C):
            end = c * C + (C - 1 if d == 0 else 0)
            pc_out[d, c * PC_ROWS:(c + 1) * PC_ROWS, :] = jnp.broadcast_to(jnp.exp(cs[end:end + 1, :]),
                                                                          (PC_ROWS, RW_WIDTH))
    bonus_out[...] = head_sum(r * kd_sum * rk_ref[...]) * v


def _rw_prep(z, p, B, gps, ctx_groups, lora):
    M = z.shape[0]
    nb8 = M // HALO
    rb = GRP // HALO
    zc = ZRKV_OFF // 1536
    lc = ZL_OFF // ZL_W
    pcr = GRP // SCAN_CHUNK * PC_ROWS

    def cur(c):
        return lambda b, t: (b * gps + t, c)

    def prev(c):
        return lambda b, t: (jnp.maximum((b * gps + t) * rb - 1, 0), c)

    def nxt(c):
        return lambda b, t: (jnp.minimum((b * gps + t + 1) * rb, nb8 - 1), c)

    def const(shape):
        nd = len(shape)
        return pl.BlockSpec(shape, lambda b, t: (0,) * nd)

    tok = pl.BlockSpec((GRP, RW_WIDTH), lambda b, t: (b * gps + t, 0))
    tok2 = pl.BlockSpec((2, GRP, RW_WIDTH), lambda b, t: (0, b * gps + t, 0))
    s1 = jax.ShapeDtypeStruct((M, RW_WIDTH), F32)
    s2 = jax.ShapeDtypeStruct((2, M, RW_WIDTH), BF16)
    return pl.pallas_call(
        functools.partial(_rw_prep_kernel, gps=gps, ctx_groups=ctx_groups, lora=lora),
        grid=(B, gps),
        in_specs=[pl.BlockSpec((GRP, 1536), cur(zc)),
                  pl.BlockSpec((HALO, 1536), prev(zc)),
                  pl.BlockSpec((HALO, 1536), nxt(zc)),
                  pl.BlockSpec((GRP, ZL_W), cur(lc)),
                  pl.BlockSpec((HALO, ZL_W), prev(lc)),
                  pl.BlockSpec((HALO, ZL_W), nxt(lc)),
                  const((1, 1536)), const((1, 1536)), const((1, ZL_W)), const((1, ZL_W)),
                  const((1, RW_WIDTH)), const((1, RW_WIDTH)), const((1, RW_WIDTH)),
                  const((2, 1, RW_WIDTH)), const((LORA_IN, 2 * RW_WIDTH)),
                  const((2, 1, RW_WIDTH)), const((LORA_IN, 2 * RW_WIDTH)),
                  const(p['g_up'].shape)],
        out_specs=[tok, tok, tok, tok2, tok2, tok2, tok2,
                   pl.BlockSpec((2, pcr, RW_WIDTH), lambda b, t: (0, b * gps + t, 0))],
        out_shape=[jax.ShapeDtypeStruct((M, RW_WIDTH), BF16), s1, s1, s2, s2, s2, s2,
                   jax.ShapeDtypeStruct((2, M // GRP * pcr, RW_WIDTH), F32)],
        compiler_params=_cparams(("parallel", "parallel")),
        name="rw_prep",
    )(z, z, z, z, z, z, p['mu_p'], p['mu_n'], p['lmu_p'], p['lmu_n'], p['k_k'], p['k_a'], p['r_k'],
      p['w0'], p['w_up'], p['a0'], p['a_up'], p['g_up'])


def _bmm(a, b, nt=False):
    a = a.astype(BF16)
    b = b.astype(BF16)
    out = []
    for h in range(a.shape[0]):
        out.append(_dot_nt(a[h], b[h]) if nt else _dot(a[h], b[h]))
    return jnp.stack(out, axis=0)


def _bmm_tn(a, b):
    a = a.astype(BF16)
    b = b.astype(BF16)
    out = []
    for h in range(a.shape[0]):
        out.append(_dot(a[h], b[h], (((0,), (0,)), ((), ()))))
    return jnp.stack(out, axis=0)


def _rw_intra_kernel(ra_ref, at_ref, bt_ref, kt_ref, v_ref, pc_ref, rh_ref, o0_ref, gm_ref, sd_ref, *, nck):
    d = pl.program_id(1)
    C, P = SCAN_CHUNK, RW_PAIRS

    def pairs(x):
        x = x.reshape(nck, -1, RW_WIDTH)
        return jnp.concatenate([x[:, :, p * PAIR_W:(p + 1) * PAIR_W] for p in range(P)], axis=0)

    def unpairs(y, like):
        rows = y.shape[1] * nck
        y = jnp.concatenate([y[p * nck:(p + 1) * nck].reshape(rows, PAIR_W) for p in range(P)], axis=1)
        return y.reshape(like.shape[1:]).astype(like.dtype)

    ti = lax.broadcasted_iota(jnp.int32, (C, PAIR_W), 0)
    tj = lax.broadcasted_iota(jnp.int32, (C, PAIR_W), 1) % C
    before = (ti - tj) * jnp.where(d == 0, 1, -1)
    strict = (before > 0)[None]
    incl = (before >= 0)[None]
    eye = (ti == tj)[None]

    ra = pairs(ra_ref[0])
    at = pairs(at_ref[0])
    bt = pairs(bt_ref[0])
    kt = pairs(kt_ref[0])
    v = pairs(v_ref[...])
    pc = pairs(pc_ref[0])[:, 0:1, :]

    lhs = jnp.concatenate([at, ra], axis=1)
    rhs = jnp.concatenate([_per_head_rows(bt), _per_head_rows(kt)], axis=1)
    gmat = _bmm(lhs, rhs, nt=True)
    a_ab = jnp.where(strict, gmat[:, :C, :PAIR_W], 0.0)
    a_ak = jnp.where(strict, gmat[:, :C, PAIR_W:], 0.0)
    a_rb = jnp.where(incl, gmat[:, C:, :PAIR_W], 0.0)
    a_rk = jnp.where(incl, gmat[:, C:, PAIR_W:], 0.0)

    tm = jnp.where(eye, 1.0, 0.0) + a_ab
    pw = a_ab
    for _ in range(int(math.log2(C)) - 1):
        pw = _bmm(pw, _per_head_rows(pw))
        tm = tm + _bmm(tm, _per_head_rows(pw))

    vrows = _per_head_rows(v)
    x0 = _bmm(a_ak, vrows)
    au = _bmm(tm, jnp.concatenate([_per_head_rows(at), _per_head_rows(x0)], axis=2))
    ah, u0 = au[:, :, :PAIR_W], au[:, :, PAIR_W:]
    upper = jnp.concatenate([_per_head_rows(ah), _per_head_rows(u0)], axis=2)
    lower = jnp.concatenate([jnp.zeros_like(vrows), vrows], axis=2)
    ro = _bmm(jnp.concatenate([a_rb, a_rk], axis=2),
              jnp.concatenate([upper, lower], axis=1))
    bh = bt * pc
    kh = kt * pc
    rh_ref[0] = unpairs(ra + ro[:, :, :PAIR_W], rh_ref)
    o0_ref[0] = unpairs(ro[:, :, PAIR_W:], o0_ref)
    gm_ref[0] = unpairs(_pair_diag(_bmm_tn(ah, bh)), gm_ref)
    sd_ref[0] = unpairs(_pair_diag(_bmm_tn(jnp.concatenate([u0, v], axis=1), jnp.concatenate([bh, kh], axis=1))),
                        sd_ref)


def _rw_seq_kernel(rhf_ref, o0f_ref, gmf_ref, sdf_ref, pcf_ref, rhb_ref, o0b_ref, gmb_ref, sdb_ref, pcb_ref,
                   of_ref, ob_ref, s_ref, *, nck):
    C = SCAN_CHUNK

    @pl.when(pl.program_id(0) == 0)
    def _():
        s_ref[...] = jnp.zeros(s_ref.shape, F32)

    dirs = ((rhf_ref, o0f_ref, gmf_ref, sdf_ref, pcf_ref, of_ref, range(nck)),
            (rhb_ref, o0b_ref, gmb_ref, sdb_ref, pcb_ref, ob_ref, range(nck - 1, -1, -1)))
    chains = [(d, b, p) for d in range(2) for b in range(s_ref.shape[1]) for p in range(RW_PAIRS)]
    state = {ch: s_ref[ch] for ch in chains}
    for step in range(nck):
        for ch in chains:
            d, b, p = ch
            rh_ref, o0_ref, gm_ref, sd_ref, pc_ref, o_ref, order = dirs[d]
            c = order[step]
            rows = slice(c * C, (c + 1) * C)
            cols = slice(p * PAIR_W, (p + 1) * PAIR_W)
            s = state[ch]
            sb = s.astype(BF16)
            o_ref[b, rows, cols] = _dot_nt(rh_ref[0, b, rows, cols], _per_head_rows(sb)) + o0_ref[0, b, rows, cols]
            state[ch] = (s * pc_ref[0, b, c * PC_ROWS:c * PC_ROWS + 1, cols]
                         + _dot(sb, _per_head_rows(gm_ref[0, b, rows, cols])) + sd_ref[0, b, rows, cols])
    for ch in chains:
        s_ref[ch] = state[ch]


def _rw_scan(ra, at, bt, kt, v, pc, B, ctx_len, nck=4):
    M = v.shape[0]
    T = M // B
    W = RW_WIDTH
    ct = nck * SCAN_CHUNK
    nblk = T // ct
    ncb = ctx_len // ct
    pcr = nck * PC_ROWS
    assert ctx_len % ct == 0 and T % ct == 0

    def view(x, rows):
        return x.reshape(2, B, rows, W)

    bb = 2 if B % 2 == 0 else 1
    per_dir = pl.BlockSpec((1, bb, ct, W), lambda b, d, j: (d, b, j, 0))
    outs = [jax.ShapeDtypeStruct((2, B, T, W), BF16), jax.ShapeDtypeStruct((2, B, T, W), F32),
            jax.ShapeDtypeStruct((2, B, T, W), BF16), jax.ShapeDtypeStruct((2, B, T, W), F32)]
    pcv = view(pc, T // SCAN_CHUNK * PC_ROWS)
    big = pl.pallas_call(
        functools.partial(_rw_intra_kernel, nck=bb * nck),
        grid=(B // bb, 2, nblk),
        in_specs=[per_dir, per_dir, per_dir, per_dir,
                  pl.BlockSpec((bb, ct, W), lambda b, d, j: (b, j, 0)),
                  pl.BlockSpec((1, bb, pcr, W), lambda b, d, j: (d, b, j, 0))],
        out_specs=[per_dir, per_dir, per_dir, per_dir],
        out_shape=outs,
        compiler_params=_cparams(("parallel", "parallel", "parallel")),
        name="rw_intra",
    )(view(ra, T), view(at, T), view(bt, T), view(kt, T), v.reshape(B, T, W), pcv)

    def bwd_blk(j):
        return jnp.where(j < ncb, ncb - 1 - j, nblk - 1 - (j - ncb))

    f_big = pl.BlockSpec((1, B, ct, W), lambda j: (0, 0, j, 0))
    b_big = pl.BlockSpec((1, B, ct, W), lambda j: (1, 0, bwd_blk(j), 0))
    f_pc = pl.BlockSpec((1, B, pcr, W), lambda j: (0, 0, j, 0))
    b_pc = pl.BlockSpec((1, B, pcr, W), lambda j: (1, 0, bwd_blk(j), 0))
    out = jax.ShapeDtypeStruct((B, T, W), F32)
    return pl.pallas_call(
        functools.partial(_rw_seq_kernel, nck=nck),
        grid=(nblk,),
        in_specs=[f_big, f_big, f_big, f_big, f_pc, b_big, b_big, b_big, b_big, b_pc],
        out_specs=[pl.BlockSpec((B, ct, W), lambda j: (0, j, 0)),
                   pl.BlockSpec((B, ct, W), lambda j: (0, bwd_blk(j), 0))],
        out_shape=[out, out],
        scratch_shapes=[pltpu.VMEM((2, B, RW_PAIRS, RW_HEAD, PAIR_W), F32)],
        compiler_params=_cparams(("arbitrary",)),
        name="rw_seq",
    )(*big, pcv, *big, pcv)


def _rw_out_kernel(of_ref, ob_ref, g_ref, bonus_ref, lnw_ref, lnb_ref, y_ref):
    ci = lax.broadcasted_iota(jnp.int32, (PAIR_W, PAIR_W), 0) // RW_HEAD
    cj = lax.broadcasted_iota(jnp.int32, (PAIR_W, PAIR_W), 1) // RW_HEAD
    head_mean = jnp.where(ci == cj, 1.0 / RW_HEAD, 0.0).astype(BF16)

    def mean(x):
        return sum(_dot(part, head_mean) for part in _split3(x))

    for p in range(RW_PAIRS):
        cols = slice(p * PAIR_W, (p + 1) * PAIR_W)
        wkv = of_ref[:, cols] + ob_ref[:, cols]
        cen = wkv - mean(wkv)
        y = cen * lax.rsqrt(mean(cen * cen) + RW_GN_EPS) * lnw_ref[:, cols] + lnb_ref[:, cols] + bonus_ref[:, cols]
        y_ref[:, cols] = (y * g_ref[:, cols]).astype(y_ref.dtype)


def _rw_out(o_f, o_b, g, bonus, lnw, lnb):
    M, W = g.shape
    tm = _tile(M, 512)
    tok = pl.BlockSpec((tm, W), lambda i: (i, 0))
    par = pl.BlockSpec((1, W), lambda i: (0, 0))
    return pl.pallas_call(
        _rw_out_kernel,
        grid=(M // tm,),
        in_specs=[tok, tok, tok, tok, par, par],
        out_specs=tok,
        out_shape=jax.ShapeDtypeStruct((M, W), BF16),
        compiler_params=_cparams(("parallel",)),
        name="rw_out",
    )(o_f, o_b, g, bonus, lnw.reshape(1, W), lnb.reshape(1, W))


def _pad_cols(w, n):
    return jnp.pad(w, [(0, 0)] * (w.ndim - 1) + [(0, n - w.shape[-1])])


def _in_layout(w, lw):
    o = 1536 + lw
    parts = [w[..., o + 3072:], w[..., 0:1536], w[..., o:o + 3072], _pad_cols(w[..., 1536:o], ZL_W)]
    return jnp.concatenate([x.astype(BF16) for x in parts], axis=-1)


def _shift_layout(mu, lw):
    return mu[..., 0:1536], _pad_cols(mu[..., 1536:1536 + lw], ZL_W)


def _lora_up(up, row0, rows):
    lora, w = up.shape[2], up.shape[3]
    parts = [jnp.pad(up[:, d], ((0, 0), (row0 + d * lora, rows - row0 - (d + 1) * lora), (d * w, (1 - d) * w)))
             for d in range(2)]
    return parts[0] + parts[1]


def _tile(n, pref):
    t = pref
    while n % t:
        t //= 2
    return t


def _axial_tables(ctx_len, seq):
    n_freq = GQ_HEAD // 4
    inv = ROPE_THETA ** (-jnp.arange(n_freq, dtype=F32) / n_freq)
    t = jnp.arange(seq, dtype=jnp.int32)
    row = (t // GRID_W).astype(F32)
    col = (t % GRID_W).astype(F32)
    ang = jnp.concatenate([row[:, None] * inv, col[:, None] * inv], axis=-1)
    cos, sin = jnp.cos(ang), jnp.sin(ang)
    cosf = jnp.concatenate([jnp.ones((ctx_len, GQ_HEAD), F32), jnp.repeat(cos, 2, axis=-1)], axis=0)
    sinf = jnp.concatenate([jnp.zeros((ctx_len, GQ_HEAD), F32),
                            jnp.stack([-sin, sin], axis=-1).reshape(seq, GQ_HEAD)], axis=0)
    return cosf, sinf


def kernel(x, c, ctx, c_ctx, ada_w, ada_b, norm1, norm2, w_in, rw_mu_prev, rw_mu_next, rw_w0, rw_w_up, rw_a0, rw_a_up, rw_g_up, rw_k_k, rw_k_a, rw_r_k, rw_ln_w, rw_ln_b, gq_q_norm, gq_k_norm, na_rpb, w_br_a, w_br_b, w_br_c, w_out, ffn_w1, ffn_w3, ffn_w2, final_norm):
    B, S, D = x.shape
    C = ctx.shape[1]
    L = ada_w.shape[0]
    T = C + S
    M = B * T
    gps = T // GRP
    cgr = C // GRP
    lora = rw_w_up.shape[2]
    assert C % GRP == 0 and S % GRP == 0 and S % GRID_W == 0
    lw = 4 * lora + rw_g_up.shape[1]
    assert w_in.shape[2] == 1536 + lw + 1536 + 1536 + 3 * D
    assert lw <= ZL_W and 2 * lora <= LORA_IN and 2 * lora >= LORA_IN // 2 and 4 * lora == LORA_GD
    tm = _tile(M, 1024)
    tm_in = next(t for t in (2176, 1024, 512, 256) if M % t == 0)

    w_in_p = _in_layout(w_in, lw)
    mu_p, lmu_p = _shift_layout(rw_mu_prev, lw)
    mu_n, lmu_n = _shift_layout(rw_mu_next, lw)
    w_up = _lora_up(rw_w_up, 0, LORA_IN)
    a_up = _lora_up(rw_a_up, 2 * lora - LORA_IN // 2, LORA_IN)
    gq, gk = gq_q_norm, gq_k_norm
    wa = w_br_a.astype(BF16)
    wb = w_br_b.astype(BF16)
    wc = w_br_c.astype(BF16)
    wo = w_out.astype(BF16)
    w1 = ffn_w1.astype(BF16)
    w3 = ffn_w3.astype(BF16)
    w2 = ffn_w2.astype(BF16)
    cosf, sinf = _axial_tables(C, S)
    na_bias = _na_bias_table(na_rpb, S // GRID_W)

    cc = jnp.concatenate([c, c_ctx[None, :], jnp.zeros((-(B + 1) % 8, D), F32)], axis=0)
    mod = _ada_mod(cc, ada_w, ada_b)
    modg = jnp.concatenate([jnp.broadcast_to(mod[:, B, None, None, :], (L, B, cgr, 6 * D)),
                            jnp.broadcast_to(mod[:, :B, None, :], (L, B, gps - cgr, 6 * D))], axis=2)
    modg = modg.reshape(L, B * gps, 1, 6 * D)

    xt = jnp.concatenate([ctx, x], axis=1).reshape(M, D)

    for l in range(L):
        mg = modg[l]
        h = _modnorm(xt, norm1[l], mg, 0, 1)
        z = _matmul(h, w_in_p, l, tm_in, 768, BF16, "in_proj")

        pr = {'mu_p': mu_p[l][None], 'mu_n': mu_n[l][None], 'lmu_p': lmu_p[l][None], 'lmu_n': lmu_n[l][None],
              'k_k': rw_k_k[l][None], 'k_a': rw_k_a[l][None], 'r_k': rw_r_k[l].reshape(1, RW_WIDTH),
              'w0': rw_w0[l][:, None, :], 'w_up': w_up[l], 'a0': rw_a0[l][:, None, :], 'a_up': a_up[l],
              'g_up': rw_g_up[l]}
        v_, g_, bonus_, ra_, at_, bt_, kt_, pc_ = _rw_prep(z, pr, B, gps, cgr, lora)
        of_, ob_ = _rw_scan(ra_, at_, bt_, kt_, v_, pc_, B, C)
        ya = _rw_out(of_.reshape(M, RW_WIDTH), ob_.reshape(M, RW_WIDTH), g_, bonus_, rw_ln_w[l], rw_ln_b[l])

        q, k, v = _gqa_prep(z, cosf, sinf, gq[l], gk[l], gps)
        yb = _gqa_attn(q.reshape(B, T, GQ_WIDTH), k.reshape(B, T, GQ_KV_WIDTH), v.reshape(B, T, 2 * GQ_KV_WIDTH), C)
        yb = yb.reshape(M, GQ_WIDTH)

        yc = _na_attn(z.reshape(B, T, Z_W), na_bias, l, C).reshape(M, NA_WIDTH)

        m = _merge(ya, yb, yc, wa, wb, wc, l, z, tm, 1024)
        xt = _matmul_res(m, wo, l, xt, mg, 2, tm, 1024, "out_proj")
        h = _modnorm(xt, norm2[l], mg, 3, 4)
        u = _ffn_up(h, w1, w3, l, tm, 512)
        xt = _matmul_res(u, w2, l, xt, mg, 5, tm, 512, "ffn_down")

    return _final_norm(xt.reshape(B, T, D), final_norm, C)
```

```python
import functools
import math

import numpy as np
import jax
import jax.numpy as jnp
from jax import lax
from jax.experimental import pallas as pl
from jax.experimental.pallas import tpu as pltpu

F32 = jnp.float32
BF16 = jnp.bfloat16

GRP = 256
GRID_W = 64
NORM_EPS = 1e-6

RW_HEAD = 64
RW_HEADS = 8
RW_WIDTH = RW_HEAD * RW_HEADS
LORA_IN = 256
LORA_GD = 384
RW_GN_EPS = 64e-5
SCAN_CHUNK = 64

GQ_HEAD = 128
GQ_HEADS = 8
GQ_KV_HEADS = 2
GQ_GROUP = GQ_HEADS // GQ_KV_HEADS
GQ_WIDTH = GQ_HEAD * GQ_HEADS
GQ_KV_WIDTH = GQ_HEAD * GQ_KV_HEADS
ROPE_THETA = 10000.0

NA_HEAD = 64
NA_HEADS = 8
NA_WIDTH = NA_HEAD * NA_HEADS
NA_WIN_ROWS = 8
NA_WIN_COLS = 16
NEG_BIG = -1e30

VMEM_LIMIT = 56 * 1024 * 1024

ZG_OFF, ZG_W = 0, 6144
ZRKV_OFF = 6144
ZQ_OFF = ZRKV_OFF + 1536
ZN_OFF = ZQ_OFF + 1536
ZL_OFF = ZN_OFF + 1536
ZL_W = 768
Z_W = ZL_OFF + ZL_W


def _cparams(sem):
    return pltpu.CompilerParams(dimension_semantics=sem, vmem_limit_bytes=VMEM_LIMIT)


def _dot(a, b, dims=None):
    if dims is None:
        dims = (((a.ndim - 1,), (0,)), ((), ()))
    return lax.dot_general(a, b, dims, preferred_element_type=F32)


def _dot_nt(a, b):
    return _dot(a, b, (((1,), (1,)), ((), ())))


def _split3(x):
    x1 = x.astype(BF16)
    r1 = x - x1.astype(F32)
    x2 = r1.astype(BF16)
    x3 = (r1 - x2.astype(F32)).astype(BF16)
    return x1, x2, x3


def _ada_kernel(c_ref, w_ref, b_ref, o_ref):
    c = c_ref[...]
    a = (c * jax.nn.sigmoid(c)).astype(BF16)
    o_ref[0] = _dot(a, w_ref[0].astype(BF16)) + b_ref[0]


def _ada_mod(cc, ada_w, ada_b):
    L, D, N = ada_w.shape
    R = cc.shape[0]
    tn = 1536
    return pl.pallas_call(
        _ada_kernel,
        grid=(L, N // tn),
        in_specs=[pl.BlockSpec((R, D), lambda l, j: (0, 0)),
                  pl.BlockSpec((1, D, tn), lambda l, j: (l, 0, j)),
                  pl.BlockSpec((1, 1, tn), lambda l, j: (l, 0, j))],
        out_specs=pl.BlockSpec((1, R, tn), lambda l, j: (l, 0, j)),
        out_shape=jax.ShapeDtypeStruct((L, R, N), F32),
        compiler_params=_cparams(("parallel", "parallel")),
        name="ada_mod",
    )(cc, ada_w, ada_b.reshape(L, 1, N))


def _modnorm_kernel(x_ref, nw_ref, sh_ref, sc_ref, h_ref, *, groups):
    for g in range(groups):
        rows = pl.ds(g * GRP, GRP)
        x = x_ref[rows, :]
        y = x * lax.rsqrt(jnp.mean(x * x, axis=-1, keepdims=True) + NORM_EPS) * nw_ref[...]
        h_ref[rows, :] = (y * (1.0 + sc_ref[g]) + sh_ref[g]).astype(BF16)


def _modnorm(x, nw, modg, sh_blk, sc_blk):
    M, D = x.shape
    tm = _tile(M, 1024)
    groups = tm // GRP
    return pl.pallas_call(
        functools.partial(_modnorm_kernel, groups=groups),
        grid=(M // tm,),
        in_specs=[pl.BlockSpec((tm, D), lambda i: (i, 0)),
                  pl.BlockSpec((1, D), lambda i: (0, 0)),
                  pl.BlockSpec((groups, 1, D), lambda i: (i, 0, sh_blk)),
                  pl.BlockSpec((groups, 1, D), lambda i: (i, 0, sc_blk))],
        out_specs=pl.BlockSpec((tm, D), lambda i: (i, 0)),
        out_shape=jax.ShapeDtypeStruct((M, D), BF16),
        compiler_params=_cparams(("parallel",)),
        name="modnorm",
    )(x, nw.reshape(1, D), modg, modg)


def _mm_kernel(a_ref, b_ref, o_ref):
    o_ref[...] = _dot(a_ref[...], b_ref[...]).astype(o_ref.dtype)


def _layer_spec(w, l, tn):
    return pl.BlockSpec((None, w.shape[1], tn), lambda i, j: (l, 0, j))


def _matmul(a, b, l, tm, tn, out_dtype, name):
    M, K = a.shape
    N = b.shape[2]
    return pl.pallas_call(
        _mm_kernel,
        grid=(M // tm, N // tn),
        in_specs=[pl.BlockSpec((tm, K), lambda i, j: (i, 0)),
                  _layer_spec(b, l, tn)],
        out_specs=pl.BlockSpec((tm, tn), lambda i, j: (i, j)),
        out_shape=jax.ShapeDtypeStruct((M, N), out_dtype),
        compiler_params=_cparams(("parallel", "parallel")),
        name=name,
    )(a, b)


def _mm_res_kernel(a_ref, b_ref, res_ref, gate_ref, o_ref, *, groups):
    acc = _dot(a_ref[...], b_ref[...])
    for g in range(groups):
        rows = slice(g * GRP, (g + 1) * GRP)
        o_ref[rows, :] = res_ref[rows, :] + gate_ref[g] * acc[rows, :]


def _matmul_res(a, b, l, res, modg, gate_blk, tm, tn, name):
    M, K = a.shape
    N = b.shape[2]
    groups = tm // GRP
    nb = N // tn
    return pl.pallas_call(
        functools.partial(_mm_res_kernel, groups=groups),
        grid=(M // tm, nb),
        in_specs=[pl.BlockSpec((tm, K), lambda i, j: (i, 0)),
                  _layer_spec(b, l, tn),
                  pl.BlockSpec((tm, tn), lambda i, j: (i, j)),
                  pl.BlockSpec((groups, 1, tn), lambda i, j: (i, 0, gate_blk * nb + j))],
        out_specs=pl.BlockSpec((tm, tn), lambda i, j: (i, j)),
        out_shape=jax.ShapeDtypeStruct((M, N), F32),
        compiler_params=_cparams(("parallel", "parallel")),
        name=name,
    )(a, b, res, modg)


def _ffn_up_kernel(h_ref, w1_ref, w3_ref, o_ref):
    h = h_ref[...]
    a = _dot(h, w1_ref[...])
    b = _dot(h, w3_ref[...])
    o_ref[...] = (a * jax.nn.sigmoid(a) * b).astype(o_ref.dtype)


def _ffn_up(h, w1, w3, l, tm, tn):
    M, K = h.shape
    N = w1.shape[2]
    return pl.pallas_call(
        _ffn_up_kernel,
        grid=(M // tm, N // tn),
        in_specs=[pl.BlockSpec((tm, K), lambda i, j: (i, 0)),
                  _layer_spec(w1, l, tn),
                  _layer_spec(w3, l, tn)],
        out_specs=pl.BlockSpec((tm, tn), lambda i, j: (i, j)),
        out_shape=jax.ShapeDtypeStruct((M, N), BF16),
        compiler_params=_cparams(("parallel", "parallel")),
        name="ffn_up",
    )(h, w1, w3)


def _merge_kernel(ya_ref, yb_ref, yc_ref, wa_ref, wb_ref, wc_ref, ga_ref, gb_ref, gc_ref, o_ref):
    def sig(x):
        return 0.5 * jnp.tanh(0.5 * x) + 0.5

    m = sig(ga_ref[...].astype(F32)) * _dot(ya_ref[...], wa_ref[...])
    m = m + sig(gb_ref[...].astype(F32)) * _dot(yb_ref[...], wb_ref[...])
    m = m + sig(gc_ref[...].astype(F32)) * _dot(yc_ref[...], wc_ref[...])
    o_ref[...] = m.astype(o_ref.dtype)


def _merge(ya, yb, yc, wa, wb, wc, l, z, tm, tn):
    M = ya.shape[0]
    D = wa.shape[2]
    nb = D // tn
    g0 = ZG_OFF // tn
    return pl.pallas_call(
        _merge_kernel,
        grid=(M // tm, nb),
        in_specs=[pl.BlockSpec((tm, ya.shape[1]), lambda i, j: (i, 0)),
                  pl.BlockSpec((tm, yb.shape[1]), lambda i, j: (i, 0)),
                  pl.BlockSpec((tm, yc.shape[1]), lambda i, j: (i, 0)),
                  _layer_spec(wa, l, tn), _layer_spec(wb, l, tn), _layer_spec(wc, l, tn),
                  pl.BlockSpec((tm, tn), lambda i, j: (i, g0 + j)),
                  pl.BlockSpec((tm, tn), lambda i, j: (i, g0 + nb + j)),
                  pl.BlockSpec((tm, tn), lambda i, j: (i, g0 + 2 * nb + j))],
        out_specs=pl.BlockSpec((tm, tn), lambda i, j: (i, j)),
        out_shape=jax.ShapeDtypeStruct((M, D), BF16),
        compiler_params=_cparams(("parallel", "parallel")),
        name="merge",
    )(ya, yb, yc, wa, wb, wc, z, z, z)


def _final_norm_kernel(x_ref, w_ref, o_ref):
    x = x_ref[0]
    o_ref[0] = x * lax.rsqrt(jnp.mean(x * x, axis=-1, keepdims=True) + NORM_EPS) * w_ref[...]


def _final_norm(xt, w, ctx_len):
    B, T, D = xt.shape
    S = T - ctx_len
    cg = ctx_len // GRP
    return pl.pallas_call(
        _final_norm_kernel,
        grid=(B, S // GRP),
        in_specs=[pl.BlockSpec((1, GRP, D), lambda b, i: (b, cg + i, 0)),
                  pl.BlockSpec((1, D), lambda b, i: (0, 0))],
        out_specs=pl.BlockSpec((1, GRP, D), lambda b, i: (b, i, 0)),
        out_shape=jax.ShapeDtypeStruct((B, S, D), F32),
        compiler_params=_cparams(("parallel", "parallel")),
        name="final_norm",
    )(xt, w.reshape(1, D))


def _gqa_prep_kernel(z_ref, cos_ref, sin_ref, gq_ref, gk_ref, q_ref, k_ref, v_ref):
    cosf = cos_ref[...]
    sinf = sin_ref[...]

    even = lax.broadcasted_iota(jnp.int32, (1, GQ_HEAD), 1) % 2 == 0

    def norm_rope(x, gain):
        x = x.astype(F32)
        y = x * lax.rsqrt(jnp.mean(x * x, axis=-1, keepdims=True) + NORM_EPS) * gain
        partner = jnp.where(even, pltpu.roll(y, GQ_HEAD - 1, axis=1), pltpu.roll(y, 1, axis=1))
        return y * cosf + partner * sinf

    scale = GQ_HEAD ** -0.5 * math.log2(math.e)
    for h in range(GQ_HEADS):
        cols = slice(h * GQ_HEAD, (h + 1) * GQ_HEAD)
        q_ref[:, cols] = (norm_rope(z_ref[:, cols], gq_ref[...]) * scale).astype(BF16)
    for h in range(GQ_KV_HEADS):
        cols = slice(h * GQ_HEAD, (h + 1) * GQ_HEAD)
        zc = slice(GQ_WIDTH + h * GQ_HEAD, GQ_WIDTH + (h + 1) * GQ_HEAD)
        k_ref[:, cols] = norm_rope(z_ref[:, zc], gk_ref[...]).astype(BF16)
        vz = slice(GQ_WIDTH + GQ_KV_WIDTH + h * GQ_HEAD, GQ_WIDTH + GQ_KV_WIDTH + (h + 1) * GQ_HEAD)
        v_ref[:, 2 * h * GQ_HEAD:(2 * h + 1) * GQ_HEAD] = z_ref[:, vz].astype(BF16)
        v_ref[:, (2 * h + 1) * GQ_HEAD:(2 * h + 2) * GQ_HEAD] = jnp.ones((GRP, GQ_HEAD), BF16)


def _gqa_prep(z, cosf, sinf, gq, gk, gps):
    M = z.shape[0]
    zw = GQ_WIDTH + 2 * GQ_KV_WIDTH
    return pl.pallas_call(
        _gqa_prep_kernel,
        grid=(M // GRP,),
        in_specs=[pl.BlockSpec((GRP, zw), lambda i: (i, ZQ_OFF // zw)),
                  pl.BlockSpec((GRP, GQ_HEAD), lambda i: (i % gps, 0)),
                  pl.BlockSpec((GRP, GQ_HEAD), lambda i: (i % gps, 0)),
                  pl.BlockSpec((1, GQ_HEAD), lambda i: (0, 0)),
                  pl.BlockSpec((1, GQ_HEAD), lambda i: (0, 0))],
        out_specs=[pl.BlockSpec((GRP, GQ_WIDTH), lambda i: (i, 0)),
                   pl.BlockSpec((GRP, GQ_KV_WIDTH), lambda i: (i, 0)),
                   pl.BlockSpec((GRP, 2 * GQ_KV_WIDTH), lambda i: (i, 0))],
        out_shape=[jax.ShapeDtypeStruct((M, GQ_WIDTH), BF16),
                   jax.ShapeDtypeStruct((M, GQ_KV_WIDTH), BF16),
                   jax.ShapeDtypeStruct((M, 2 * GQ_KV_WIDTH), BF16)],
        compiler_params=_cparams(("parallel",)),
        name="gqa_prep",
    )(z, cosf, sinf, gq.reshape(1, GQ_HEAD), gk.reshape(1, GQ_HEAD))


def _gqa_attn_kernel(q_ref, k_ref, v_ref, o_ref, *, ctx_tiles, ctx_len, n_keys):
    qi = pl.program_id(2)

    def attend(nk):
        k = k_ref[0, 0:nk, :]
        v = v_ref[0, 0:nk, :]

        def scores(h):
            return _dot_nt(q_ref[0, :, h * GQ_HEAD:(h + 1) * GQ_HEAD], k)

        def finish(h, p):
            ol = _dot(p, v)
            o = ol[:, :GQ_HEAD] / ol[:, GQ_HEAD:GQ_HEAD + 1]
            o_ref[0, :, h * GQ_HEAD:(h + 1) * GQ_HEAD] = o.astype(o_ref.dtype)

        s = scores(0)
        for h in range(GQ_GROUP):
            s_next = scores(h + 1) if h + 1 < GQ_GROUP else None
            p = jnp.exp2(s - jnp.max(s, axis=-1, keepdims=True)).astype(BF16)
            finish(h, p)
            s = s_next

    @pl.when(qi < ctx_tiles)
    def _():
        attend(ctx_len)

    @pl.when(qi >= ctx_tiles)
    def _():
        attend(n_keys)


def _gqa_attn(q, k, v, ctx_len, tq=256):
    B, T, _ = q.shape
    gw = GQ_GROUP * GQ_HEAD
    kern = functools.partial(_gqa_attn_kernel, ctx_tiles=ctx_len // tq, ctx_len=ctx_len, n_keys=T)
    return pl.pallas_call(
        kern,
        grid=(B, GQ_KV_HEADS, T // tq),
        in_specs=[pl.BlockSpec((1, tq, gw), lambda b, g, i: (b, i, g)),
                  pl.BlockSpec((1, T, GQ_HEAD), lambda b, g, i: (b, 0, g)),
                  pl.BlockSpec((1, T, 2 * GQ_HEAD), lambda b, g, i: (b, 0, g))],
        out_specs=pl.BlockSpec((1, tq, gw), lambda b, g, i: (b, i, g)),
        out_shape=jax.ShapeDtypeStruct((B, T, GQ_WIDTH), BF16),
        compiler_params=_cparams(("parallel", "parallel", "arbitrary")),
        name="gqa_attn",
    )(q, k, v)


def _na_kernel(q_ref, k_ref, v_ref, bias_ref, o_ref, qs_ref, ks_ref, vs_ref, *, ctx_len, rows):
    scale = NA_HEAD ** -0.5
    wr = min(NA_WIN_ROWS, rows)
    nwin = wr * GRID_W
    rpg = math.gcd(rows, 8)

    for hh in range(2):
        cols = slice(hh * NA_HEAD, (hh + 1) * NA_HEAD)
        qs_ref[hh] = (q_ref[0, :, cols] * scale).astype(BF16)
        ks_ref[hh] = k_ref[0, :, cols].astype(BF16)
        vs_ref[hh] = v_ref[0, :, cols].astype(BF16)

    for hh in range(2):
        cols = slice(hh * NA_HEAD, (hh + 1) * NA_HEAD)
        kc = ks_ref[hh, 0:ctx_len, :]
        vc = vs_ref[hh, 0:ctx_len, :]

        s = _dot_nt(qs_ref[hh, 0:ctx_len, :], kc)
        p = jnp.exp(s - jnp.max(s, axis=-1, keepdims=True))
        o = _dot(p.astype(BF16), vc) / jnp.sum(p, axis=-1, keepdims=True)
        o_ref[0, 0:ctx_len, cols] = o.astype(o_ref.dtype)

        def scores(g):
            i0 = g * rpg
            qoff = pl.multiple_of(ctx_len + i0 * GRID_W, rpg * GRID_W)
            qg = qs_ref[hh, pl.ds(qoff, rpg * GRID_W), :]
            s2 = _dot_nt(qg, kc)
            koffs, s1 = [], []
            for r in range(rpg):
                rs = jnp.clip(i0 + r - wr // 2, 0, rows - wr)
                koffs.append(pl.multiple_of(ctx_len + rs * GRID_W, GRID_W))
                kw = ks_ref[hh, pl.ds(koffs[r], nwin), :]
                s1.append(_dot_nt(qg[r * GRID_W:(r + 1) * GRID_W], kw) + bias_ref[hh, i0 + r - rs])
            return qoff, koffs, jnp.concatenate(s1, axis=0), s2

        def finish(qoff, koffs, s1, s2):
            m = jnp.maximum(jnp.max(s1, axis=-1, keepdims=True), jnp.max(s2, axis=-1, keepdims=True))
            p1 = jnp.exp(s1 - m)
            p2 = jnp.exp(s2 - m)
            l = jnp.sum(p1, axis=-1, keepdims=True) + jnp.sum(p2, axis=-1, keepdims=True)
            p1 = p1.astype(BF16)
            o1 = [_dot(p1[r * GRID_W:(r + 1) * GRID_W], vs_ref[hh, pl.ds(koffs[r], nwin), :]) for r in range(rpg)]
            o = (jnp.concatenate(o1, axis=0) + _dot(p2.astype(BF16), vc)) / l
            o_ref[0, pl.ds(qoff, rpg * GRID_W), cols] = o.astype(o_ref.dtype)

        ngrp = rows // rpg
        per_iter = math.gcd(ngrp, 4)

        def groups_body(it, carry):
            cur = scores(it * per_iter)
            for u in range(per_iter):
                nxt = scores(it * per_iter + u + 1) if u + 1 < per_iter else None
                finish(*cur)
                cur = nxt
            return carry

        lax.fori_loop(0, ngrp // per_iter, groups_body, 0)


def _na_bias_table(rpb, rows):
    wr = min(NA_WIN_ROWS, rows)
    cols = np.arange(GRID_W)
    cstart = np.clip(cols - NA_WIN_COLS // 2, 0, GRID_W - NA_WIN_COLS)
    c = np.arange(GRID_W)
    inside = (c[None, :] >= cstart[:, None]) & (c[None, :] < cstart[:, None] + NA_WIN_COLS)
    col_off = np.clip(c[None, :] - cols[:, None] + (NA_WIN_COLS - 1), 0, 2 * NA_WIN_COLS - 2)
    onehot = (np.arange(2 * NA_WIN_COLS - 1)[:, None, None] == col_off[None]).astype(np.float32)
    tc = jnp.einsum('...ro,ojc->...rjc', rpb, onehot, precision=lax.Precision.HIGHEST)
    tc = jnp.where(inside, tc, NEG_BIG)
    t = jnp.stack([tc[..., NA_WIN_ROWS - 1 - d:NA_WIN_ROWS - 1 - d + wr, :, :] for d in range(wr)], axis=-4)
    t = jnp.swapaxes(t, -3, -2)
    return t.reshape(rpb.shape[:-2] + (wr, GRID_W, wr * GRID_W))


def _na_attn(z, bias, l, ctx_len):
    B, T, _ = z.shape
    rows = (T - ctx_len) // GRID_W
    wr = bias.shape[2]
    pw = 2 * NA_HEAD
    c0 = ZN_OFF // pw
    per = NA_WIDTH // pw
    scr = pltpu.VMEM((2, T, NA_HEAD), BF16)
    return pl.pallas_call(
        functools.partial(_na_kernel, ctx_len=ctx_len, rows=rows),
        grid=(B, per),
        in_specs=[pl.BlockSpec((1, T, pw), lambda b, h: (b, 0, c0 + h)),
                  pl.BlockSpec((1, T, pw), lambda b, h: (b, 0, c0 + per + h)),
                  pl.BlockSpec((1, T, pw), lambda b, h: (b, 0, c0 + 2 * per + h)),
                  pl.BlockSpec((None, 2, wr, GRID_W, wr * GRID_W), lambda b, h: (l, h, 0, 0, 0))],
        out_specs=pl.BlockSpec((1, T, pw), lambda b, h: (b, 0, h)),
        out_shape=jax.ShapeDtypeStruct((B, T, NA_WIDTH), BF16),
        scratch_shapes=[scr, scr, scr],
        compiler_params=_cparams(("parallel", "parallel")),
        name="na_attn",
    )(z, z, z, bias)


RW_PAIRS = RW_HEADS // 2
PAIR_W = 2 * RW_HEAD
PC_ROWS = 8
HALO = 16


def _head0_lanes(shape):
    return lax.broadcasted_iota(jnp.int32, shape, len(shape) - 1) < RW_HEAD


def _per_head_rows(x):
    m0 = _head0_lanes(x.shape)
    return jnp.concatenate([jnp.where(m0, x, 0.0), jnp.where(m0, 0.0, x)], axis=-2)


def _pair_diag(full):
    half = full.shape[-2] // 2
    top, bot = full[..., :half, :], full[..., half:, :]
    return jnp.where(_head0_lanes(top.shape), top, bot)


def _rw_prep_kernel(z_ref, zp_ref, zn_ref, l_ref, lp_ref, ln_ref,
                    mup_ref, mun_ref, lmup_ref, lmun_ref, kk_ref_, ka_ref, rk_ref, w0_ref, wup_ref, a0_ref,
                    aup_ref, gup_ref,
                    v_out, g_out, bonus_out, ra_out, at_out, bt_out, kt_out, pc_out, *, gps, ctx_groups, lora):
    t = pl.program_id(1)
    first = jnp.logical_or(t == 0, t == ctx_groups)
    last = jnp.logical_or(t == ctx_groups - 1, t == gps - 1)
    pv = jnp.where(first, 0.0, 1.0)
    nv = jnp.where(last, 0.0, 1.0)
    row = lax.broadcasted_iota(jnp.int32, (GRP, 1), 0)
    C = SCAN_CHUNK

    def shift(x_ref, p_ref, n_ref, mp_ref, mn_ref):
        x = x_ref[...].astype(F32)
        xp = jnp.where(row == 0, p_ref[HALO - 1:HALO, :].astype(F32) * pv, pltpu.roll(x, 1, axis=0))
        xn = jnp.where(row == GRP - 1, n_ref[0:1, :].astype(F32) * nv, pltpu.roll(x, GRP - 1, axis=0))
        return x + mp_ref[...] * (xp - x) + mn_ref[...] * (xn - x)

    zs = shift(z_ref, zp_ref, zn_ref, mup_ref, mun_ref)
    ls = shift(l_ref, lp_ref, ln_ref, lmup_ref, lmun_ref)
    r = zs[:, 0:RW_WIDTH]
    k = zs[:, RW_WIDTH:2 * RW_WIDTH]
    v = zs[:, 2 * RW_WIDTH:3 * RW_WIDTH]

    ci = lax.broadcasted_iota(jnp.int32, (PAIR_W, PAIR_W), 0) // RW_HEAD
    cj = lax.broadcasted_iota(jnp.int32, (PAIR_W, PAIR_W), 1) // RW_HEAD
    head_ones = jnp.where(ci == cj, 1.0, 0.0).astype(BF16)

    def head_sum(x):
        parts = _split3(x)
        return jnp.concatenate([sum(_dot(part[:, p * PAIR_W:(p + 1) * PAIR_W], head_ones) for part in parts)
                                for p in range(RW_PAIRS)], axis=1)

    def dot_f32(x, w):
        x1, x2, _ = _split3(x)
        w1, w2, _ = _split3(w)
        return _dot(x1, w1) + (_dot(x2, w1) + _dot(x1, w2))

    kk = k * kk_ref_[...]
    kk = kk / jnp.maximum(jnp.sqrt(head_sum(kk * kk)), 1e-12)

    ti = lax.broadcasted_iota(jnp.int32, (GRP, GRP), 0)
    tj = lax.broadcasted_iota(jnp.int32, (GRP, GRP), 1)
    same = (ti // C) == (tj // C)

    v_out[...] = v.astype(BF16)
    gd = ls[:, LORA_GD:LORA_GD + gup_ref.shape[0]]
    g_out[...] = _dot(jax.nn.sigmoid(gd).astype(BF16), gup_ref[...].astype(BF16))

    lane = lax.broadcasted_iota(jnp.int32, (1, LORA_IN), 1)
    w_lin = dot_f32(jnp.where(lane < 2 * lora, jnp.tanh(ls[:, :LORA_IN]), 0.0), wup_ref[...])
    a_lin = _dot(ls[:, LORA_IN // 2:LORA_IN // 2 + LORA_IN].astype(BF16), aup_ref[...].astype(BF16))

    kd_sum = jnp.zeros_like(k)
    for d in range(2):
        dcols = slice(d * RW_WIDTH, (d + 1) * RW_WIDTH)
        w_raw = w0_ref[d] + w_lin[:, dcols]
        lw = -math.exp(-0.5) * jax.nn.sigmoid(w_raw)
        a = jax.nn.sigmoid(a0_ref[d] + a_lin[:, dcols])
        kd = k * (1.0 + (a - 1.0) * ka_ref[...])
        kd_sum = kd_sum + kd
        tri = jnp.where(jnp.logical_and(same, (ti >= tj) if d == 0 else (ti <= tj)), 1.0, 0.0).astype(BF16)
        cs = sum(_dot(tri, part) for part in _split3(lw))
        pinv = jnp.exp(-cs)
        ra_out[d] = (r * jnp.exp(cs)).astype(BF16)
        at_out[d] = (-kk * jnp.exp(cs - lw)).astype(BF16)
        bt_out[d] = (kk * a * pinv).astype(BF16)
        kt_out[d] = (kd * pinv).astype(BF16)
        for c in range(GRP // C):
            end = c * C + (C - 1 if d == 0 else 0)
            pc_out[d, c * PC_ROWS:(c + 1) * PC_ROWS, :] = jnp.broadcast_to(jnp.exp(cs[end:end + 1, :]),
                                                                          (PC_ROWS, RW_WIDTH))
    bonus_out[...] = head_sum(r * kd_sum * rk_ref[...]) * v


def _rw_prep(z, p, B, gps, ctx_groups, lora):
    M = z.shape[0]
    nb8 = M // HALO
    rb = GRP // HALO
    zc = ZRKV_OFF // 1536
    lc = ZL_OFF // ZL_W
    pcr = GRP // SCAN_CHUNK * PC_ROWS

    def cur(c):
        return lambda b, t: (b * gps + t, c)

    def prev(c):
        return lambda b, t: (jnp.maximum((b * gps + t) * rb - 1, 0), c)

    def nxt(c):
        return lambda b, t: (jnp.minimum((b * gps + t + 1) * rb, nb8 - 1), c)

    def const(shape):
        nd = len(shape)
        return pl.BlockSpec(shape, lambda b, t: (0,) * nd)

    tok = pl.BlockSpec((GRP, RW_WIDTH), lambda b, t: (b * gps + t, 0))
    tok2 = pl.BlockSpec((2, GRP, RW_WIDTH), lambda b, t: (0, b * gps + t, 0))
    s1 = jax.ShapeDtypeStruct((M, RW_WIDTH), F32)
    s2 = jax.ShapeDtypeStruct((2, M, RW_WIDTH), BF16)
    return pl.pallas_call(
        functools.partial(_rw_prep_kernel, gps=gps, ctx_groups=ctx_groups, lora=lora),
        grid=(B, gps),
        in_specs=[pl.BlockSpec((GRP, 1536), cur(zc)),
                  pl.BlockSpec((HALO, 1536), prev(zc)),
                  pl.BlockSpec((HALO, 1536), nxt(zc)),
                  pl.BlockSpec((GRP, ZL_W), cur(lc)),
                  pl.BlockSpec((HALO, ZL_W), prev(lc)),
                  pl.BlockSpec((HALO, ZL_W), nxt(lc)),
                  const((1, 1536)), const((1, 1536)), const((1, ZL_W)), const((1, ZL_W)),
                  const((1, RW_WIDTH)), const((1, RW_WIDTH)), const((1, RW_WIDTH)),
                  const((2, 1, RW_WIDTH)), const((LORA_IN, 2 * RW_WIDTH)),
                  const((2, 1, RW_WIDTH)), const((LORA_IN, 2 * RW_WIDTH)),
                  const(p['g_up'].shape)],
        out_specs=[tok, tok, tok, tok2, tok2, tok2, tok2,
                   pl.BlockSpec((2, pcr, RW_WIDTH), lambda b, t: (0, b * gps + t, 0))],
        out_shape=[jax.ShapeDtypeStruct((M, RW_WIDTH), BF16), s1, s1, s2, s2, s2, s2,
                   jax.ShapeDtypeStruct((2, M // GRP * pcr, RW_WIDTH), F32)],
        compiler_params=_cparams(("parallel", "parallel")),
        name="rw_prep",
    )(z, z, z, z, z, z, p['mu_p'], p['mu_n'], p['lmu_p'], p['lmu_n'], p['k_k'], p['k_a'], p['r_k'],
      p['w0'], p['w_up'], p['a0'], p['a_up'], p['g_up'])


def _bmm(a, b, nt=False):
    a = a.astype(BF16)
    b = b.astype(BF16)
    out = []
    for h in range(a.shape[0]):
        out.append(_dot_nt(a[h], b[h]) if nt else _dot(a[h], b[h]))
    return jnp.stack(out, axis=0)


def _bmm_tn(a, b):
    a = a.astype(BF16)
    b = b.astype(BF16)
    out = []
    for h in range(a.shape[0]):
        out.append(_dot(a[h], b[h], (((0,), (0,)), ((), ()))))
    return jnp.stack(out, axis=0)


def _rw_intra_kernel(ra_ref, at_ref, bt_ref, kt_ref, v_ref, pc_ref, rh_ref, o0_ref, gm_ref, sd_ref, *, nck):
    d = pl.program_id(1)
    C, P = SCAN_CHUNK, RW_PAIRS

    def pairs(x):
        x = x.reshape(nck, -1, RW_WIDTH)
        return jnp.concatenate([x[:, :, p * PAIR_W:(p + 1) * PAIR_W] for p in range(P)], axis=0)

    def unpairs(y, like):
        rows = y.shape[1] * nck
        y = jnp.concatenate([y[p * nck:(p + 1) * nck].reshape(rows, PAIR_W) for p in range(P)], axis=1)
        return y.reshape(like.shape[1:]).astype(like.dtype)

    ti = lax.broadcasted_iota(jnp.int32, (C, PAIR_W), 0)
    tj = lax.broadcasted_iota(jnp.int32, (C, PAIR_W), 1) % C
    before = (ti - tj) * jnp.where(d == 0, 1, -1)
    strict = (before > 0)[None]
    incl = (before >= 0)[None]
    eye = (ti == tj)[None]

    ra = pairs(ra_ref[0])
    at = pairs(at_ref[0])
    bt = pairs(bt_ref[0])
    kt = pairs(kt_ref[0])
    v = pairs(v_ref[...])
    pc = pairs(pc_ref[0])[:, 0:1, :]

    lhs = jnp.concatenate([at, ra], axis=1)
    rhs = jnp.concatenate([_per_head_rows(bt), _per_head_rows(kt)], axis=1)
    gmat = _bmm(lhs, rhs, nt=True)
    a_ab = jnp.where(strict, gmat[:, :C, :PAIR_W], 0.0)
    a_ak = jnp.where(strict, gmat[:, :C, PAIR_W:], 0.0)
    a_rb = jnp.where(incl, gmat[:, C:, :PAIR_W], 0.0)
    a_rk = jnp.where(incl, gmat[:, C:, PAIR_W:], 0.0)

    tm = jnp.where(eye, 1.0, 0.0) + a_ab
    pw = a_ab
    for _ in range(int(math.log2(C)) - 1):
        pw = _bmm(pw, _per_head_rows(pw))
        tm = tm + _bmm(tm, _per_head_rows(pw))

    vrows = _per_head_rows(v)
    x0 = _bmm(a_ak, vrows)
    au = _bmm(tm, jnp.concatenate([_per_head_rows(at), _per_head_rows(x0)], axis=2))
    ah, u0 = au[:, :, :PAIR_W], au[:, :, PAIR_W:]
    upper = jnp.concatenate([_per_head_rows(ah), _per_head_rows(u0)], axis=2)
    lower = jnp.concatenate([jnp.zeros_like(vrows), vrows], axis=2)
    ro = _bmm(jnp.concatenate([a_rb, a_rk], axis=2),
              jnp.concatenate([upper, lower], axis=1))
    bh = bt * pc
    kh = kt * pc
    rh_ref[0] = unpairs(ra + ro[:, :, :PAIR_W], rh_ref)
    o0_ref[0] = unpairs(ro[:, :, PAIR_W:], o0_ref)
    gm_ref[0] = unpairs(_pair_diag(_bmm_tn(ah, bh)), gm_ref)
    sd_ref[0] = unpairs(_pair_diag(_bmm_tn(jnp.concatenate([u0, v], axis=1), jnp.concatenate([bh, kh], axis=1))),
                        sd_ref)


def _rw_seq_kernel(rhf_ref, o0f_ref, gmf_ref, sdf_ref, pcf_ref, rhb_ref, o0b_ref, gmb_ref, sdb_ref, pcb_ref,
                   of_ref, ob_ref, s_ref, *, nck):
    C = SCAN_CHUNK

    @pl.when(pl.program_id(0) == 0)
    def _():
        s_ref[...] = jnp.zeros(s_ref.shape, F32)

    dirs = ((rhf_ref, o0f_ref, gmf_ref, sdf_ref, pcf_ref, of_ref, range(nck)),
            (rhb_ref, o0b_ref, gmb_ref, sdb_ref, pcb_ref, ob_ref, range(nck - 1, -1, -1)))
    chains = [(d, b, p) for d in range(2) for b in range(s_ref.shape[1]) for p in range(RW_PAIRS)]
    state = {ch: s_ref[ch] for ch in chains}
    for step in range(nck):
        for ch in chains:
            d, b, p = ch
            rh_ref, o0_ref, gm_ref, sd_ref, pc_ref, o_ref, order = dirs[d]
            c = order[step]
            rows = slice(c * C, (c + 1) * C)
            cols = slice(p * PAIR_W, (p + 1) * PAIR_W)
            s = state[ch]
            sb = s.astype(BF16)
            o_ref[b, rows, cols] = _dot_nt(rh_ref[0, b, rows, cols], _per_head_rows(sb)) + o0_ref[0, b, rows, cols]
            state[ch] = (s * pc_ref[0, b, c * PC_ROWS:c * PC_ROWS + 1, cols]
                         + _dot(sb, _per_head_rows(gm_ref[0, b, rows, cols])) + sd_ref[0, b, rows, cols])
    for ch in chains:
        s_ref[ch] = state[ch]


def _rw_scan(ra, at, bt, kt, v, pc, B, ctx_len, nck=4):
    M = v.shape[0]
    T = M // B
    W = RW_WIDTH
    ct = nck * SCAN_CHUNK
    nblk = T // ct
    ncb = ctx_len // ct
    pcr = nck * PC_ROWS
    assert ctx_len % ct == 0 and T % ct == 0

    def view(x, rows):
        return x.reshape(2, B, rows, W)

    bb = 2 if B % 2 == 0 else 1
    per_dir = pl.BlockSpec((1, bb, ct, W), lambda b, d, j: (d, b, j, 0))
    outs = [jax.ShapeDtypeStruct((2, B, T, W), BF16), jax.ShapeDtypeStruct((2, B, T, W), F32),
            jax.ShapeDtypeStruct((2, B, T, W), BF16), jax.ShapeDtypeStruct((2, B, T, W), F32)]
    pcv = view(pc, T // SCAN_CHUNK * PC_ROWS)
    big = pl.pallas_call(
        functools.partial(_rw_intra_kernel, nck=bb * nck),
        grid=(B // bb, 2, nblk),
        in_specs=[per_dir, per_dir, per_dir, per_dir,
                  pl.BlockSpec((bb, ct, W), lambda b, d, j: (b, j, 0)),
                  pl.BlockSpec((1, bb, pcr, W), lambda b, d, j: (d, b, j, 0))],
        out_specs=[per_dir, per_dir, per_dir, per_dir],
        out_shape=outs,
        compiler_params=_cparams(("parallel", "parallel", "parallel")),
        name="rw_intra",
    )(view(ra, T), view(at, T), view(bt, T), view(kt, T), v.reshape(B, T, W), pcv)

    def bwd_blk(j):
        return jnp.where(j < ncb, ncb - 1 - j, nblk - 1 - (j - ncb))

    f_big = pl.BlockSpec((1, B, ct, W), lambda j: (0, 0, j, 0))
    b_big = pl.BlockSpec((1, B, ct, W), lambda j: (1, 0, bwd_blk(j), 0))
    f_pc = pl.BlockSpec((1, B, pcr, W), lambda j: (0, 0, j, 0))
    b_pc = pl.BlockSpec((1, B, pcr, W), lambda j: (1, 0, bwd_blk(j), 0))
    out = jax.ShapeDtypeStruct((B, T, W), F32)
    return pl.pallas_call(
        functools.partial(_rw_seq_kernel, nck=nck),
        grid=(nblk,),
        in_specs=[f_big, f_big, f_big, f_big, f_pc, b_big, b_big, b_big, b_big, b_pc],
        out_specs=[pl.BlockSpec((B, ct, W), lambda j: (0, j, 0)),
                   pl.BlockSpec((B, ct, W), lambda j: (0, bwd_blk(j), 0))],
        out_shape=[out, out],
        scratch_shapes=[pltpu.VMEM((2, B, RW_PAIRS, RW_HEAD, PAIR_W), F32)],
        compiler_params=_cparams(("arbitrary",)),
        name="rw_seq",
    )(*big, pcv, *big, pcv)


def _rw_out_kernel(of_ref, ob_ref, g_ref, bonus_ref, lnw_ref, lnb_ref, y_ref):
    ci = lax.broadcasted_iota(jnp.int32, (PAIR_W, PAIR_W), 0) // RW_HEAD
    cj = lax.broadcasted_iota(jnp.int32, (PAIR_W, PAIR_W), 1) // RW_HEAD
    head_mean = jnp.where(ci == cj, 1.0 / RW_HEAD, 0.0).astype(BF16)

    def mean(x):
        return sum(_dot(part, head_mean) for part in _split3(x))

    for p in range(RW_PAIRS):
        cols = slice(p * PAIR_W, (p + 1) * PAIR_W)
        wkv = of_ref[:, cols] + ob_ref[:, cols]
        cen = wkv - mean(wkv)
        y = cen * lax.rsqrt(mean(cen * cen) + RW_GN_EPS) * lnw_ref[:, cols] + lnb_ref[:, cols] + bonus_ref[:, cols]
        y_ref[:, cols] = (y * g_ref[:, cols]).astype(y_ref.dtype)


def _rw_out(o_f, o_b, g, bonus, lnw, lnb):
    M, W = g.shape
    tm = _tile(M, 512)
    tok = pl.BlockSpec((tm, W), lambda i: (i, 0))
    par = pl.BlockSpec((1, W), lambda i: (0, 0))
    return pl.pallas_call(
        _rw_out_kernel,
        grid=(M // tm,),
        in_specs=[tok, tok, tok, tok, par, par],
        out_specs=tok,
        out_shape=jax.ShapeDtypeStruct((M, W), BF16),
        compiler_params=_cparams(("parallel",)),
        name="rw_out",
    )(o_f, o_b, g, bonus, lnw.reshape(1, W), lnb.reshape(1, W))


def _pad_cols(w, n):
    return jnp.pad(w, [(0, 0)] * (w.ndim - 1) + [(0, n - w.shape[-1])])


def _in_layout(w, lw):
    o = 1536 + lw
    parts = [w[..., o + 3072:], w[..., 0:1536], w[..., o:o + 3072], _pad_cols(w[..., 1536:o], ZL_W)]
    return jnp.concatenate([x.astype(BF16) for x in parts], axis=-1)


def _shift_layout(mu, lw):
    return mu[..., 0:1536], _pad_cols(mu[..., 1536:1536 + lw], ZL_W)


def _lora_up(up, row0, rows):
    lora, w = up.shape[2], up.shape[3]
    parts = [jnp.pad(up[:, d], ((0, 0), (row0 + d * lora, rows - row0 - (d + 1) * lora), (d * w, (1 - d) * w)))
             for d in range(2)]
    return parts[0] + parts[1]


def _tile(n, pref):
    t = pref
    while n % t:
        t //= 2
    return t


def _axial_tables(ctx_len, seq):
    n_freq = GQ_HEAD // 4
    inv = ROPE_THETA ** (-jnp.arange(n_freq, dtype=F32) / n_freq)
    t = jnp.arange(seq, dtype=jnp.int32)
    row = (t // GRID_W).astype(F32)
    col = (t % GRID_W).astype(F32)
    ang = jnp.concatenate([row[:, None] * inv, col[:, None] * inv], axis=-1)
    cos, sin = jnp.cos(ang), jnp.sin(ang)
    cosf = jnp.concatenate([jnp.ones((ctx_len, GQ_HEAD), F32), jnp.repeat(cos, 2, axis=-1)], axis=0)
    sinf = jnp.concatenate([jnp.zeros((ctx_len, GQ_HEAD), F32),
                            jnp.stack([-sin, sin], axis=-1).reshape(seq, GQ_HEAD)], axis=0)
    return cosf, sinf


def kernel(x, c, ctx, c_ctx, ada_w, ada_b, norm1, norm2, w_in, rw_mu_prev, rw_mu_next, rw_w0, rw_w_up, rw_a0, rw_a_up, rw_g_up, rw_k_k, rw_k_a, rw_r_k, rw_ln_w, rw_ln_b, gq_q_norm, gq_k_norm, na_rpb, w_br_a, w_br_b, w_br_c, w_out, ffn_w1, ffn_w3, ffn_w2, final_norm):
    B, S, D = x.shape
    C = ctx.shape[1]
    L = ada_w.shape[0]
    T = C + S
    M = B * T
    gps = T // GRP
    cgr = C // GRP
    lora = rw_w_up.shape[2]
    assert C % GRP == 0 and S % GRP == 0 and S % GRID_W == 0
    lw = 4 * lora + rw_g_up.shape[1]
    assert w_in.shape[2] == 1536 + lw + 1536 + 1536 + 3 * D
    assert lw <= ZL_W and 2 * lora <= LORA_IN and 2 * lora >= LORA_IN // 2 and 4 * lora == LORA_GD
    tm = _tile(M, 1024)
    tm_in = next(t for t in (2176, 1024, 512, 256) if M % t == 0)

    w_in_p = _in_layout(w_in, lw)
    mu_p, lmu_p = _shift_layout(rw_mu_prev, lw)
    mu_n, lmu_n = _shift_layout(rw_mu_next, lw)
    w_up = _lora_up(rw_w_up, 0, LORA_IN)
    a_up = _lora_up(rw_a_up, 2 * lora - LORA_IN // 2, LORA_IN)
    gq, gk = gq_q_norm, gq_k_norm
    wa = w_br_a.astype(BF16)
    wb = w_br_b.astype(BF16)
    wc = w_br_c.astype(BF16)
    wo = w_out.astype(BF16)
    w1 = ffn_w1.astype(BF16)
    w3 = ffn_w3.astype(BF16)
    w2 = ffn_w2.astype(BF16)
    cosf, sinf = _axial_tables(C, S)
    na_bias = _na_bias_table(na_rpb, S // GRID_W)

    cc = jnp.concatenate([c, c_ctx[None, :], jnp.zeros((-(B + 1) % 8, D), F32)], axis=0)
    mod = _ada_mod(cc, ada_w, ada_b)
    modg = jnp.concatenate([jnp.broadcast_to(mod[:, B, None, None, :], (L, B, cgr, 6 * D)),
                            jnp.broadcast_to(mod[:, :B, None, :], (L, B, gps - cgr, 6 * D))], axis=2)
    modg = modg.reshape(L, B * gps, 1, 6 * D)

    xt = jnp.concatenate([ctx, x], axis=1).reshape(M, D)

    for l in range(L):
        mg = modg[l]
        h = _modnorm(xt, norm1[l], mg, 0, 1)
        z = _matmul(h, w_in_p, l, tm_in, 768, BF16, "in_proj")

        pr = {'mu_p': mu_p[l][None], 'mu_n': mu_n[l][None], 'lmu_p': lmu_p[l][None], 'lmu_n': lmu_n[l][None],
              'k_k': rw_k_k[l][None], 'k_a': rw_k_a[l][None], 'r_k': rw_r_k[l].reshape(1, RW_WIDTH),
              'w0': rw_w0[l][:, None, :], 'w_up': w_up[l], 'a0': rw_a0[l][:, None, :], 'a_up': a_up[l],
              'g_up': rw_g_up[l]}
        v_, g_, bonus_, ra_, at_, bt_, kt_, pc_ = _rw_prep(z, pr, B, gps, cgr, lora)
        of_, ob_ = _rw_scan(ra_, at_, bt_, kt_, v_, pc_, B, C)
        ya = _rw_out(of_.reshape(M, RW_WIDTH), ob_.reshape(M, RW_WIDTH), g_, bonus_, rw_ln_w[l], rw_ln_b[l])

        q, k, v = _gqa_prep(z, cosf, sinf, gq[l], gk[l], gps)
        yb = _gqa_attn(q.reshape(B, T, GQ_WIDTH), k.reshape(B, T, GQ_KV_WIDTH), v.reshape(B, T, 2 * GQ_KV_WIDTH), C)
        yb = yb.reshape(M, GQ_WIDTH)

        yc = _na_attn(z.reshape(B, T, Z_W), na_bias, l, C).reshape(M, NA_WIDTH)

        m = _merge(ya, yb, yc, wa, wb, wc, l, z, tm, 1024)
        xt = _matmul_res(m, wo, l, xt, mg, 2, tm, 1024, "out_proj")
        h = _modnorm(xt, norm2[l], mg, 3, 4)
        u = _ffn_up(h, w1, w3, l, tm, 512)
        xt = _matmul_res(u, w2, l, xt, mg, 5, tm, 512, "ffn_down")

    return _final_norm(xt.reshape(B, T, D), final_norm, C)
```

```python
import functools
import math

import numpy as np
import jax
import jax.numpy as jnp
from jax import lax
from jax.experimental import pallas as pl
from jax.experimental.pallas import tpu as pltpu

F32 = jnp.float32
BF16 = jnp.bfloat16

GRP = 256
GRID_W = 64
NORM_EPS = 1e-6

RW_HEAD = 64
RW_HEADS = 8
RW_WIDTH = RW_HEAD * RW_HEADS
LORA_IN = 256
LORA_GD = 384
RW_GN_EPS = 64e-5
SCAN_CHUNK = 64

GQ_HEAD = 128
GQ_HEADS = 8
GQ_KV_HEADS = 2
GQ_GROUP = GQ_HEADS // GQ_KV_HEADS
GQ_WIDTH = GQ_HEAD * GQ_HEADS
GQ_KV_WIDTH = GQ_HEAD * GQ_KV_HEADS
ROPE_THETA = 10000.0

NA_HEAD = 64
NA_HEADS = 8
NA_WIDTH = NA_HEAD * NA_HEADS
NA_WIN_ROWS = 8
NA_WIN_COLS = 16
NEG_BIG = -1e30

VMEM_LIMIT = 56 * 1024 * 1024

ZG_OFF, ZG_W = 0, 6144
ZRKV_OFF = 6144
ZQ_OFF = ZRKV_OFF + 1536
ZN_OFF = ZQ_OFF + 1536
ZL_OFF = ZN_OFF + 1536
ZL_W = 768
Z_W = ZL_OFF + ZL_W


def _cparams(sem):
    return pltpu.CompilerParams(dimension_semantics=sem, vmem_limit_bytes=VMEM_LIMIT)


def _dot(a, b, dims=None):
    if dims is None:
        dims = (((a.ndim - 1,), (0,)), ((), ()))
    return lax.dot_general(a, b, dims, preferred_element_type=F32)


def _dot_nt(a, b):
    return _dot(a, b, (((1,), (1,)), ((), ())))


def _split3(x):
    x1 = x.astype(BF16)
    r1 = x - x1.astype(F32)
    x2 = r1.astype(BF16)
    x3 = (r1 - x2.astype(F32)).astype(BF16)
    return x1, x2, x3


def _ada_kernel(c_ref, w_ref, b_ref, o_ref):
    c = c_ref[...]
    a = (c * jax.nn.sigmoid(c)).astype(BF16)
    o_ref[0] = _dot(a, w_ref[0].astype(BF16)) + b_ref[0]


def _ada_mod(cc, ada_w, ada_b):
    L, D, N = ada_w.shape
    R = cc.shape[0]
    tn = 1536
    return pl.pallas_call(
        _ada_kernel,
        grid=(L, N // tn),
        in_specs=[pl.BlockSpec((R, D), lambda l, j: (0, 0)),
                  pl.BlockSpec((1, D, tn), lambda l, j: (l, 0, j)),
                  pl.BlockSpec((1, 1, tn), lambda l, j: (l, 0, j))],
        out_specs=pl.BlockSpec((1, R, tn), lambda l, j: (l, 0, j)),
        out_shape=jax.ShapeDtypeStruct((L, R, N), F32),
        compiler_params=_cparams(("parallel", "parallel")),
        name="ada_mod",
    )(cc, ada_w, ada_b.reshape(L, 1, N))


def _modnorm_kernel(x_ref, nw_ref, sh_ref, sc_ref, h_ref, *, groups):
    for g in range(groups):
        rows = pl.ds(g * GRP, GRP)
        x = x_ref[rows, :]
        y = x * lax.rsqrt(jnp.mean(x * x, axis=-1, keepdims=True) + NORM_EPS) * nw_ref[...]
        h_ref[rows, :] = (y * (1.0 + sc_ref[g]) + sh_ref[g]).astype(BF16)


def _modnorm(x, nw, modg, sh_blk, sc_blk):
    M, D = x.shape
    tm = _tile(M, 1024)
    groups = tm // GRP
    return pl.pallas_call(
        functools.partial(_modnorm_kernel, groups=groups),
        grid=(M // tm,),
        in_specs=[pl.BlockSpec((tm, D), lambda i: (i, 0)),
                  pl.BlockSpec((1, D), lambda i: (0, 0)),
                  pl.BlockSpec((groups, 1, D), lambda i: (i, 0, sh_blk)),
                  pl.BlockSpec((groups, 1, D), lambda i: (i, 0, sc_blk))],
        out_specs=pl.BlockSpec((tm, D), lambda i: (i, 0)),
        out_shape=jax.ShapeDtypeStruct((M, D), BF16),
        compiler_params=_cparams(("parallel",)),
        name="modnorm",
    )(x, nw.reshape(1, D), modg, modg)


def _mm_kernel(a_ref, b_ref, o_ref):
    o_ref[...] = _dot(a_ref[...], b_ref[...]).astype(o_ref.dtype)


def _layer_spec(w, l, tn):
    return pl.BlockSpec((None, w.shape[1], tn), lambda i, j: (l, 0, j))


def _matmul(a, b, l, tm, tn, out_dtype, name):
    M, K = a.shape
    N = b.shape[2]
    return pl.pallas_call(
        _mm_kernel,
        grid=(M // tm, N // tn),
        in_specs=[pl.BlockSpec((tm, K), lambda i, j: (i, 0)),
                  _layer_spec(b, l, tn)],
        out_specs=pl.BlockSpec((tm, tn), lambda i, j: (i, j)),
        out_shape=jax.ShapeDtypeStruct((M, N), out_dtype),
        compiler_params=_cparams(("parallel", "parallel")),
        name=name,
    )(a, b)


def _mm_res_kernel(a_ref, b_ref, res_ref, gate_ref, o_ref, *, groups):
    acc = _dot(a_ref[...], b_ref[...])
    for g in range(groups):
        rows = slice(g * GRP, (g + 1) * GRP)
        o_ref[rows, :] = res_ref[rows, :] + gate_ref[g] * acc[rows, :]


def _matmul_res(a, b, l, res, modg, gate_blk, tm, tn, name):
    M, K = a.shape
    N = b.shape[2]
    groups = tm // GRP
    nb = N // tn
    return pl.pallas_call(
        functools.partial(_mm_res_kernel, groups=groups),
        grid=(M // tm, nb),
        in_specs=[pl.BlockSpec((tm, K), lambda i, j: (i, 0)),
                  _layer_spec(b, l, tn),
                  pl.BlockSpec((tm, tn), lambda i, j: (i, j)),
                  pl.BlockSpec((groups, 1, tn), lambda i, j: (i, 0, gate_blk * nb + j))],
        out_specs=pl.BlockSpec((tm, tn), lambda i, j: (i, j)),
        out_shape=jax.ShapeDtypeStruct((M, N), F32),
        compiler_params=_cparams(("parallel", "parallel")),
        name=name,
    )(a, b, res, modg)


def _ffn_up_kernel(h_ref, w1_ref, w3_ref, o_ref):
    h = h_ref[...]
    a = _dot(h, w1_ref[...])
    b = _dot(h, w3_ref[...])
    o_ref[...] = (a * jax.nn.sigmoid(a) * b).astype(o_ref.dtype)


def _ffn_up(h, w1, w3, l, tm, tn):
    M, K = h.shape
    N = w1.shape[2]
    return pl.pallas_call(
        _ffn_up_kernel,
        grid=(M // tm, N // tn),
        in_specs=[pl.BlockSpec((tm, K), lambda i, j: (i, 0)),
                  _layer_spec(w1, l, tn),
                  _layer_spec(w3, l, tn)],
        out_specs=pl.BlockSpec((tm, tn), lambda i, j: (i, j)),
        out_shape=jax.ShapeDtypeStruct((M, N), BF16),
        compiler_params=_cparams(("parallel", "parallel")),
        name="ffn_up",
    )(h, w1, w3)


def _merge_kernel(ya_ref, yb_ref, yc_ref, wa_ref, wb_ref, wc_ref, ga_ref, gb_ref, gc_ref, o_ref):
    def sig(x):
        return 0.5 * jnp.tanh(0.5 * x) + 0.5

    m = sig(ga_ref[...].astype(F32)) * _dot(ya_ref[...], wa_ref[...])
    m = m + sig(gb_ref[...].astype(F32)) * _dot(yb_ref[...], wb_ref[...])
    m = m + sig(gc_ref[...].astype(F32)) * _dot(yc_ref[...], wc_ref[...])
    o_ref[...] = m.astype(o_ref.dtype)


def _merge(ya, yb, yc, wa, wb, wc, l, z, tm, tn):
    M = ya.shape[0]
    D = wa.shape[2]
    nb = D // tn
    g0 = ZG_OFF // tn
    return pl.pallas_call(
        _merge_kernel,
        grid=(M // tm, nb),
        in_specs=[pl.BlockSpec((tm, ya.shape[1]), lambda i, j: (i, 0)),
                  pl.BlockSpec((tm, yb.shape[1]), lambda i, j: (i, 0)),
                  pl.BlockSpec((tm, yc.shape[1]), lambda i, j: (i, 0)),
                  _layer_spec(wa, l, tn), _layer_spec(wb, l, tn), _layer_spec(wc, l, tn),
                  pl.BlockSpec((tm, tn), lambda i, j: (i, g0 + j)),
                  pl.BlockSpec((tm, tn), lambda i, j: (i, g0 + nb + j)),
                  pl.BlockSpec((tm, tn), lambda i, j: (i, g0 + 2 * nb + j))],
        out_specs=pl.BlockSpec((tm, tn), lambda i, j: (i, j)),
        out_shape=jax.ShapeDtypeStruct((M, D), BF16),
        compiler_params=_cparams(("parallel", "parallel")),
        name="merge",
    )(ya, yb, yc, wa, wb, wc, z, z, z)


def _final_norm_kernel(x_ref, w_ref, o_ref):
    x = x_ref[0]
    o_ref[0] = x * lax.rsqrt(jnp.mean(x * x, axis=-1, keepdims=True) + NORM_EPS) * w_ref[...]


def _final_norm(xt, w, ctx_len):
    B, T, D = xt.shape
    S = T - ctx_len
    cg = ctx_len // GRP
    return pl.pallas_call(
        _final_norm_kernel,
        grid=(B, S // GRP),
        in_specs=[pl.BlockSpec((1, GRP, D), lambda b, i: (b, cg + i, 0)),
                  pl.BlockSpec((1, D), lambda b, i: (0, 0))],
        out_specs=pl.BlockSpec((1, GRP, D), lambda b, i: (b, i, 0)),
        out_shape=jax.ShapeDtypeStruct((B, S, D), F32),
        compiler_params=_cparams(("parallel", "parallel")),
        name="final_norm",
    )(xt, w.reshape(1, D))


def _gqa_prep_kernel(z_ref, cos_ref, sin_ref, gq_ref, gk_ref, q_ref, k_ref, v_ref):
    cosf = cos_ref[...]
    sinf = sin_ref[...]

    even = lax.broadcasted_iota(jnp.int32, (1, GQ_HEAD), 1) % 2 == 0

    def norm_rope(x, gain):
        x = x.astype(F32)
        y = x * lax.rsqrt(jnp.mean(x * x, axis=-1, keepdims=True) + NORM_EPS) * gain
        partner = jnp.where(even, pltpu.roll(y, GQ_HEAD - 1, axis=1), pltpu.roll(y, 1, axis=1))
        return y * cosf + partner * sinf

    scale = GQ_HEAD ** -0.5 * math.log2(math.e)
    for h in range(GQ_HEADS):
        cols = slice(h * GQ_HEAD, (h + 1) * GQ_HEAD)
        q_ref[:, cols] = (norm_rope(z_ref[:, cols], gq_ref[...]) * scale).astype(BF16)
    for h in range(GQ_KV_HEADS):
        cols = slice(h * GQ_HEAD, (h + 1) * GQ_HEAD)
        zc = slice(GQ_WIDTH + h * GQ_HEAD, GQ_WIDTH + (h + 1) * GQ_HEAD)
        k_ref[:, cols] = norm_rope(z_ref[:, zc], gk_ref[...]).astype(BF16)
        vz = slice(GQ_WIDTH + GQ_KV_WIDTH + h * GQ_HEAD, GQ_WIDTH + GQ_KV_WIDTH + (h + 1) * GQ_HEAD)
        v_ref[:, 2 * h * GQ_HEAD:(2 * h + 1) * GQ_HEAD] = z_ref[:, vz].astype(BF16)
        v_ref[:, (2 * h + 1) * GQ_HEAD:(2 * h + 2) * GQ_HEAD] = jnp.ones((GRP, GQ_HEAD), BF16)


def _gqa_prep(z, cosf, sinf, gq, gk, gps):
    M = z.shape[0]
    zw = GQ_WIDTH + 2 * GQ_KV_WIDTH
    return pl.pallas_call(
        _gqa_prep_kernel,
        grid=(M // GRP,),
        in_specs=[pl.BlockSpec((GRP, zw), lambda i: (i, ZQ_OFF // zw)),
                  pl.BlockSpec((GRP, GQ_HEAD), lambda i: (i % gps, 0)),
                  pl.BlockSpec((GRP, GQ_HEAD), lambda i: (i % gps, 0)),
                  pl.BlockSpec((1, GQ_HEAD), lambda i: (0, 0)),
                  pl.BlockSpec((1, GQ_HEAD), lambda i: (0, 0))],
        out_specs=[pl.BlockSpec((GRP, GQ_WIDTH), lambda i: (i, 0)),
                   pl.BlockSpec((GRP, GQ_KV_WIDTH), lambda i: (i, 0)),
                   pl.BlockSpec((GRP, 2 * GQ_KV_WIDTH), lambda i: (i, 0))],
        out_shape=[jax.ShapeDtypeStruct((M, GQ_WIDTH), BF16),
                   jax.ShapeDtypeStruct((M, GQ_KV_WIDTH), BF16),
                   jax.ShapeDtypeStruct((M, 2 * GQ_KV_WIDTH), BF16)],
        compiler_params=_cparams(("parallel",)),
        name="gqa_prep",
    )(z, cosf, sinf, gq.reshape(1, GQ_HEAD), gk.reshape(1, GQ_HEAD))


def _gqa_attn_kernel(q_ref, k_ref, v_ref, o_ref, *, ctx_tiles, ctx_len, n_keys):
    qi = pl.program_id(2)

    def attend(nk):
        k = k_ref[0, 0:nk, :]
        v = v_ref[0, 0:nk, :]

        def scores(h):
            return _dot_nt(q_ref[0, :, h * GQ_HEAD:(h + 1) * GQ_HEAD], k)

        def finish(h, p):
            ol = _dot(p, v)
            o = ol[:, :GQ_HEAD] / ol[:, GQ_HEAD:GQ_HEAD + 1]
            o_ref[0, :, h * GQ_HEAD:(h + 1) * GQ_HEAD] = o.astype(o_ref.dtype)

        s = scores(0)
        for h in range(GQ_GROUP):
            s_next = scores(h + 1) if h + 1 < GQ_GROUP else None
            p = jnp.exp2(s - jnp.max(s, axis=-1, keepdims=True)).astype(BF16)
            finish(h, p)
            s = s_next

    @pl.when(qi < ctx_tiles)
    def _():
        attend(ctx_len)

    @pl.when(qi >= ctx_tiles)
    def _():
        attend(n_keys)


def _gqa_attn(q, k, v, ctx_len, tq=256):
    B, T, _ = q.shape
    gw = GQ_GROUP * GQ_HEAD
    kern = functools.partial(_gqa_attn_kernel, ctx_tiles=ctx_len // tq, ctx_len=ctx_len, n_keys=T)
    return pl.pallas_call(
        kern,
        grid=(B, GQ_KV_HEADS, T // tq),
        in_specs=[pl.BlockSpec((1, tq, gw), lambda b, g, i: (b, i, g)),
                  pl.BlockSpec((1, T, GQ_HEAD), lambda b, g, i: (b, 0, g)),
                  pl.BlockSpec((1, T, 2 * GQ_HEAD), lambda b, g, i: (b, 0, g))],
        out_specs=pl.BlockSpec((1, tq, gw), lambda b, g, i: (b, i, g)),
        out_shape=jax.ShapeDtypeStruct((B, T, GQ_WIDTH), BF16),
        compiler_params=_cparams(("parallel", "parallel", "arbitrary")),
        name="gqa_attn",
    )(q, k, v)


def _na_kernel(q_ref, k_ref, v_ref, bias_ref, o_ref, qs_ref, ks_ref, vs_ref, *, ctx_len, rows):
    scale = NA_HEAD ** -0.5 * math.log2(math.e)
    wr = min(NA_WIN_ROWS, rows)
    nwin = wr * GRID_W
    rpg = math.gcd(rows, 8)

    for hh in range(2):
        cols = slice(hh * NA_HEAD, (hh + 1) * NA_HEAD)
        qs_ref[hh] = (q_ref[0, :, cols] * scale).astype(BF16)
        ks_ref[hh] = k_ref[0, :, cols].astype(BF16)
        vs_ref[hh] = v_ref[0, :, cols].astype(BF16)

    for hh in range(2):
        cols = slice(hh * NA_HEAD, (hh + 1) * NA_HEAD)
        kc = ks_ref[hh, 0:ctx_len, :]
        vc = vs_ref[hh, 0:ctx_len, :]

        s = _dot_nt(qs_ref[hh, 0:ctx_len, :], kc)
        p = jnp.exp2(s - jnp.max(s, axis=-1, keepdims=True))
        o = _dot(p.astype(BF16), vc) / jnp.sum(p, axis=-1, keepdims=True)
        o_ref[0, 0:ctx_len, cols] = o.astype(o_ref.dtype)

        def scores(g):
            i0 = g * rpg
            qoff = pl.multiple_of(ctx_len + i0 * GRID_W, rpg * GRID_W)
            qg = qs_ref[hh, pl.ds(qoff, rpg * GRID_W), :]
            s2 = _dot_nt(qg, kc)
            koffs, s1 = [], []
            for r in range(rpg):
                rs = jnp.clip(i0 + r - wr // 2, 0, rows - wr)
                koffs.append(pl.multiple_of(ctx_len + rs * GRID_W, GRID_W))
                kw = ks_ref[hh, pl.ds(koffs[r], nwin), :]
                s1.append(_dot_nt(qg[r * GRID_W:(r + 1) * GRID_W], kw) + bias_ref[hh, i0 + r - rs])
            return qoff, koffs, jnp.concatenate(s1, axis=0), s2

        def finish(qoff, koffs, s1, s2):
            m = jnp.maximum(jnp.max(s1, axis=-1, keepdims=True), jnp.max(s2, axis=-1, keepdims=True))
            p1 = jnp.exp2(s1 - m)
            p2 = jnp.exp2(s2 - m)
            l = jnp.sum(p1, axis=-1, keepdims=True) + jnp.sum(p2, axis=-1, keepdims=True)
            p1 = p1.astype(BF16)
            o1 = [_dot(p1[r * GRID_W:(r + 1) * GRID_W], vs_ref[hh, pl.ds(koffs[r], nwin), :]) for r in range(rpg)]
            o = (jnp.concatenate(o1, axis=0) + _dot(p2.astype(BF16), vc)) / l
            o_ref[0, pl.ds(qoff, rpg * GRID_W), cols] = o.astype(o_ref.dtype)

        ngrp = rows // rpg
        per_iter = math.gcd(ngrp, 4)

        def groups_body(it, carry):
            cur = scores(it * per_iter)
            for u in range(per_iter):
                nxt = scores(it * per_iter + u + 1) if u + 1 < per_iter else None
                finish(*cur)
                cur = nxt
            return carry

        lax.fori_loop(0, ngrp // per_iter, groups_body, 0)


def _na_bias_table(rpb, rows):
    wr = min(NA_WIN_ROWS, rows)
    cols = np.arange(GRID_W)
    cstart = np.clip(cols - NA_WIN_COLS // 2, 0, GRID_W - NA_WIN_COLS)
    c = np.arange(GRID_W)
    inside = (c[None, :] >= cstart[:, None]) & (c[None, :] < cstart[:, None] + NA_WIN_COLS)
    col_off = np.clip(c[None, :] - cols[:, None] + (NA_WIN_COLS - 1), 0, 2 * NA_WIN_COLS - 2)
    onehot = (np.arange(2 * NA_WIN_COLS - 1)[:, None, None] == col_off[None]).astype(np.float32)
    tc = jnp.einsum('...ro,ojc->...rjc', rpb, onehot, precision=lax.Precision.HIGHEST)
    tc = jnp.where(inside, tc * math.log2(math.e), NEG_BIG)
    t = jnp.stack([tc[..., NA_WIN_ROWS - 1 - d:NA_WIN_ROWS - 1 - d + wr, :, :] for d in range(wr)], axis=-4)
    t = jnp.swapaxes(t, -3, -2)
    return t.reshape(rpb.shape[:-2] + (wr, GRID_W, wr * GRID_W))


def _na_attn(z, bias, l, ctx_len):
    B, T, _ = z.shape
    rows = (T - ctx_len) // GRID_W
    wr = bias.shape[2]
    pw = 2 * NA_HEAD
    c0 = ZN_OFF // pw
    per = NA_WIDTH // pw
    scr = pltpu.VMEM((2, T, NA_HEAD), BF16)
    return pl.pallas_call(
        functools.partial(_na_kernel, ctx_len=ctx_len, rows=rows),
        grid=(B, per),
        in_specs=[pl.BlockSpec((1, T, pw), lambda b, h: (b, 0, c0 + h)),
                  pl.BlockSpec((1, T, pw), lambda b, h: (b, 0, c0 + per + h)),
                  pl.BlockSpec((1, T, pw), lambda b, h: (b, 0, c0 + 2 * per + h)),
                  pl.BlockSpec((None, 2, wr, GRID_W, wr * GRID_W), lambda b, h: (l, h, 0, 0, 0))],
        out_specs=pl.BlockSpec((1, T, pw), lambda b, h: (b, 0, h)),
        out_shape=jax.ShapeDtypeStruct((B, T, NA_WIDTH), BF16),
        scratch_shapes=[scr, scr, scr],
        compiler_params=_cparams(("parallel", "parallel")),
        name="na_attn",
    )(z, z, z, bias)


RW_PAIRS = RW_HEADS // 2
PAIR_W = 2 * RW_HEAD
PC_ROWS = 8
HALO = 16


def _head0_lanes(shape):
    return lax.broadcasted_iota(jnp.int32, shape, len(shape) - 1) < RW_HEAD


def _per_head_rows(x):
    m0 = _head0_lanes(x.shape)
    return jnp.concatenate([jnp.where(m0, x, 0.0), jnp.where(m0, 0.0, x)], axis=-2)


def _pair_diag(full):
    half = full.shape[-2] // 2
    top, bot = full[..., :half, :], full[..., half:, :]
    return jnp.where(_head0_lanes(top.shape), top, bot)


def _rw_prep_kernel(z_ref, zp_ref, zn_ref, l_ref, lp_ref, ln_ref,
                    mup_ref, mun_ref, lmup_ref, lmun_ref, kk_ref_, ka_ref, rk_ref, w0_ref, wup_ref, a0_ref,
                    aup_ref, gup_ref,
                    v_out, g_out, bonus_out, ra_out, at_out, bt_out, kt_out, pc_out, *, gps, ctx_groups, lora):
    t = pl.program_id(1)
    first = jnp.logical_or(t == 0, t == ctx_groups)
    last = jnp.logical_or(t == ctx_groups - 1, t == gps - 1)
    pv = jnp.where(first, 0.0, 1.0)
    nv = jnp.where(last, 0.0, 1.0)
    row = lax.broadcasted_iota(jnp.int32, (GRP, 1), 0)
    C = SCAN_CHUNK

    def shift(x_ref, p_ref, n_ref, mp_ref, mn_ref):
        x = x_ref[...].astype(F32)
        xp = jnp.where(row == 0, p_ref[HALO - 1:HALO, :].astype(F32) * pv, pltpu.roll(x, 1, axis=0))
        xn = jnp.where(row == GRP - 1, n_ref[0:1, :].astype(F32) * nv, pltpu.roll(x, GRP - 1, axis=0))
        return x + mp_ref[...] * (xp - x) + mn_ref[...] * (xn - x)

    zs = shift(z_ref, zp_ref, zn_ref, mup_ref, mun_ref)
    ls = shift(l_ref, lp_ref, ln_ref, lmup_ref, lmun_ref)
    r = zs[:, 0:RW_WIDTH]
    k = zs[:, RW_WIDTH:2 * RW_WIDTH]
    v = zs[:, 2 * RW_WIDTH:3 * RW_WIDTH]

    ci = lax.broadcasted_iota(jnp.int32, (PAIR_W, PAIR_W), 0) // RW_HEAD
    cj = lax.broadcasted_iota(jnp.int32, (PAIR_W, PAIR_W), 1) // RW_HEAD
    head_ones = jnp.where(ci == cj, 1.0, 0.0).astype(BF16)

    def head_sum(x):
        parts = _split3(x)
        return jnp.concatenate([sum(_dot(part[:, p * PAIR_W:(p + 1) * PAIR_W], head_ones) for part in parts)
                                for p in range(RW_PAIRS)], axis=1)

    def dot_f32(x, w):
        x1, x2, _ = _split3(x)
        w1, w2, _ = _split3(w)
        return _dot(x1, w1) + (_dot(x2, w1) + _dot(x1, w2))

    kk = k * kk_ref_[...]
    kk = kk / jnp.maximum(jnp.sqrt(head_sum(kk * kk)), 1e-12)

    ti = lax.broadcasted_iota(jnp.int32, (GRP, GRP), 0)
    tj = lax.broadcasted_iota(jnp.int32, (GRP, GRP), 1)
    same = (ti // C) == (tj // C)

    v_out[...] = v.astype(BF16)
    gd = ls[:, LORA_GD:LORA_GD + gup_ref.shape[0]]
    g_out[...] = _dot(jax.nn.sigmoid(gd).astype(BF16), gup_ref[...].astype(BF16))

    lane = lax.broadcasted_iota(jnp.int32, (1, LORA_IN), 1)
    w_lin = dot_f32(jnp.where(lane < 2 * lora, jnp.tanh(ls[:, :LORA_IN]), 0.0), wup_ref[...])
    a_lin = _dot(ls[:, LORA_IN // 2:LORA_IN // 2 + LORA_IN].astype(BF16), aup_ref[...].astype(BF16))

    kd_sum = jnp.zeros_like(k)
    for d in range(2):
        dcols = slice(d * RW_WIDTH, (d + 1) * RW_WIDTH)
        w_raw = w0_ref[d] + w_lin[:, dcols]
        lw = -math.exp(-0.5) * jax.nn.sigmoid(w_raw)
        a = jax.nn.sigmoid(a0_ref[d] + a_lin[:, dcols])
        kd = k * (1.0 + (a - 1.0) * ka_ref[...])
        kd_sum = kd_sum + kd
        tri = jnp.where(jnp.logical_and(same, (ti >= tj) if d == 0 else (ti <= tj)), 1.0, 0.0).astype(BF16)
        cs = sum(_dot(tri, part) for part in _split3(lw))
        pinv = jnp.exp(-cs)
        ra_out[d] = (r * jnp.exp(cs)).astype(BF16)
        at_out[d] = (-kk * jnp.exp(cs - lw)).astype(BF16)
        bt_out[d] = (kk * a * pinv).astype(BF16)
        kt_out[d] = (kd * pinv).astype(BF16)
        for c in range(GRP // C):
            end = c * C + (C - 1 if d == 0 else 0)
            pc_out[d, c * PC_ROWS:(c + 1) * PC_ROWS, :] = jnp.broadcast_to(jnp.exp(cs[end:end + 1, :]),
                                                                          (PC_ROWS, RW_WIDTH))
    bonus_out[...] = head_sum(r * kd_sum * rk_ref[...]) * v


def _rw_prep(z, p, B, gps, ctx_groups, lora):
    M = z.shape[0]
    nb8 = M // HALO
    rb = GRP // HALO
    zc = ZRKV_OFF // 1536
    lc = ZL_OFF // ZL_W
    pcr = GRP // SCAN_CHUNK * PC_ROWS

    def cur(c):
        return lambda b, t: (b * gps + t, c)

    def prev(c):
        return lambda b, t: (jnp.maximum((b * gps + t) * rb - 1, 0), c)

    def nxt(c):
        return lambda b, t: (jnp.minimum((b * gps + t + 1) * rb, nb8 - 1), c)

    def const(shape):
        nd = len(shape)
        return pl.BlockSpec(shape, lambda b, t: (0,) * nd)

    tok = pl.BlockSpec((GRP, RW_WIDTH), lambda b, t: (b * gps + t, 0))
    tok2 = pl.BlockSpec((2, GRP, RW_WIDTH), lambda b, t: (0, b * gps + t, 0))
    s1 = jax.ShapeDtypeStruct((M, RW_WIDTH), F32)
    s2 = jax.ShapeDtypeStruct((2, M, RW_WIDTH), BF16)
    return pl.pallas_call(
        functools.partial(_rw_prep_kernel, gps=gps, ctx_groups=ctx_groups, lora=lora),
        grid=(B, gps),
        in_specs=[pl.BlockSpec((GRP, 1536), cur(zc)),
                  pl.BlockSpec((HALO, 1536), prev(zc)),
                  pl.BlockSpec((HALO, 1536), nxt(zc)),
                  pl.BlockSpec((GRP, ZL_W), cur(lc)),
                  pl.BlockSpec((HALO, ZL_W), prev(lc)),
                  pl.BlockSpec((HALO, ZL_W), nxt(lc)),
                  const((1, 1536)), const((1, 1536)), const((1, ZL_W)), const((1, ZL_W)),
                  const((1, RW_WIDTH)), const((1, RW_WIDTH)), const((1, RW_WIDTH)),
                  const((2, 1, RW_WIDTH)), const((LORA_IN, 2 * RW_WIDTH)),
                  const((2, 1, RW_WIDTH)), const((LORA_IN, 2 * RW_WIDTH)),
                  const(p['g_up'].shape)],
        out_specs=[tok, tok, tok, tok2, tok2, tok2, tok2,
                   pl.BlockSpec((2, pcr, RW_WIDTH), lambda b, t: (0, b * gps + t, 0))],
        out_shape=[jax.ShapeDtypeStruct((M, RW_WIDTH), BF16), s1, s1, s2, s2, s2, s2,
                   jax.ShapeDtypeStruct((2, M // GRP * pcr, RW_WIDTH), F32)],
        compiler_params=_cparams(("parallel", "parallel")),
        name="rw_prep",
    )(z, z, z, z, z, z, p['mu_p'], p['mu_n'], p['lmu_p'], p['lmu_n'], p['k_k'], p['k_a'], p['r_k'],
      p['w0'], p['w_up'], p['a0'], p['a_up'], p['g_up'])


def _bmm(a, b, nt=False):
    a = a.astype(BF16)
    b = b.astype(BF16)
    out = []
    for h in range(a.shape[0]):
        out.append(_dot_nt(a[h], b[h]) if nt else _dot(a[h], b[h]))
    return jnp.stack(out, axis=0)


def _bmm_tn(a, b):
    a = a.astype(BF16)
    b = b.astype(BF16)
    out = []
    for h in range(a.shape[0]):
        out.append(_dot(a[h], b[h], (((0,), (0,)), ((), ()))))
    return jnp.stack(out, axis=0)


def _rw_intra_kernel(ra_ref, at_ref, bt_ref, kt_ref, v_ref, pc_ref, rh_ref, o0_ref, gm_ref, sd_ref, *, nck):
    d = pl.program_id(1)
    C, P = SCAN_CHUNK, RW_PAIRS

    def pairs(x):
        x = x.reshape(nck, -1, RW_WIDTH)
        return jnp.concatenate([x[:, :, p * PAIR_W:(p + 1) * PAIR_W] for p in range(P)], axis=0)

    def unpairs(y, like):
        rows = y.shape[1] * nck
        y = jnp.concatenate([y[p * nck:(p + 1) * nck].reshape(rows, PAIR_W) for p in range(P)], axis=1)
        return y.reshape(like.shape[1:]).astype(like.dtype)

    ti = lax.broadcasted_iota(jnp.int32, (C, PAIR_W), 0)
    tj = lax.broadcasted_iota(jnp.int32, (C, PAIR_W), 1) % C
    before = (ti - tj) * jnp.where(d == 0, 1, -1)
    strict = (before > 0)[None]
    incl = (before >= 0)[None]
    eye = (ti == tj)[None]

    ra = pairs(ra_ref[0])
    at = pairs(at_ref[0])
    bt = pairs(bt_ref[0])
    kt = pairs(kt_ref[0])
    v = pairs(v_ref[...])
    pc = pairs(pc_ref[0])[:, 0:1, :]

    lhs = jnp.concatenate([at, ra], axis=1)
    rhs = jnp.concatenate([_per_head_rows(bt), _per_head_rows(kt)], axis=1)
    gmat = _bmm(lhs, rhs, nt=True)
    a_ab = jnp.where(strict, gmat[:, :C, :PAIR_W], 0.0)
    a_ak = jnp.where(strict, gmat[:, :C, PAIR_W:], 0.0)
    a_rb = jnp.where(incl, gmat[:, C:, :PAIR_W], 0.0)
    a_rk = jnp.where(incl, gmat[:, C:, PAIR_W:], 0.0)

    tm = jnp.where(eye, 1.0, 0.0) + a_ab
    pw = a_ab
    for _ in range(int(math.log2(C)) - 1):
        pw = _bmm(pw, _per_head_rows(pw))
        tm = tm + _bmm(tm, _per_head_rows(pw))

    vrows = _per_head_rows(v)
    x0 = _bmm(a_ak, vrows)
    au = _bmm(tm, jnp.concatenate([_per_head_rows(at), _per_head_rows(x0)], axis=2))
    ah, u0 = au[:, :, :PAIR_W], au[:, :, PAIR_W:]
    upper = jnp.concatenate([_per_head_rows(ah), _per_head_rows(u0)], axis=2)
    lower = jnp.concatenate([jnp.zeros_like(vrows), vrows], axis=2)
    ro = _bmm(jnp.concatenate([a_rb, a_rk], axis=2),
              jnp.concatenate([upper, lower], axis=1))
    bh = bt * pc
    kh = kt * pc
    rh_ref[0] = unpairs(ra + ro[:, :, :PAIR_W], rh_ref)
    o0_ref[0] = unpairs(ro[:, :, PAIR_W:], o0_ref)
    gm_ref[0] = unpairs(_pair_diag(_bmm_tn(ah, bh)), gm_ref)
    sd_ref[0] = unpairs(_pair_diag(_bmm_tn(jnp.concatenate([u0, v], axis=1), jnp.concatenate([bh, kh], axis=1))),
                        sd_ref)


def _rw_seq_kernel(rhf_ref, o0f_ref, gmf_ref, sdf_ref, pcf_ref, rhb_ref, o0b_ref, gmb_ref, sdb_ref, pcb_ref,
                   of_ref, ob_ref, s_ref, *, nck):
    C = SCAN_CHUNK

    @pl.when(pl.program_id(0) == 0)
    def _():
        s_ref[...] = jnp.zeros(s_ref.shape, F32)

    dirs = ((rhf_ref, o0f_ref, gmf_ref, sdf_ref, pcf_ref, of_ref, range(nck)),
            (rhb_ref, o0b_ref, gmb_ref, sdb_ref, pcb_ref, ob_ref, range(nck - 1, -1, -1)))
    chains = [(d, b, p) for d in range(2) for b in range(s_ref.shape[1]) for p in range(RW_PAIRS)]
    state = {ch: s_ref[ch] for ch in chains}
    for step in range(nck):
        for ch in chains:
            d, b, p = ch
            rh_ref, o0_ref, gm_ref, sd_ref, pc_ref, o_ref, order = dirs[d]
            c = order[step]
            rows = slice(c * C, (c + 1) * C)
            cols = slice(p * PAIR_W, (p + 1) * PAIR_W)
            s = state[ch]
            sb = s.astype(BF16)
            o_ref[b, rows, cols] = _dot_nt(rh_ref[0, b, rows, cols], _per_head_rows(sb)) + o0_ref[0, b, rows, cols]
            state[ch] = (s * pc_ref[0, b, c * PC_ROWS:c * PC_ROWS + 1, cols]
                         + _dot(sb, _per_head_rows(gm_ref[0, b, rows, cols])) + sd_ref[0, b, rows, cols])
    for ch in chains:
        s_ref[ch] = state[ch]


def _rw_scan(ra, at, bt, kt, v, pc, B, ctx_len, nck=4):
    M = v.shape[0]
    T = M // B
    W = RW_WIDTH
    ct = nck * SCAN_CHUNK
    nblk = T // ct
    ncb = ctx_len // ct
    pcr = nck * PC_ROWS
    assert ctx_len % ct == 0 and T % ct == 0

    def view(x, rows):
        return x.reshape(2, B, rows, W)

    bb = 2 if B % 2 == 0 else 1
    per_dir = pl.BlockSpec((1, bb, ct, W), lambda b, d, j: (d, b, j, 0))
    outs = [jax.ShapeDtypeStruct((2, B, T, W), BF16), jax.ShapeDtypeStruct((2, B, T, W), F32),
            jax.ShapeDtypeStruct((2, B, T, W), BF16), jax.ShapeDtypeStruct((2, B, T, W), F32)]
    pcv = view(pc, T // SCAN_CHUNK * PC_ROWS)
    big = pl.pallas_call(
        functools.partial(_rw_intra_kernel, nck=bb * nck),
        grid=(B // bb, 2, nblk),
        in_specs=[per_dir, per_dir, per_dir, per_dir,
                  pl.BlockSpec((bb, ct, W), lambda b, d, j: (b, j, 0)),
                  pl.BlockSpec((1, bb, pcr, W), lambda b, d, j: (d, b, j, 0))],
        out_specs=[per_dir, per_dir, per_dir, per_dir],
        out_shape=outs,
        compiler_params=_cparams(("parallel", "parallel", "parallel")),
        name="rw_intra",
    )(view(ra, T), view(at, T), view(bt, T), view(kt, T), v.reshape(B, T, W), pcv)

    def bwd_blk(j):
        return jnp.where(j < ncb, ncb - 1 - j, nblk - 1 - (j - ncb))

    f_big = pl.BlockSpec((1, B, ct, W), lambda j: (0, 0, j, 0))
    b_big = pl.BlockSpec((1, B, ct, W), lambda j: (1, 0, bwd_blk(j), 0))
    f_pc = pl.BlockSpec((1, B, pcr, W), lambda j: (0, 0, j, 0))
    b_pc = pl.BlockSpec((1, B, pcr, W), lambda j: (1, 0, bwd_blk(j), 0))
    out = jax.ShapeDtypeStruct((B, T, W), F32)
    return pl.pallas_call(
        functools.partial(_rw_seq_kernel, nck=nck),
        grid=(nblk,),
        in_specs=[f_big, f_big, f_big, f_big, f_pc, b_big, b_big, b_big, b_big, b_pc],
        out_specs=[pl.BlockSpec((B, ct, W), lambda j: (0, j, 0)),
                   pl.BlockSpec((B, ct, W), lambda j: (0, bwd_blk(j), 0))],
        out_shape=[out, out],
        scratch_shapes=[pltpu.VMEM((2, B, RW_PAIRS, RW_HEAD, PAIR_W), F32)],
        compiler_params=_cparams(("arbitrary",)),
        name="rw_seq",
    )(*big, pcv, *big, pcv)


def _rw_out_kernel(of_ref, ob_ref, g_ref, bonus_ref, lnw_ref, lnb_ref, y_ref):
    ci = lax.broadcasted_iota(jnp.int32, (PAIR_W, PAIR_W), 0) // RW_HEAD
    cj = lax.broadcasted_iota(jnp.int32, (PAIR_W, PAIR_W), 1) // RW_HEAD
    head_mean = jnp.where(ci == cj, 1.0 / RW_HEAD, 0.0).astype(BF16)

    def mean(x):
        return sum(_dot(part, head_mean) for part in _split3(x))

    for p in range(RW_PAIRS):
        cols = slice(p * PAIR_W, (p + 1) * PAIR_W)
        wkv = of_ref[:, cols] + ob_ref[:, cols]
        cen = wkv - mean(wkv)
        y = cen * lax.rsqrt(mean(cen * cen) + RW_GN_EPS) * lnw_ref[:, cols] + lnb_ref[:, cols] + bonus_ref[:, cols]
        y_ref[:, cols] = (y * g_ref[:, cols]).astype(y_ref.dtype)


def _rw_out(o_f, o_b, g, bonus, lnw, lnb):
    M, W = g.shape
    tm = _tile(M, 1024)
    tok = pl.BlockSpec((tm, W), lambda i: (i, 0))
    par = pl.BlockSpec((1, W), lambda i: (0, 0))
    return pl.pallas_call(
        _rw_out_kernel,
        grid=(M // tm,),
        in_specs=[tok, tok, tok, tok, par, par],
        out_specs=tok,
        out_shape=jax.ShapeDtypeStruct((M, W), BF16),
        compiler_params=_cparams(("parallel",)),
        name="rw_out",
    )(o_f, o_b, g, bonus, lnw.reshape(1, W), lnb.reshape(1, W))


def _pad_cols(w, n):
    return jnp.pad(w, [(0, 0)] * (w.ndim - 1) + [(0, n - w.shape[-1])])


def _in_layout(w, lw):
    o = 1536 + lw
    parts = [w[..., o + 3072:], w[..., 0:1536], w[..., o:o + 3072], _pad_cols(w[..., 1536:o], ZL_W)]
    return jnp.concatenate([x.astype(BF16) for x in parts], axis=-1)


def _shift_layout(mu, lw):
    return mu[..., 0:1536], _pad_cols(mu[..., 1536:1536 + lw], ZL_W)


def _lora_up(up, row0, rows):
    lora, w = up.shape[2], up.shape[3]
    parts = [jnp.pad(up[:, d], ((0, 0), (row0 + d * lora, rows - row0 - (d + 1) * lora), (d * w, (1 - d) * w)))
             for d in range(2)]
    return parts[0] + parts[1]


def _tile(n, pref):
    t = pref
    while n % t:
        t //= 2
    return t


def _axial_tables(ctx_len, seq):
    n_freq = GQ_HEAD // 4
    inv = ROPE_THETA ** (-jnp.arange(n_freq, dtype=F32) / n_freq)
    t = jnp.arange(seq, dtype=jnp.int32)
    row = (t // GRID_W).astype(F32)
    col = (t % GRID_W).astype(F32)
    ang = jnp.concatenate([row[:, None] * inv, col[:, None] * inv], axis=-1)
    cos, sin = jnp.cos(ang), jnp.sin(ang)
    cosf = jnp.concatenate([jnp.ones((ctx_len, GQ_HEAD), F32), jnp.repeat(cos, 2, axis=-1)], axis=0)
    sinf = jnp.concatenate([jnp.zeros((ctx_len, GQ_HEAD), F32),
                            jnp.stack([-sin, sin], axis=-1).reshape(seq, GQ_HEAD)], axis=0)
    return cosf, sinf


def kernel(x, c, ctx, c_ctx, ada_w, ada_b, norm1, norm2, w_in, rw_mu_prev, rw_mu_next, rw_w0, rw_w_up, rw_a0, rw_a_up, rw_g_up, rw_k_k, rw_k_a, rw_r_k, rw_ln_w, rw_ln_b, gq_q_norm, gq_k_norm, na_rpb, w_br_a, w_br_b, w_br_c, w_out, ffn_w1, ffn_w3, ffn_w2, final_norm):
    B, S, D = x.shape
    C = ctx.shape[1]
    L = ada_w.shape[0]
    T = C + S
    M = B * T
    gps = T // GRP
    cgr = C // GRP
    lora = rw_w_up.shape[2]
    assert C % GRP == 0 and S % GRP == 0 and S % GRID_W == 0
    lw = 4 * lora + rw_g_up.shape[1]
    assert w_in.shape[2] == 1536 + lw + 1536 + 1536 + 3 * D
    assert lw <= ZL_W and 2 * lora <= LORA_IN and 2 * lora >= LORA_IN // 2 and 4 * lora == LORA_GD
    tm = _tile(M, 1024)
    tm_in = next(t for t in (2176, 1024, 512, 256) if M % t == 0)

    w_in_p = _in_layout(w_in, lw)
    mu_p, lmu_p = _shift_layout(rw_mu_prev, lw)
    mu_n, lmu_n = _shift_layout(rw_mu_next, lw)
    w_up = _lora_up(rw_w_up, 0, LORA_IN)
    a_up = _lora_up(rw_a_up, 2 * lora - LORA_IN // 2, LORA_IN)
    gq, gk = gq_q_norm, gq_k_norm
    wa = w_br_a.astype(BF16)
    wb = w_br_b.astype(BF16)
    wc = w_br_c.astype(BF16)
    wo = w_out.astype(BF16)
    w1 = ffn_w1.astype(BF16)
    w3 = ffn_w3.astype(BF16)
    w2 = ffn_w2.astype(BF16)
    cosf, sinf = _axial_tables(C, S)
    na_bias = _na_bias_table(na_rpb, S // GRID_W)

    cc = jnp.concatenate([c, c_ctx[None, :], jnp.zeros((-(B + 1) % 8, D), F32)], axis=0)
    mod = _ada_mod(cc, ada_w, ada_b)
    modg = jnp.concatenate([jnp.broadcast_to(mod[:, B, None, None, :], (L, B, cgr, 6 * D)),
                            jnp.broadcast_to(mod[:, :B, None, :], (L, B, gps - cgr, 6 * D))], axis=2)
    modg = modg.reshape(L, B * gps, 1, 6 * D)

    xt = jnp.concatenate([ctx, x], axis=1).reshape(M, D)

    for l in range(L):
        mg = modg[l]
        h = _modnorm(xt, norm1[l], mg, 0, 1)
        z = _matmul(h, w_in_p, l, tm_in, 768, BF16, "in_proj")

        pr = {'mu_p': mu_p[l][None], 'mu_n': mu_n[l][None], 'lmu_p': lmu_p[l][None], 'lmu_n': lmu_n[l][None],
              'k_k': rw_k_k[l][None], 'k_a': rw_k_a[l][None], 'r_k': rw_r_k[l].reshape(1, RW_WIDTH),
              'w0': rw_w0[l][:, None, :], 'w_up': w_up[l], 'a0': rw_a0[l][:, None, :], 'a_up': a_up[l],
              'g_up': rw_g_up[l]}
        v_, g_, bonus_, ra_, at_, bt_, kt_, pc_ = _rw_prep(z, pr, B, gps, cgr, lora)
        of_, ob_ = _rw_scan(ra_, at_, bt_, kt_, v_, pc_, B, C)
        ya = _rw_out(of_.reshape(M, RW_WIDTH), ob_.reshape(M, RW_WIDTH), g_, bonus_, rw_ln_w[l], rw_ln_b[l])

        q, k, v = _gqa_prep(z, cosf, sinf, gq[l], gk[l], gps)
        yb = _gqa_attn(q.reshape(B, T, GQ_WIDTH), k.reshape(B, T, GQ_KV_WIDTH), v.reshape(B, T, 2 * GQ_KV_WIDTH), C)
        yb = yb.reshape(M, GQ_WIDTH)

        yc = _na_attn(z.reshape(B, T, Z_W), na_bias, l, C).reshape(M, NA_WIDTH)

        m = _merge(ya, yb, yc, wa, wb, wc, l, z, tm, 1024)
        xt = _matmul_res(m, wo, l, xt, mg, 2, tm, 1024, "out_proj")
        h = _modnorm(xt, norm2[l], mg, 3, 4)
        u = _ffn_up(h, w1, w3, l, tm, 512)
        xt = _matmul_res(u, w2, l, xt, mg, 5, tm, 512, "ffn_down")

    return _final_norm(xt.reshape(B, T, D), final_norm, C)
```
